```python
import math, functools
import jax, jax.numpy as jnp
from jax import lax
import numpy as np

D_MODEL = 1024
BATCH = 8
SEQ = 4096
DEPTH = 2
DEC_BATCH = 32
DEC_SEQ = 4
PAST_LEN = 16384
PAGE_SIZE = 128

N_MIXERS = 4
GROUP_W = D_MODEL // N_MIXERS
HEAD_DIM = 64
N_HEADS = GROUP_W // HEAD_DIM
ROPE_BASE = 10000.0
HG_DK = 128
RW_DECAY_LORA = 64
RW_A_LORA = 64
RW_GATE_LORA = 128
RW_LNX_EPS = 6.4e-4
DA_DQK = HEAD_DIM // 2
W_RET = 3 * N_HEADS * HEAD_DIM + GROUP_W
W_HG = 2 * N_HEADS * HG_DK + N_HEADS * HEAD_DIM + GROUP_W
W_RW = 3 * GROUP_W + RW_DECAY_LORA + RW_A_LORA + RW_GATE_LORA
W_DA = 2 * (N_HEADS * 2 * DA_DQK) + N_HEADS * HEAD_DIM
N_COLS = W_RET + W_HG + W_RW + W_DA
RW_SPLITS = [GROUP_W, 2 * GROUP_W, 3 * GROUP_W, 3 * GROUP_W + RW_DECAY_LORA, 3 * GROUP_W + RW_DECAY_LORA + RW_A_LORA]
CHUNK = 64
Q_BLOCK = 128
MASK_NEG = -1e30
N_EXPERTS = 64
TOP_K = 8
D_EXPERT = 256
ROUTED_SCALE = 2.5
MOE_BLOCK = 128
ALPHA = (2.0 * DEPTH) ** 0.25
BETA = (8.0 * DEPTH) ** -0.25
LN_EPS = 1e-5
NORM_EPS = 1e-5

kernel_name = 'hybrid_ret_hgrn2_rwkv7_diffattn_moe_step'


def _f32(t):
    return t.astype(jnp.float32)


def _layernorm(x, g, b):
    xf = _f32(x)
    mu = jnp.mean(xf, -1, keepdims=True)
    xc = xf - mu
    var = jnp.mean(xc * xc, -1, keepdims=True)
    return (xc * lax.rsqrt(var + LN_EPS) * _f32(g) + _f32(b)).astype(x.dtype)


def _rms(x, eps=NORM_EPS):
    return x * lax.rsqrt(jnp.mean(x * x, -1, keepdims=True) + eps)


def _head_ln(x, eps):
    xc = x - jnp.mean(x, -1, keepdims=True)
    return xc * lax.rsqrt(jnp.mean(xc * xc, -1, keepdims=True) + eps)


def _rotary(x, pos):
    half = x.shape[-1] // 2
    freq = 1.0 / (ROPE_BASE ** jnp.linspace(0.0, 1.0, half, dtype=jnp.float32))
    ang = _f32(pos)[:, None] * freq[None, :]
    cos = jnp.cos(ang)[None, :, None, :]
    sin = jnp.sin(ang)[None, :, None, :]
    x1, x2 = x[..., :half], x[..., half:]
    return jnp.concatenate([x1 * cos - x2 * sin, x1 * sin + x2 * cos], -1)


def _chunk_scan(step, xs, s0):
    B, L = xs[0].shape[:2]
    c = CHUNK if L % CHUNK == 0 else L
    n = L // c
    chunked = tuple(jnp.moveaxis(a.reshape(B, n, c, *a.shape[2:]), 1, 0) for a in xs)
    s, out = lax.scan(step, s0, chunked)
    return jnp.moveaxis(out, 0, 1).reshape(B, L, *out.shape[3:]), s


def _retention_step(S, xs, log_gamma):
    q, k, v = xs
    C = q.shape[1]
    t = jnp.arange(C, dtype=jnp.float32)
    gap = t[:, None] - t[None, :]
    dmask = jnp.where(gap >= 0, jnp.exp(jnp.maximum(gap, 0.0)[None] * log_gamma[:, None, None]), 0.0)
    scores = jnp.einsum('bqhd,bkhd->bhqk', q, k) * dmask[None]
    o = jnp.einsum('bhqk,bkhe->bqhe', scores, v)
    o = o + jnp.einsum('bqhd,bhde->bqhe', q, S) * jnp.exp((t + 1.0)[:, None] * log_gamma[None, :])[None, :, :, None]
    k_w = k * jnp.exp((C - 1.0 - t)[:, None] * log_gamma[None, :])[None, :, :, None]
    S = jnp.exp(C * log_gamma)[None, :, None, None] * S + jnp.einsum('bkhd,bkhe->bhde', k_w, v)
    return S, o


def _hgrn2_step(S, xs):
    q, k, logf, v = xs
    C = q.shape[1]
    b = jnp.cumsum(logf, axis=1)
    causal = jnp.tril(jnp.ones((C, C), bool))[None, :, :, None, None]
    diff = jnp.where(causal, b[:, :, None] - b[:, None, :], 0.0)
    decay = jnp.where(causal, jnp.exp(diff), 0.0)
    attn = jnp.einsum('bqhd,bqkhd,bkhd->bhqk', q, decay, k)
    o = jnp.einsum('bhqk,bkhe->bqhe', attn, v) + jnp.einsum('bqhd,bhde->bqhe', q * jnp.exp(b), S)
    b_last = b[:, -1]
    S = jnp.exp(b_last)[..., None] * S + jnp.einsum('bkhd,bkhe->bhde', k * jnp.exp(b_last[:, None] - b), v)
    return S, o


def _rwkv_scan(S0, r, w, k, v, a, b):
    def step(S, xs):
        r_t, w_t, k_t, v_t, a_t, b_t = xs
        S = (S * w_t[:, :, None, :]
             + jnp.einsum('bhvk,bhk->bhv', S, a_t)[..., None] * b_t[:, :, None, :]
             + v_t[..., None] * k_t[:, :, None, :])
        return S, jnp.einsum('bhvk,bhk->bhv', S, r_t)
    xs = tuple(jnp.moveaxis(t, 1, 0) for t in (r, w, k, v, a, b))
    S, y = lax.scan(step, S0, xs)
    return jnp.moveaxis(y, 0, 1), S


def _rwkv7(p_rw, prev_row, S0, l, W):
    B, L, _ = p_rw.shape
    p = _f32(p_rw)
    prev = jnp.concatenate([_f32(prev_row)[:, None], p[:, :-1]], axis=1)
    xs = p + (prev - p) * _f32(W['rw_mu'][l])
    r, k, v, wd, ad, gd = jnp.split(xs, RW_SPLITS, axis=-1)
    w_log = -jax.nn.softplus(-(_f32(W['rw_w0'][l]) + jnp.tanh(wd) @ _f32(W['rw_w_up'][l]))) - 0.5
    decay = jnp.exp(-jnp.exp(w_log))
    a = jax.nn.sigmoid(_f32(W['rw_a0'][l]) + ad @ _f32(W['rw_a_up'][l]))
    g = jax.nn.sigmoid(gd) @ _f32(W['rw_g_up'][l])
    hd = lambda t: t.reshape(B, L, N_HEADS, HEAD_DIM)
    kk = hd(k * _f32(W['rw_k_k'][l]))
    kk = kk / jnp.maximum(jnp.sqrt(jnp.sum(kk * kk, -1, keepdims=True)), 1e-12)
    k = hd(k * (1.0 + (a - 1.0) * _f32(W['rw_k_a'][l])))
    r, v, a_h = hd(r), hd(v), hd(a)
    y, S = _rwkv_scan(_f32(S0), r, hd(decay), k, v, -kk, kk * a_h)
    y = _head_ln(y, RW_LNX_EPS).reshape(B, L, GROUP_W) * _f32(W['rw_lnx_w'][l]) + _f32(W['rw_lnx_b'][l])
    bonus = (jnp.sum(r * k * _f32(W['rw_r_k'][l]), -1, keepdims=True) * v).reshape(B, L, GROUP_W)
    return (y + bonus) * g, S, p_rw[:, -1]


def _diff_attn_prompt(q, k, v, lam):
    B, S = q.shape[:2]
    nb = S // Q_BLOCK
    qb = jnp.moveaxis(q.reshape(B, nb, Q_BLOCK, *q.shape[2:]), 1, 0)
    kpos = jnp.arange(S)
    scale = DA_DQK ** -0.5

    def block(args):
        qi, i = args
        s = jnp.einsum('bqhmd,bkhmd->bhmqk', qi, k) * scale
        qpos = i * Q_BLOCK + jnp.arange(Q_BLOCK)
        s = jnp.where(kpos[None, :] <= qpos[:, None], s, MASK_NEG)
        p = jax.nn.softmax(s, axis=-1)
        return jnp.einsum('bhqk,bkhe->bqhe', p[:, :, 0] - lam * p[:, :, 1], v)

    o = lax.map(block, (qb, jnp.arange(nb)))
    return jnp.moveaxis(o, 0, 1).reshape(B, S, N_HEADS, HEAD_DIM)


def _diff_attn_sample(q, k, v, k_past, v_past, lam):
    L = q.shape[1]
    P = k_past.shape[1]
    scale = DA_DQK ** -0.5
    s_past = jnp.einsum('bqhmd,bkhmd->bhmqk', q, k_past) * scale
    s_new = jnp.einsum('bqhmd,bkhmd->bhmqk', q, k) * scale
    s_new = jnp.where(jnp.tril(jnp.ones((L, L), bool)), s_new, MASK_NEG)
    p = jax.nn.softmax(jnp.concatenate([_f32(s_past), s_new], -1), axis=-1)
    pd = p[:, :, 0] - lam * p[:, :, 1]
    return (jnp.einsum('bhqk,bkhe->bqhe', pd[..., :P], v_past)
            + jnp.einsum('bhqk,bkhe->bqhe', pd[..., P:], v))


def _swiglu(x, wg, wu, wd):
    return (jax.nn.silu(x @ wg) * (x @ wu)) @ wd


def _grouped_experts(xf, idx, wts, wg, wu, wd):
    M, D = xf.shape
    A = M * TOP_K
    flat_e = idx.reshape(-1)
    order = jnp.argsort(flat_e)
    e_sorted = flat_e[order]
    tok_sorted = (order // TOP_K).astype(jnp.int32)
    w_sorted = wts.reshape(-1)[order]
    counts = jnp.bincount(flat_e, length=N_EXPERTS)
    padded = (counts + MOE_BLOCK - 1) // MOE_BLOCK * MOE_BLOCK
    pad_end = jnp.cumsum(padded)
    pad_start = pad_end - padded
    start = jnp.cumsum(counts) - counts
    dest = pad_start[e_sorted] + jnp.arange(A) - start[e_sorted]
    n_blocks = -(-A // MOE_BLOCK) + N_EXPERTS
    P = n_blocks * MOE_BLOCK
    row_tok = jnp.zeros((P,), jnp.int32).at[dest].set(tok_sorted)
    row_w = jnp.zeros((P,), jnp.float32).at[dest].set(w_sorted)
    blk_e = jnp.minimum(jnp.searchsorted(pad_end, jnp.arange(n_blocks) * MOE_BLOCK, side='right'), N_EXPERTS - 1)

    def body(acc, args):
        toks, rw, e = args
        yb = _swiglu(xf[toks], wg[e], wu[e], wd[e])
        return acc.at[toks].add(_f32(yb) * rw[:, None]), None

    acc, _ = lax.scan(body, jnp.zeros((M, D), jnp.float32),
                      (row_tok.reshape(n_blocks, MOE_BLOCK), row_w.reshape(n_blocks, MOE_BLOCK), blk_e))
    return acc.astype(xf.dtype)


def _moe(x, l, W):
    B, L, D = x.shape
    xf = x.reshape(-1, D)
    scores = jax.nn.sigmoid(_f32(xf @ W['router_w'][l]))
    _, idx = lax.top_k(scores + _f32(W['router_bias'][l]), TOP_K)
    wts = jnp.take_along_axis(scores, idx, axis=-1)
    wts = wts / jnp.sum(wts, -1, keepdims=True) * ROUTED_SCALE
    routed = _grouped_experts(xf, idx, wts, W['e_gate'][l], W['e_up'][l], W['e_down'][l])
    shared = _swiglu(xf, W['sh_gate'][l], W['sh_up'][l], W['sh_down'][l])
    return (routed + shared).reshape(B, L, D)


def _layer(l, x, pos, ret_s0, hg_s0, rw_s0, shift0, kv_past, W):
    B, L, _ = x.shape
    dt = x.dtype
    proj = x @ W['w_in'][l]
    p_ret, p_hg, p_rw, p_da = jnp.split(proj, [W_RET, W_RET + W_HG, W_RET + W_HG + W_RW], axis=-1)
    heads = lambda t, d: t.reshape(B, L, N_HEADS, d)

    r_q, r_k, r_v, r_g = jnp.split(_f32(p_ret), 4, axis=-1)
    log_gamma = jnp.log1p(-jnp.exp2(-5.0 - jnp.arange(N_HEADS, dtype=jnp.float32)))
    q = _rotary(heads(r_q, HEAD_DIM), pos)
    k = _rotary(heads(r_k, HEAD_DIM), pos) * HEAD_DIM ** -0.5
    o, ret_s = _chunk_scan(functools.partial(_retention_step, log_gamma=log_gamma),
                           (q, k, heads(r_v, HEAD_DIM)), _f32(ret_s0))
    o_a = _rms(o).reshape(B, L, GROUP_W) * jax.nn.silu(r_g)

    h_q, h_f, h_i, h_g = jnp.split(_f32(p_hg), [N_HEADS * HG_DK, 2 * N_HEADS * HG_DK,
                                                2 * N_HEADS * HG_DK + N_HEADS * HEAD_DIM], axis=-1)
    lb_soft = jax.nn.softmax(_f32(W['hg_lb']), axis=0)
    lb = (jnp.cumsum(lb_soft, axis=0) - lb_soft[0])[l]
    if l == 0:
        log_f = jax.nn.log_sigmoid(h_f)
    else:
        log_f = jnp.logaddexp(jnp.log(lb), jnp.log1p(-lb) + jax.nn.log_sigmoid(h_f))
    k_in = (1.0 - lb) * jax.nn.sigmoid(-h_f)
    q_h = jax.nn.silu(h_q) * HG_DK ** -0.5
    o, hg_s = _chunk_scan(_hgrn2_step, (heads(q_h, HG_DK), heads(k_in, HG_DK), heads(log_f, HG_DK),
                                        heads(h_i, HEAD_DIM)), _f32(hg_s0))
    o_b = (_rms(o) * _f32(W['hg_norm_w'][l])).reshape(B, L, GROUP_W) * jax.nn.silu(h_g)

    o_c, rw_s, shift_new = _rwkv7(p_rw, shift0, rw_s0, l, W)

    d_q, d_k, d_v = jnp.split(p_da, 3, axis=-1)
    q5 = _f32(d_q).reshape(B, L, N_HEADS, 2, DA_DQK)
    k5 = _f32(d_k).reshape(B, L, N_HEADS, 2, DA_DQK)
    v4 = _f32(d_v).reshape(B, L, N_HEADS, HEAD_DIM)
    lam_init = 0.8 - 0.6 * math.exp(-0.3 * l)
    lam = (jnp.exp(jnp.sum(_f32(W['da_lq1'][l]) * _f32(W['da_lk1'][l])))
           - jnp.exp(jnp.sum(_f32(W['da_lq2'][l]) * _f32(W['da_lk2'][l]))) + lam_init)
    if kv_past is None:
        o = _diff_attn_prompt(q5, k5, v4, lam)
    else:
        k_past, v_past = kv_past
        o = _diff_attn_sample(q5, k5, v4, k_past.reshape(B, -1, N_HEADS, 2, DA_DQK), v_past, lam)
    o_d = (_rms(o) * _f32(W['da_norm_w'][l]) * (1.0 - lam_init)).reshape(B, L, GROUP_W)

    mix = jnp.concatenate([o_a, o_b, o_c, o_d], -1).astype(dt) @ W['w_o'][l]
    x = _layernorm(ALPHA * x + mix, W['ln1_g'][l], W['ln1_b'][l])
    x = _layernorm(ALPHA * x + _moe(x, l, W), W['ln2_g'][l], W['ln2_b'][l])
    new = (ret_s.astype(dt), hg_s.astype(dt), rw_s.astype(dt), shift_new,
           d_k.reshape(B, L, N_HEADS, HEAD_DIM), d_v.reshape(B, L, N_HEADS, HEAD_DIM))
    return x, new


def setup_inputs(seed: int = 0) -> dict:
    key = jax.random.key(seed)
    ki = iter(jax.random.split(key, 64))

    def nrm(shape, scale=1.0):
        return jax.random.normal(next(ki), shape, jnp.float32) * scale

    def uni(shape, lo, hi):
        return jax.random.uniform(next(ki), shape, jnp.float32, lo, hi)

    n_pages = PAST_LEN // PAGE_SIZE
    n_phys = (DEC_BATCH * n_pages * 5) // 4
    H, Dh = N_HEADS, HEAD_DIM
    x_prompt = nrm((BATCH, SEQ, D_MODEL))
    x_sample = nrm((DEC_BATCH, DEC_SEQ, D_MODEL))
    state_ret = nrm((DEPTH, DEC_BATCH, H, Dh, Dh))
    state_hgrn = nrm((DEPTH, DEC_BATCH, H, HG_DK, Dh))
    state_rwkv = nrm((DEPTH, DEC_BATCH, H, Dh, Dh), 0.5)
    state_rwkv_shift = nrm((DEPTH, DEC_BATCH, W_RW))
    cache_k = nrm((DEPTH, n_phys, PAGE_SIZE, H, Dh))
    cache_v = nrm((DEPTH, n_phys, PAGE_SIZE, H, Dh))
    page_table = jax.random.permutation(next(ki), n_phys)[: DEC_BATCH * n_pages].reshape(DEC_BATCH, n_pages).astype(jnp.int32)
    return {
        'x_prompt': x_prompt,
        'x_sample': x_sample,
        'state_ret': state_ret,
        'state_hgrn': state_hgrn,
        'state_rwkv': state_rwkv,
        'state_rwkv_shift': state_rwkv_shift,
        'cache_k': cache_k,
        'cache_v': cache_v,
        'page_table': page_table,
        'w_in': nrm((DEPTH, D_MODEL, N_COLS), D_MODEL ** -0.5),
        'w_o': nrm((DEPTH, D_MODEL, D_MODEL), D_MODEL ** -0.5 * BETA),
        'hg_lb': nrm((DEPTH, H * HG_DK), 0.5),
        'hg_norm_w': 1.0 + nrm((DEPTH, Dh), 0.02),
        'rw_mu': uni((DEPTH, W_RW), 0.0, 1.0),
        'rw_w0': uni((DEPTH, GROUP_W), -6.0, -1.0),
        'rw_w_up': nrm((DEPTH, RW_DECAY_LORA, GROUP_W), 0.5 * RW_DECAY_LORA ** -0.5),
        'rw_a0': nrm((DEPTH, GROUP_W), 0.1),
        'rw_a_up': nrm((DEPTH, RW_A_LORA, GROUP_W), RW_A_LORA ** -0.5),
        'rw_g_up': nrm((DEPTH, RW_GATE_LORA, GROUP_W), RW_GATE_LORA ** -0.5),
        'rw_k_k': 0.85 + nrm((DEPTH, GROUP_W), 0.02),
        'rw_k_a': 1.0 + nrm((DEPTH, GROUP_W), 0.02),
        'rw_r_k': nrm((DEPTH, H, Dh), 0.1),
        'rw_lnx_w': 1.0 + nrm((DEPTH, GROUP_W), 0.02),
        'rw_lnx_b': nrm((DEPTH, GROUP_W), 0.02),
        'da_lq1': nrm((DEPTH, DA_DQK), 0.1),
        'da_lk1': nrm((DEPTH, DA_DQK), 0.1),
        'da_lq2': nrm((DEPTH, DA_DQK), 0.1),
        'da_lk2': nrm((DEPTH, DA_DQK), 0.1),
        'da_norm_w': 1.0 + nrm((DEPTH, Dh), 0.02),
        'ln1_g': 1.0 + nrm((DEPTH, D_MODEL), 0.02),
        'ln1_b': nrm((DEPTH, D_MODEL), 0.02),
        'router_w': nrm((DEPTH, D_MODEL, N_EXPERTS), D_MODEL ** -0.5),
        'router_bias': nrm((DEPTH, N_EXPERTS), 0.01),
        'e_gate': nrm((DEPTH, N_EXPERTS, D_MODEL, D_EXPERT), D_MODEL ** -0.5),
        'e_up': nrm((DEPTH, N_EXPERTS, D_MODEL, D_EXPERT), D_MODEL ** -0.5),
        'e_down': nrm((DEPTH, N_EXPERTS, D_EXPERT, D_MODEL), D_EXPERT ** -0.5 * BETA),
        'sh_gate': nrm((DEPTH, D_MODEL, D_EXPERT), D_MODEL ** -0.5),
        'sh_up': nrm((DEPTH, D_MODEL, D_EXPERT), D_MODEL ** -0.5),
        'sh_down': nrm((DEPTH, D_EXPERT, D_MODEL), D_EXPERT ** -0.5 * BETA),
        'ln2_g': 1.0 + nrm((DEPTH, D_MODEL), 0.02),
        'ln2_b': nrm((DEPTH, D_MODEL), 0.02),
    }


def reference(x_prompt, x_sample, state_ret, state_hgrn, state_rwkv, state_rwkv_shift, cache_k, cache_v,
              page_table, w_in, w_o, hg_lb, hg_norm_w, rw_mu, rw_w0, rw_w_up, rw_a0, rw_a_up, rw_g_up,
              rw_k_k, rw_k_a, rw_r_k, rw_lnx_w, rw_lnx_b, da_lq1, da_lk1, da_lq2, da_lk2, da_norm_w,
              ln1_g, ln1_b, router_w, router_bias, e_gate, e_up, e_down, sh_gate, sh_up, sh_down,
              ln2_g, ln2_b):
    W = {'w_in': w_in, 'w_o': w_o, 'hg_lb': hg_lb, 'hg_norm_w': hg_norm_w, 'rw_mu': rw_mu,
         'rw_w0': rw_w0, 'rw_w_up': rw_w_up, 'rw_a0': rw_a0, 'rw_a_up': rw_a_up, 'rw_g_up': rw_g_up,
         'rw_k_k': rw_k_k, 'rw_k_a': rw_k_a, 'rw_r_k': rw_r_k, 'rw_lnx_w': rw_lnx_w, 'rw_lnx_b': rw_lnx_b,
         'da_lq1': da_lq1, 'da_lk1': da_lk1, 'da_lq2': da_lq2, 'da_lk2': da_lk2, 'da_norm_w': da_norm_w,
         'ln1_g': ln1_g, 'ln1_b': ln1_b, 'router_w': router_w, 'router_bias': router_bias,
         'e_gate': e_gate, 'e_up': e_up, 'e_down': e_down, 'sh_gate': sh_gate, 'sh_up': sh_up,
         'sh_down': sh_down, 'ln2_g': ln2_g, 'ln2_b': ln2_b}
    B, S, _ = x_prompt.shape
    DB, L, _ = x_sample.shape
    past_len = page_table.shape[1] * PAGE_SIZE
    pos_p = jnp.arange(S)
    pos_s = past_len + jnp.arange(L)
    zero_ret = jnp.zeros((B, N_HEADS, HEAD_DIM, HEAD_DIM), jnp.float32)
    zero_hg = jnp.zeros((B, N_HEADS, HG_DK, HEAD_DIM), jnp.float32)
    zero_shift = jnp.zeros((B, W_RW), x_prompt.dtype)
    yp, ys = x_prompt, x_sample
    new_p, new_s = [], []
    for l in range(DEPTH):
        yp, st = _layer(l, yp, pos_p, zero_ret, zero_hg, zero_ret, zero_shift, None, W)
        new_p.append(st)
        k_past = cache_k[l][page_table].reshape(DB, past_len, N_HEADS, HEAD_DIM)
        v_past = cache_v[l][page_table].reshape(DB, past_len, N_HEADS, HEAD_DIM)
        ys, st = _layer(l, ys, pos_s, state_ret[l], state_hgrn[l], state_rwkv[l], state_rwkv_shift[l],
                        (k_past, v_past), W)
        new_s.append(st)

    def stk(sts, i):
        return jnp.stack([s[i] for s in sts])

    return (yp, ys, stk(new_p, 0), stk(new_s, 0), stk(new_p, 1), stk(new_s, 1), stk(new_p, 2), stk(new_s, 2),
            stk(new_p, 3), stk(new_s, 3), stk(new_p, 4), stk(new_p, 5), stk(new_s, 4), stk(new_s, 5))
```

```python
import math, functools
import jax, jax.numpy as jnp
from jax import lax
from jax.experimental import pallas as pl
from jax.experimental.pallas import tpu as pltpu

D_MODEL = 1024
DEPTH = 2
PAGE_SIZE = 128
N_MIXERS = 4
GROUP_W = D_MODEL // N_MIXERS
HEAD_DIM = 64
N_HEADS = GROUP_W // HEAD_DIM
ROPE_BASE = 10000.0
HG_DK = 128
RW_DECAY_LORA = 64
RW_A_LORA = 64
RW_GATE_LORA = 128
RW_LNX_EPS = 6.4e-4
DA_DQK = HEAD_DIM // 2
W_RET = 3 * N_HEADS * HEAD_DIM + GROUP_W
W_HG = 2 * N_HEADS * HG_DK + N_HEADS * HEAD_DIM + GROUP_W
W_RW = 3 * GROUP_W + RW_DECAY_LORA + RW_A_LORA + RW_GATE_LORA
W_DA = 2 * (N_HEADS * 2 * DA_DQK) + N_HEADS * HEAD_DIM
N_COLS = W_RET + W_HG + W_RW + W_DA
RW_SPLITS = [GROUP_W, 2 * GROUP_W, 3 * GROUP_W, 3 * GROUP_W + RW_DECAY_LORA, 3 * GROUP_W + RW_DECAY_LORA + RW_A_LORA]
CHUNK = 64
Q_BLOCK = 128
MASK_NEG = -1e30
N_EXPERTS = 64
TOP_K = 8
D_EXPERT = 256
ROUTED_SCALE = 2.5
MOE_BLOCK = 128
ALPHA = (2.0 * DEPTH) ** 0.25
LN_EPS = 1e-5
NORM_EPS = 1e-5

V7X_VMEM_LIMIT_BYTES = 56 * 1024 * 1024


def _f32(t):
    return t.astype(jnp.float32)


def _matmul_body(x_ref, w_ref, o_ref):
    o_ref[...] = jnp.dot(x_ref[...].astype(jnp.bfloat16), w_ref[...],
                         preferred_element_type=jnp.float32)


def _matmul(x, w_bf16, tm):
    M, K = x.shape
    N = w_bf16.shape[1]
    tm = min(tm, M)
    assert M % tm == 0
    return pl.pallas_call(
        _matmul_body,
        grid=(M // tm,),
        in_specs=[pl.BlockSpec((tm, K), lambda i: (i, 0)),
                  pl.BlockSpec((K, N), lambda i: (0, 0))],
        out_specs=pl.BlockSpec((tm, N), lambda i: (i, 0)),
        out_shape=jax.ShapeDtypeStruct((M, N), jnp.float32),
        compiler_params=pltpu.CompilerParams(
            dimension_semantics=("arbitrary",), vmem_limit_bytes=V7X_VMEM_LIMIT_BYTES),
        name="matmul",
    )(x, w_bf16)


def _layernorm(x, g, b):
    xf = _f32(x)
    mu = jnp.mean(xf, -1, keepdims=True)
    xc = xf - mu
    var = jnp.mean(xc * xc, -1, keepdims=True)
    return (xc * lax.rsqrt(var + LN_EPS) * _f32(g) + _f32(b)).astype(x.dtype)


def _rms(x, eps=NORM_EPS):
    return x * lax.rsqrt(jnp.mean(x * x, -1, keepdims=True) + eps)


def _head_ln(x, eps):
    xc = x - jnp.mean(x, -1, keepdims=True)
    return xc * lax.rsqrt(jnp.mean(xc * xc, -1, keepdims=True) + eps)


def _rotary(x, pos):
    half = x.shape[-1] // 2
    freq = 1.0 / (ROPE_BASE ** jnp.linspace(0.0, 1.0, half, dtype=jnp.float32))
    ang = _f32(pos)[:, None] * freq[None, :]
    cos = jnp.cos(ang)[None, :, None, :]
    sin = jnp.sin(ang)[None, :, None, :]
    x1, x2 = x[..., :half], x[..., half:]
    return jnp.concatenate([x1 * cos - x2 * sin, x1 * sin + x2 * cos], -1)


def _chunk_scan(step, xs, s0):
    B, L = xs[0].shape[:2]
    c = CHUNK if L % CHUNK == 0 else L
    n = L // c
    chunked = tuple(jnp.moveaxis(a.reshape(B, n, c, *a.shape[2:]), 1, 0) for a in xs)
    s, out = lax.scan(step, s0, chunked)
    return jnp.moveaxis(out, 0, 1).reshape(B, L, *out.shape[3:]), s


def _retention_step(S, xs, log_gamma):
    q, k, v = xs
    C = q.shape[1]
    t = jnp.arange(C, dtype=jnp.float32)
    gap = t[:, None] - t[None, :]
    dmask = jnp.where(gap >= 0, jnp.exp(jnp.maximum(gap, 0.0)[None] * log_gamma[:, None, None]), 0.0)
    scores = jnp.einsum('bqhd,bkhd->bhqk', q, k) * dmask[None]
    o = jnp.einsum('bhqk,bkhe->bqhe', scores, v)
    o = o + jnp.einsum('bqhd,bhde->bqhe', q, S) * jnp.exp((t + 1.0)[:, None] * log_gamma[None, :])[None, :, :, None]
    k_w = k * jnp.exp((C - 1.0 - t)[:, None] * log_gamma[None, :])[None, :, :, None]
    S = jnp.exp(C * log_gamma)[None, :, None, None] * S + jnp.einsum('bkhd,bkhe->bhde', k_w, v)
    return S, o


def _hgrn2_step(S, xs):
    q, k, logf, v = xs
    C = q.shape[1]
    b = jnp.cumsum(logf, axis=1)
    causal = jnp.tril(jnp.ones((C, C), bool))[None, :, :, None, None]
    diff = jnp.where(causal, b[:, :, None] - b[:, None, :], 0.0)
    decay = jnp.where(causal, jnp.exp(diff), 0.0)
    attn = jnp.einsum('bqhd,bqkhd,bkhd->bhqk', q, decay, k)
    o = jnp.einsum('bhqk,bkhe->bqhe', attn, v) + jnp.einsum('bqhd,bhde->bqhe', q * jnp.exp(b), S)
    b_last = b[:, -1]
    S = jnp.exp(b_last)[..., None] * S + jnp.einsum('bkhd,bkhe->bhde', k * jnp.exp(b_last[:, None] - b), v)
    return S, o


def _rwkv_scan(S0, r, w, k, v, a, b):
    def step(S, xs):
        r_t, w_t, k_t, v_t, a_t, b_t = xs
        S = (S * w_t[:, :, None, :]
             + jnp.einsum('bhvk,bhk->bhv', S, a_t)[..., None] * b_t[:, :, None, :]
             + v_t[..., None] * k_t[:, :, None, :])
        return S, jnp.einsum('bhvk,bhk->bhv', S, r_t)
    xs = tuple(jnp.moveaxis(t, 1, 0) for t in (r, w, k, v, a, b))
    S, y = lax.scan(step, S0, xs)
    return jnp.moveaxis(y, 0, 1), S


def _rwkv7(p_rw, prev_row, S0, l, W):
    B, L, _ = p_rw.shape
    p = _f32(p_rw)
    prev = jnp.concatenate([_f32(prev_row)[:, None], p[:, :-1]], axis=1)
    xs = p + (prev - p) * _f32(W['rw_mu'][l])
    r, k, v, wd, ad, gd = jnp.split(xs, RW_SPLITS, axis=-1)
    w_log = -jax.nn.softplus(-(_f32(W['rw_w0'][l]) + jnp.tanh(wd) @ _f32(W['rw_w_up'][l]))) - 0.5
    decay = jnp.exp(-jnp.exp(w_log))
    a = jax.nn.sigmoid(_f32(W['rw_a0'][l]) + ad @ _f32(W['rw_a_up'][l]))
    g = jax.nn.sigmoid(gd) @ _f32(W['rw_g_up'][l])
    hd = lambda t: t.reshape(B, L, N_HEADS, HEAD_DIM)
    kk = hd(k * _f32(W['rw_k_k'][l]))
    kk = kk / jnp.maximum(jnp.sqrt(jnp.sum(kk * kk, -1, keepdims=True)), 1e-12)
    k = hd(k * (1.0 + (a - 1.0) * _f32(W['rw_k_a'][l])))
    r, v, a_h = hd(r), hd(v), hd(a)
    y, S = _rwkv_scan(_f32(S0), r, hd(decay), k, v, -kk, kk * a_h)
    y = _head_ln(y, RW_LNX_EPS).reshape(B, L, GROUP_W) * _f32(W['rw_lnx_w'][l]) + _f32(W['rw_lnx_b'][l])
    bonus = (jnp.sum(r * k * _f32(W['rw_r_k'][l]), -1, keepdims=True) * v).reshape(B, L, GROUP_W)
    return (y + bonus) * g, S, p_rw[:, -1]


def _diff_attn_prompt(q, k, v, lam):
    B, S = q.shape[:2]
    qb_sz = Q_BLOCK if S % Q_BLOCK == 0 else S
    nb = S // qb_sz
    qb = jnp.moveaxis(q.reshape(B, nb, qb_sz, *q.shape[2:]), 1, 0)
    kpos = jnp.arange(S)
    scale = DA_DQK ** -0.5

    def block(args):
        qi, i = args
        s = jnp.einsum('bqhmd,bkhmd->bhmqk', qi, k) * scale
        qpos = i * qb_sz + jnp.arange(qb_sz)
        s = jnp.where(kpos[None, :] <= qpos[:, None], s, MASK_NEG)
        p = jax.nn.softmax(s, axis=-1)
        return jnp.einsum('bhqk,bkhe->bqhe', p[:, :, 0] - lam * p[:, :, 1], v)

    o = lax.map(block, (qb, jnp.arange(nb)))
    return jnp.moveaxis(o, 0, 1).reshape(B, S, N_HEADS, HEAD_DIM)


def _diff_attn_sample(q, k, v, k_past, v_past, lam):
    L = q.shape[1]
    P = k_past.shape[1]
    scale = DA_DQK ** -0.5
    s_past = jnp.einsum('bqhmd,bkhmd->bhmqk', q, k_past) * scale
    s_new = jnp.einsum('bqhmd,bkhmd->bhmqk', q, k) * scale
    s_new = jnp.where(jnp.tril(jnp.ones((L, L), bool)), s_new, MASK_NEG)
    p = jax.nn.softmax(jnp.concatenate([_f32(s_past), s_new], -1), axis=-1)
    pd = p[:, :, 0] - lam * p[:, :, 1]
    return (jnp.einsum('bhqk,bkhe->bqhe', pd[..., :P], v_past)
            + jnp.einsum('bhqk,bkhe->bqhe', pd[..., P:], v))


def _swiglu(x, wg, wu, wd):
    return (jax.nn.silu(x @ wg) * (x @ wu)) @ wd


def _grouped_experts(xf, idx, wts, wg, wu, wd):
    M, D = xf.shape
    A = M * TOP_K
    flat_e = idx.reshape(-1)
    order = jnp.argsort(flat_e)
    e_sorted = flat_e[order]
    tok_sorted = (order // TOP_K).astype(jnp.int32)
    w_sorted = wts.reshape(-1)[order]
    counts = jnp.bincount(flat_e, length=N_EXPERTS)
    padded = (counts + MOE_BLOCK - 1) // MOE_BLOCK * MOE_BLOCK
    pad_end = jnp.cumsum(padded)
    pad_start = pad_end - padded
    start = jnp.cumsum(counts) - counts
    dest = pad_start[e_sorted] + jnp.arange(A) - start[e_sorted]
    n_blocks = -(-A // MOE_BLOCK) + N_EXPERTS
    P = n_blocks * MOE_BLOCK
    row_tok = jnp.zeros((P,), jnp.int32).at[dest].set(tok_sorted)
    row_w = jnp.zeros((P,), jnp.float32).at[dest].set(w_sorted)
    blk_e = jnp.minimum(jnp.searchsorted(pad_end, jnp.arange(n_blocks) * MOE_BLOCK, side='right'), N_EXPERTS - 1)

    def body(acc, args):
        toks, rw, e = args
        yb = _swiglu(xf[toks], wg[e], wu[e], wd[e])
        return acc.at[toks].add(_f32(yb) * rw[:, None]), None

    acc, _ = lax.scan(body, jnp.zeros((M, D), jnp.float32),
                      (row_tok.reshape(n_blocks, MOE_BLOCK), row_w.reshape(n_blocks, MOE_BLOCK), blk_e))
    return acc.astype(xf.dtype)


def _moe(x, l, W):
    B, L, D = x.shape
    xf = x.reshape(-1, D)
    scores = jax.nn.sigmoid(_f32(xf @ W['router_w'][l]))
    _, idx = lax.top_k(scores + _f32(W['router_bias'][l]), TOP_K)
    wts = jnp.take_along_axis(scores, idx, axis=-1)
    wts = wts / jnp.sum(wts, -1, keepdims=True) * ROUTED_SCALE
    routed = _grouped_experts(xf, idx, wts, W['e_gate'][l], W['e_up'][l], W['e_down'][l])
    shared = _swiglu(xf, W['sh_gate'][l], W['sh_up'][l], W['sh_down'][l])
    return (routed + shared).reshape(B, L, D)


def _layer(l, x, pos, ret_s0, hg_s0, rw_s0, shift0, kv_past, W):
    B, L, _ = x.shape
    dt = x.dtype
    proj = _matmul(x.reshape(B * L, D_MODEL), W['w_in_bf16'][l], 256).reshape(B, L, N_COLS)
    p_ret, p_hg, p_rw, p_da = jnp.split(proj, [W_RET, W_RET + W_HG, W_RET + W_HG + W_RW], axis=-1)
    heads = lambda t, d: t.reshape(B, L, N_HEADS, d)

    r_q, r_k, r_v, r_g = jnp.split(_f32(p_ret), 4, axis=-1)
    log_gamma = jnp.log1p(-jnp.exp2(-5.0 - jnp.arange(N_HEADS, dtype=jnp.float32)))
    q = _rotary(heads(r_q, HEAD_DIM), pos)
    k = _rotary(heads(r_k, HEAD_DIM), pos) * HEAD_DIM ** -0.5
    o, ret_s = _chunk_scan(functools.partial(_retention_step, log_gamma=log_gamma),
                           (q, k, heads(r_v, HEAD_DIM)), _f32(ret_s0))
    o_a = _rms(o).reshape(B, L, GROUP_W) * jax.nn.silu(r_g)

    h_q, h_f, h_i, h_g = jnp.split(_f32(p_hg), [N_HEADS * HG_DK, 2 * N_HEADS * HG_DK,
                                                2 * N_HEADS * HG_DK + N_HEADS * HEAD_DIM], axis=-1)
    lb_soft = jax.nn.softmax(_f32(W['hg_lb']), axis=0)
    lb = (jnp.cumsum(lb_soft, axis=0) - lb_soft[0])[l]
    if l == 0:
        log_f = jax.nn.log_sigmoid(h_f)
    else:
        log_f = jnp.logaddexp(jnp.log(lb), jnp.log1p(-lb) + jax.nn.log_sigmoid(h_f))
    k_in = (1.0 - lb) * jax.nn.sigmoid(-h_f)
    q_h = jax.nn.silu(h_q) * HG_DK ** -0.5
    o, hg_s = _chunk_scan(_hgrn2_step, (heads(q_h, HG_DK), heads(k_in, HG_DK), heads(log_f, HG_DK),
                                        heads(h_i, HEAD_DIM)), _f32(hg_s0))
    o_b = (_rms(o) * _f32(W['hg_norm_w'][l])).reshape(B, L, GROUP_W) * jax.nn.silu(h_g)

    o_c, rw_s, shift_new = _rwkv7(p_rw, shift0, rw_s0, l, W)

    d_q, d_k, d_v = jnp.split(p_da, 3, axis=-1)
    q5 = _f32(d_q).reshape(B, L, N_HEADS, 2, DA_DQK)
    k5 = _f32(d_k).reshape(B, L, N_HEADS, 2, DA_DQK)
    v4 = _f32(d_v).reshape(B, L, N_HEADS, HEAD_DIM)
    lam_init = 0.8 - 0.6 * math.exp(-0.3 * l)
    lam = (jnp.exp(jnp.sum(_f32(W['da_lq1'][l]) * _f32(W['da_lk1'][l])))
           - jnp.exp(jnp.sum(_f32(W['da_lq2'][l]) * _f32(W['da_lk2'][l]))) + lam_init)
    if kv_past is None:
        o = _diff_attn_prompt(q5, k5, v4, lam)
    else:
        k_past, v_past = kv_past
        o = _diff_attn_sample(q5, k5, v4, k_past.reshape(B, -1, N_HEADS, 2, DA_DQK), v_past, lam)
    o_d = (_rms(o) * _f32(W['da_norm_w'][l]) * (1.0 - lam_init)).reshape(B, L, GROUP_W)

    mix_in = jnp.concatenate([o_a, o_b, o_c, o_d], -1).astype(dt)
    mix = _matmul(mix_in.reshape(B * L, D_MODEL), W['w_o_bf16'][l], 256).reshape(B, L, D_MODEL)
    x = _layernorm(ALPHA * x + mix, W['ln1_g'][l], W['ln1_b'][l])
    x = _layernorm(ALPHA * x + _moe(x, l, W), W['ln2_g'][l], W['ln2_b'][l])
    new = (ret_s.astype(dt), hg_s.astype(dt), rw_s.astype(dt), shift_new,
           d_k.reshape(B, L, N_HEADS, HEAD_DIM), d_v.reshape(B, L, N_HEADS, HEAD_DIM))
    return x, new


def kernel(x_prompt, x_sample, state_ret, state_hgrn, state_rwkv, state_rwkv_shift, cache_k, cache_v,
           page_table, w_in, w_o, hg_lb, hg_norm_w, rw_mu, rw_w0, rw_w_up, rw_a0, rw_a_up, rw_g_up,
           rw_k_k, rw_k_a, rw_r_k, rw_lnx_w, rw_lnx_b, da_lq1, da_lk1, da_lq2, da_lk2, da_norm_w,
           ln1_g, ln1_b, router_w, router_bias, e_gate, e_up, e_down, sh_gate, sh_up, sh_down,
           ln2_g, ln2_b):
    W = {'w_in': w_in, 'w_o': w_o, 'hg_lb': hg_lb, 'hg_norm_w': hg_norm_w, 'rw_mu': rw_mu,
         'rw_w0': rw_w0, 'rw_w_up': rw_w_up, 'rw_a0': rw_a0, 'rw_a_up': rw_a_up, 'rw_g_up': rw_g_up,
         'rw_k_k': rw_k_k, 'rw_k_a': rw_k_a, 'rw_r_k': rw_r_k, 'rw_lnx_w': rw_lnx_w, 'rw_lnx_b': rw_lnx_b,
         'da_lq1': da_lq1, 'da_lk1': da_lk1, 'da_lq2': da_lq2, 'da_lk2': da_lk2, 'da_norm_w': da_norm_w,
         'ln1_g': ln1_g, 'ln1_b': ln1_b, 'router_w': router_w, 'router_bias': router_bias,
         'e_gate': e_gate, 'e_up': e_up, 'e_down': e_down, 'sh_gate': sh_gate, 'sh_up': sh_up,
         'sh_down': sh_down, 'ln2_g': ln2_g, 'ln2_b': ln2_b}
    W['w_in_bf16'] = w_in.astype(jnp.bfloat16)
    W['w_o_bf16'] = w_o.astype(jnp.bfloat16)
    B, S, _ = x_prompt.shape
    DB, L, _ = x_sample.shape
    past_len = page_table.shape[1] * PAGE_SIZE
    pos_p = jnp.arange(S)
    pos_s = past_len + jnp.arange(L)
    zero_ret = jnp.zeros((B, N_HEADS, HEAD_DIM, HEAD_DIM), jnp.float32)
    zero_hg = jnp.zeros((B, N_HEADS, HG_DK, HEAD_DIM), jnp.float32)
    zero_shift = jnp.zeros((B, W_RW), x_prompt.dtype)
    yp, ys = x_prompt, x_sample
    new_p, new_s = [], []
    for l in range(DEPTH):
        yp, st = _layer(l, yp, pos_p, zero_ret, zero_hg, zero_ret, zero_shift, None, W)
        new_p.append(st)
        k_past = cache_k[l][page_table].reshape(DB, past_len, N_HEADS, HEAD_DIM)
        v_past = cache_v[l][page_table].reshape(DB, past_len, N_HEADS, HEAD_DIM)
        ys, st = _layer(l, ys, pos_s, state_ret[l], state_hgrn[l], state_rwkv[l], state_rwkv_shift[l],
                        (k_past, v_past), W)
        new_s.append(st)

    def stk(sts, i):
        return jnp.stack([s[i] for s in sts])

    return (yp, ys, stk(new_p, 0), stk(new_s, 0), stk(new_p, 1), stk(new_s, 1), stk(new_p, 2), stk(new_s, 2),
            stk(new_p, 3), stk(new_s, 3), stk(new_p, 4), stk(new_p, 5), stk(new_s, 4), stk(new_s, 5))
```

```python
import math, functools
import jax, jax.numpy as jnp
from jax import lax
from jax.experimental import pallas as pl
from jax.experimental.pallas import tpu as pltpu

D_MODEL = 1024
DEPTH = 2
PAGE_SIZE = 128
N_MIXERS = 4
GROUP_W = D_MODEL // N_MIXERS
HEAD_DIM = 64
N_HEADS = GROUP_W // HEAD_DIM
ROPE_BASE = 10000.0
HG_DK = 128
RW_DECAY_LORA = 64
RW_A_LORA = 64
RW_GATE_LORA = 128
RW_LNX_EPS = 6.4e-4
DA_DQK = HEAD_DIM // 2
W_RET = 3 * N_HEADS * HEAD_DIM + GROUP_W
W_HG = 2 * N_HEADS * HG_DK + N_HEADS * HEAD_DIM + GROUP_W
W_RW = 3 * GROUP_W + RW_DECAY_LORA + RW_A_LORA + RW_GATE_LORA
W_DA = 2 * (N_HEADS * 2 * DA_DQK) + N_HEADS * HEAD_DIM
N_COLS = W_RET + W_HG + W_RW + W_DA
RW_SPLITS = [GROUP_W, 2 * GROUP_W, 3 * GROUP_W, 3 * GROUP_W + RW_DECAY_LORA, 3 * GROUP_W + RW_DECAY_LORA + RW_A_LORA]
CHUNK = 64
Q_BLOCK = 128
MASK_NEG = -1e30
N_EXPERTS = 64
TOP_K = 8
D_EXPERT = 256
ROUTED_SCALE = 2.5
MOE_BLOCK = 128
ALPHA = (2.0 * DEPTH) ** 0.25
LN_EPS = 1e-5
NORM_EPS = 1e-5

V7X_VMEM_LIMIT_BYTES = 56 * 1024 * 1024


def _f32(t):
    return t.astype(jnp.float32)


def _matmul_body(x_ref, w_ref, o_ref):
    o_ref[...] = jnp.dot(x_ref[...].astype(jnp.bfloat16), w_ref[...],
                         preferred_element_type=jnp.float32)


def _matmul(x, w_bf16, tm):
    M, K = x.shape
    N = w_bf16.shape[1]
    tm = min(tm, M)
    assert M % tm == 0
    return pl.pallas_call(
        _matmul_body,
        grid=(M // tm,),
        in_specs=[pl.BlockSpec((tm, K), lambda i: (i, 0)),
                  pl.BlockSpec((K, N), lambda i: (0, 0))],
        out_specs=pl.BlockSpec((tm, N), lambda i: (i, 0)),
        out_shape=jax.ShapeDtypeStruct((M, N), jnp.float32),
        compiler_params=pltpu.CompilerParams(
            dimension_semantics=("arbitrary",), vmem_limit_bytes=V7X_VMEM_LIMIT_BYTES),
        name="matmul",
    )(x, w_bf16)


def _ln_rows(z, g, b):
    mu = jnp.mean(z, -1, keepdims=True)
    zc = z - mu
    var = jnp.mean(zc * zc, -1, keepdims=True)
    return zc * lax.rsqrt(var + LN_EPS) * g + b


def _matmul_res_ln_body(x_ref, w_ref, res_ref, g_ref, b_ref, o_ref):
    mix = jnp.dot(x_ref[...].astype(jnp.bfloat16), w_ref[...], preferred_element_type=jnp.float32)
    o_ref[...] = _ln_rows(ALPHA * res_ref[...] + mix, g_ref[...], b_ref[...])


def _matmul_res_ln(x, w_bf16, res, g, b, tm):
    M, K = x.shape
    N = w_bf16.shape[1]
    tm = min(tm, M)
    assert M % tm == 0
    row = lambda i: (i, 0)
    fixed = lambda i: (0, 0)
    return pl.pallas_call(
        _matmul_res_ln_body,
        grid=(M // tm,),
        in_specs=[pl.BlockSpec((tm, K), row), pl.BlockSpec((K, N), fixed), pl.BlockSpec((tm, N), row),
                  pl.BlockSpec((1, N), fixed), pl.BlockSpec((1, N), fixed)],
        out_specs=pl.BlockSpec((tm, N), row),
        out_shape=jax.ShapeDtypeStruct((M, N), jnp.float32),
        compiler_params=pltpu.CompilerParams(
            dimension_semantics=("arbitrary",), vmem_limit_bytes=V7X_VMEM_LIMIT_BYTES),
        name="out_proj_ln",
    )(x, w_bf16, res, g.reshape(1, N), b.reshape(1, N))


EXPERT_ROWS = 256


def _start_row_gather(idx_ref, n_rows, src_hbm, dst_at, sem):
    def body(r, carry):
        t = idx_ref[0, 0, r]
        pltpu.make_async_copy(src_hbm.at[pl.ds(t, 1)], dst_at(r), sem).start()
        return carry
    lax.fori_loop(0, n_rows, body, 0, unroll=8)


def _moe_experts_body(blk_e_ref, n_used_ref, tok_ref, tok_next_ref, x_hbm, wg_ref, wu_ref, wd_ref,
                      y_ref, xbuf, sem):
    i = pl.program_id(0)
    n_used = n_used_ref[0]
    slot = i % 2

    def gather(idx_ref, s):
        _start_row_gather(idx_ref, EXPERT_ROWS, x_hbm, lambda r: xbuf.at[s, pl.ds(r, 1)], sem.at[s])

    @pl.when(i == 0)
    def _():
        gather(tok_ref, 0)

    @pl.when(i + 1 < n_used)
    def _():
        gather(tok_next_ref, 1 - slot)

    @pl.when(i < n_used)
    def _():
        pltpu.make_async_copy(x_hbm.at[pl.ds(0, EXPERT_ROWS)], xbuf.at[slot], sem.at[slot]).wait()
        x = xbuf[slot].astype(jnp.bfloat16)
        g = jnp.dot(x, wg_ref[0], preferred_element_type=jnp.float32)
        u = jnp.dot(x, wu_ref[0], preferred_element_type=jnp.float32)
        h = (g * jax.nn.sigmoid(g) * u).astype(jnp.bfloat16)
        y_ref[...] = jnp.dot(h, wd_ref[0], preferred_element_type=jnp.float32)

    @pl.when(i >= n_used)
    def _():
        y_ref[...] = jnp.zeros_like(y_ref)


def _moe_experts(x, row_tok, blk_e, n_used, wg, wu, wd):
    M, D = x.shape
    n_blocks = blk_e.shape[0]
    tok3 = row_tok.reshape(n_blocks, 1, EXPERT_ROWS)
    smem_blk = lambda f: pl.BlockSpec((1, 1, EXPERT_ROWS), f, memory_space=pltpu.SMEM)
    grid_spec = pltpu.PrefetchScalarGridSpec(
        num_scalar_prefetch=2,
        grid=(n_blocks,),
        in_specs=[smem_blk(lambda i, e, n: (i, 0, 0)),
                  smem_blk(lambda i, e, n: (jnp.minimum(i + 1, n_blocks - 1), 0, 0)),
                  pl.BlockSpec(memory_space=pl.ANY),
                  pl.BlockSpec((1, D, D_EXPERT), lambda i, e, n: (e[i], 0, 0)),
                  pl.BlockSpec((1, D, D_EXPERT), lambda i, e, n: (e[i], 0, 0)),
                  pl.BlockSpec((1, D_EXPERT, D), lambda i, e, n: (e[i], 0, 0))],
        out_specs=pl.BlockSpec((EXPERT_ROWS, D), lambda i, e, n: (i, 0)),
        scratch_shapes=[pltpu.VMEM((2, EXPERT_ROWS, D), jnp.float32), pltpu.SemaphoreType.DMA((2,))],
    )
    return pl.pallas_call(
        _moe_experts_body,
        grid_spec=grid_spec,
        out_shape=jax.ShapeDtypeStruct((n_blocks * EXPERT_ROWS, D), jnp.float32),
        compiler_params=pltpu.CompilerParams(
            dimension_semantics=("arbitrary",), vmem_limit_bytes=V7X_VMEM_LIMIT_BYTES),
        name="moe_experts",
    )(blk_e, n_used, tok3, tok3, x, wg, wu, wd)


COMBINE_ROWS = 128


def _moe_combine_body(pos_ref, pos_next_ref, wts_ref, x_ref, gu_ref, shd_ref, g_ref, b_ref, y_hbm,
                      o_ref, buf, sem):
    i = pl.program_id(0)
    n = pl.num_programs(0)
    slot = i % 2
    n_rows = COMBINE_ROWS * TOP_K

    def gather(idx_ref, s):
        def dst(r):
            k = jnp.bitwise_and(r, TOP_K - 1)
            token = lax.shift_right_logical(r, TOP_K.bit_length() - 1)
            return buf.at[s, pl.ds(k * COMBINE_ROWS + token, 1)]
        _start_row_gather(idx_ref, n_rows, y_hbm, dst, sem.at[s])

    @pl.when(i == 0)
    def _():
        gather(pos_ref, 0)

    @pl.when(i + 1 < n)
    def _():
        gather(pos_next_ref, 1 - slot)

    pltpu.make_async_copy(y_hbm.at[pl.ds(0, n_rows)], buf.at[slot], sem.at[slot]).wait()
    wts = wts_ref[...]
    routed = wts[:, 0:1] * buf[slot, pl.ds(0, COMBINE_ROWS)]
    for k in range(1, TOP_K):
        routed = routed + wts[:, k:k + 1] * buf[slot, pl.ds(k * COMBINE_ROWS, COMBINE_ROWS)]
    gu = gu_ref[...]
    g = gu[:, :D_EXPERT]
    u = gu[:, D_EXPERT:]
    h = (g * jax.nn.sigmoid(g) * u).astype(jnp.bfloat16)
    shared = jnp.dot(h, shd_ref[...], preferred_element_type=jnp.float32)
    o_ref[...] = _ln_rows(ALPHA * x_ref[...] + (routed + shared), g_ref[...], b_ref[...])


def _moe_combine(y_sorted, inv_pos, wts, x, gu, shd_bf16, g, b):
    M, D = x.shape
    tt = min(COMBINE_ROWS, M)
    assert tt == COMBINE_ROWS and M % tt == 0
    n_tiles = M // tt
    pos3 = inv_pos.reshape(n_tiles, 1, tt * TOP_K)
    row = lambda i: (i, 0)
    fixed = lambda i: (0, 0)
    smem_blk = lambda f: pl.BlockSpec((1, 1, tt * TOP_K), f, memory_space=pltpu.SMEM)
    return pl.pallas_call(
        _moe_combine_body,
        grid=(n_tiles,),
        in_specs=[smem_blk(lambda i: (i, 0, 0)),
                  smem_blk(lambda i: (jnp.minimum(i + 1, n_tiles - 1), 0, 0)),
                  pl.BlockSpec((tt, TOP_K), row),
                  pl.BlockSpec((tt, D), row),
                  pl.BlockSpec((tt, 2 * D_EXPERT), row),
                  pl.BlockSpec((D_EXPERT, D), fixed),
                  pl.BlockSpec((1, D), fixed), pl.BlockSpec((1, D), fixed),
                  pl.BlockSpec(memory_space=pl.ANY)],
        out_specs=pl.BlockSpec((tt, D), row),
        out_shape=jax.ShapeDtypeStruct((M, D), jnp.float32),
        scratch_shapes=[pltpu.VMEM((2, tt * TOP_K, D), jnp.float32), pltpu.SemaphoreType.DMA((2,))],
        compiler_params=pltpu.CompilerParams(
            dimension_semantics=("arbitrary",), vmem_limit_bytes=V7X_VMEM_LIMIT_BYTES),
        name="moe_combine_ln",
    )(pos3, pos3, wts, x, gu, shd_bf16, g.reshape(1, D), b.reshape(1, D), y_sorted)


def _moe_ln(x, l, W):
    M, D = x.shape
    rgu = _matmul(x, W['gu_router_bf16'][l], 256)
    scores = jax.nn.sigmoid(rgu[:, 2 * D_EXPERT:2 * D_EXPERT + N_EXPERTS])
    _, idx = lax.top_k(scores + _f32(W['router_bias'][l]), TOP_K)
    wts = jnp.take_along_axis(scores, idx, axis=-1)
    wts = wts / jnp.sum(wts, -1, keepdims=True) * ROUTED_SCALE

    A = M * TOP_K
    flat_e = idx.reshape(-1)
    order = jnp.argsort(flat_e)
    e_sorted = flat_e[order]
    tok_sorted = (order // TOP_K).astype(jnp.int32)
    counts = jnp.bincount(flat_e, length=N_EXPERTS)
    padded = (counts + EXPERT_ROWS - 1) // EXPERT_ROWS * EXPERT_ROWS
    pad_end = jnp.cumsum(padded)
    pad_start = pad_end - padded
    start = jnp.cumsum(counts) - counts
    dest = (pad_start[e_sorted] + jnp.arange(A) - start[e_sorted]).astype(jnp.int32)
    n_blocks = -(-A // EXPERT_ROWS) + N_EXPERTS
    P = n_blocks * EXPERT_ROWS
    row_tok = jnp.zeros((P,), jnp.int32).at[dest].set(tok_sorted)
    inv_pos = jnp.zeros((A,), jnp.int32).at[order].set(dest)
    blk_e = jnp.minimum(jnp.searchsorted(pad_end, jnp.arange(n_blocks) * EXPERT_ROWS, side='right'),
                        N_EXPERTS - 1).astype(jnp.int32)
    n_used = (pad_end[-1:] // EXPERT_ROWS).astype(jnp.int32)

    y_sorted = _moe_experts(x, row_tok, blk_e, n_used, W['e_gate_bf16'][l], W['e_up_bf16'][l],
                            W['e_down_bf16'][l])
    return _moe_combine(y_sorted, inv_pos, wts, x, rgu, W['sh_down_bf16'][l], W['ln2_g'][l], W['ln2_b'][l])


def _layernorm(x, g, b):
    xf = _f32(x)
    mu = jnp.mean(xf, -1, keepdims=True)
    xc = xf - mu
    var = jnp.mean(xc * xc, -1, keepdims=True)
    return (xc * lax.rsqrt(var + LN_EPS) * _f32(g) + _f32(b)).astype(x.dtype)


def _rms(x, eps=NORM_EPS):
    return x * lax.rsqrt(jnp.mean(x * x, -1, keepdims=True) + eps)


def _head_ln(x, eps):
    xc = x - jnp.mean(x, -1, keepdims=True)
    return xc * lax.rsqrt(jnp.mean(xc * xc, -1, keepdims=True) + eps)


def _rotary(x, pos):
    half = x.shape[-1] // 2
    freq = 1.0 / (ROPE_BASE ** jnp.linspace(0.0, 1.0, half, dtype=jnp.float32))
    ang = _f32(pos)[:, None] * freq[None, :]
    cos = jnp.cos(ang)[None, :, None, :]
    sin = jnp.sin(ang)[None, :, None, :]
    x1, x2 = x[..., :half], x[..., half:]
    return jnp.concatenate([x1 * cos - x2 * sin, x1 * sin + x2 * cos], -1)


def _chunk_scan(step, xs, s0):
    B, L = xs[0].shape[:2]
    c = CHUNK if L % CHUNK == 0 else L
    n = L // c
    chunked = tuple(jnp.moveaxis(a.reshape(B, n, c, *a.shape[2:]), 1, 0) for a in xs)
    s, out = lax.scan(step, s0, chunked)
    return jnp.moveaxis(out, 0, 1).reshape(B, L, *out.shape[3:]), s


def _retention_step(S, xs, log_gamma):
    q, k, v = xs
    C = q.shape[1]
    t = jnp.arange(C, dtype=jnp.float32)
    gap = t[:, None] - t[None, :]
    dmask = jnp.where(gap >= 0, jnp.exp(jnp.maximum(gap, 0.0)[None] * log_gamma[:, None, None]), 0.0)
    scores = jnp.einsum('bqhd,bkhd->bhqk', q, k) * dmask[None]
    o = jnp.einsum('bhqk,bkhe->bqhe', scores, v)
    o = o + jnp.einsum('bqhd,bhde->bqhe', q, S) * jnp.exp((t + 1.0)[:, None] * log_gamma[None, :])[None, :, :, None]
    k_w = k * jnp.exp((C - 1.0 - t)[:, None] * log_gamma[None, :])[None, :, :, None]
    S = jnp.exp(C * log_gamma)[None, :, None, None] * S + jnp.einsum('bkhd,bkhe->bhde', k_w, v)
    return S, o


def _hgrn2_step(S, xs):
    q, k, logf, v = xs
    C = q.shape[1]
    b = jnp.cumsum(logf, axis=1)
    causal = jnp.tril(jnp.ones((C, C), bool))[None, :, :, None, None]
    diff = jnp.where(causal, b[:, :, None] - b[:, None, :], 0.0)
    decay = jnp.where(causal, jnp.exp(diff), 0.0)
    attn = jnp.einsum('bqhd,bqkhd,bkhd->bhqk', q, decay, k)
    o = jnp.einsum('bhqk,bkhe->bqhe', attn, v) + jnp.einsum('bqhd,bhde->bqhe', q * jnp.exp(b), S)
    b_last = b[:, -1]
    S = jnp.exp(b_last)[..., None] * S + jnp.einsum('bkhd,bkhe->bhde', k * jnp.exp(b_last[:, None] - b), v)
    return S, o


def _rwkv_scan(S0, r, w, k, v, a, b):
    def step(S, xs):
        r_t, w_t, k_t, v_t, a_t, b_t = xs
        S = (S * w_t[:, :, None, :]
             + jnp.einsum('bhvk,bhk->bhv', S, a_t)[..., None] * b_t[:, :, None, :]
             + v_t[..., None] * k_t[:, :, None, :])
        return S, jnp.einsum('bhvk,bhk->bhv', S, r_t)
    xs = tuple(jnp.moveaxis(t, 1, 0) for t in (r, w, k, v, a, b))
    S, y = lax.scan(step, S0, xs)
    return jnp.moveaxis(y, 0, 1), S


def _rwkv7(p_rw, prev_row, S0, l, W):
    B, L, _ = p_rw.shape
    p = _f32(p_rw)
    prev = jnp.concatenate([_f32(prev_row)[:, None], p[:, :-1]], axis=1)
    xs = p + (prev - p) * _f32(W['rw_mu'][l])
    r, k, v, wd, ad, gd = jnp.split(xs, RW_SPLITS, axis=-1)
    w_log = -jax.nn.softplus(-(_f32(W['rw_w0'][l]) + jnp.tanh(wd) @ _f32(W['rw_w_up'][l]))) - 0.5
    decay = jnp.exp(-jnp.exp(w_log))
    a = jax.nn.sigmoid(_f32(W['rw_a0'][l]) + ad @ _f32(W['rw_a_up'][l]))
    g = jax.nn.sigmoid(gd) @ _f32(W['rw_g_up'][l])
    hd = lambda t: t.reshape(B, L, N_HEADS, HEAD_DIM)
    kk = hd(k * _f32(W['rw_k_k'][l]))
    kk = kk / jnp.maximum(jnp.sqrt(jnp.sum(kk * kk, -1, keepdims=True)), 1e-12)
    k = hd(k * (1.0 + (a - 1.0) * _f32(W['rw_k_a'][l])))
    r, v, a_h = hd(r), hd(v), hd(a)
    y, S = _rwkv_scan(_f32(S0), r, hd(decay), k, v, -kk, kk * a_h)
    y = _head_ln(y, RW_LNX_EPS).reshape(B, L, GROUP_W) * _f32(W['rw_lnx_w'][l]) + _f32(W['rw_lnx_b'][l])
    bonus = (jnp.sum(r * k * _f32(W['rw_r_k'][l]), -1, keepdims=True) * v).reshape(B, L, GROUP_W)
    return (y + bonus) * g, S, p_rw[:, -1]


def _diff_attn_prompt(q, k, v, lam):
    B, S = q.shape[:2]
    qb_sz = Q_BLOCK if S % Q_BLOCK == 0 else S
    nb = S // qb_sz
    qb = jnp.moveaxis(q.reshape(B, nb, qb_sz, *q.shape[2:]), 1, 0)
    kpos = jnp.arange(S)
    scale = DA_DQK ** -0.5

    def block(args):
        qi, i = args
        s = jnp.einsum('bqhmd,bkhmd->bhmqk', qi, k) * scale
        qpos = i * qb_sz + jnp.arange(qb_sz)
        s = jnp.where(kpos[None, :] <= qpos[:, None], s, MASK_NEG)
        p = jax.nn.softmax(s, axis=-1)
        return jnp.einsum('bhqk,bkhe->bqhe', p[:, :, 0] - lam * p[:, :, 1], v)

    o = lax.map(block, (qb, jnp.arange(nb)))
    return jnp.moveaxis(o, 0, 1).reshape(B, S, N_HEADS, HEAD_DIM)


def _diff_attn_sample(q, k, v, k_past, v_past, lam):
    L = q.shape[1]
    P = k_past.shape[1]
    scale = DA_DQK ** -0.5
    s_past = jnp.einsum('bqhmd,bkhmd->bhmqk', q, k_past) * scale
    s_new = jnp.einsum('bqhmd,bkhmd->bhmqk', q, k) * scale
    s_new = jnp.where(jnp.tril(jnp.ones((L, L), bool)), s_new, MASK_NEG)
    p = jax.nn.softmax(jnp.concatenate([_f32(s_past), s_new], -1), axis=-1)
    pd = p[:, :, 0] - lam * p[:, :, 1]
    return (jnp.einsum('bhqk,bkhe->bqhe', pd[..., :P], v_past)
            + jnp.einsum('bhqk,bkhe->bqhe', pd[..., P:], v))


def _swiglu(x, wg, wu, wd):
    return (jax.nn.silu(x @ wg) * (x @ wu)) @ wd


def _grouped_experts(xf, idx, wts, wg, wu, wd):
    M, D = xf.shape
    A = M * TOP_K
    flat_e = idx.reshape(-1)
    order = jnp.argsort(flat_e)
    e_sorted = flat_e[order]
    tok_sorted = (order // TOP_K).astype(jnp.int32)
    w_sorted = wts.reshape(-1)[order]
    counts = jnp.bincount(flat_e, length=N_EXPERTS)
    padded = (counts + MOE_BLOCK - 1) // MOE_BLOCK * MOE_BLOCK
    pad_end = jnp.cumsum(padded)
    pad_start = pad_end - padded
    start = jnp.cumsum(counts) - counts
    dest = pad_start[e_sorted] + jnp.arange(A) - start[e_sorted]
    n_blocks = -(-A // MOE_BLOCK) + N_EXPERTS
    P = n_blocks * MOE_BLOCK
    row_tok = jnp.zeros((P,), jnp.int32).at[dest].set(tok_sorted)
    row_w = jnp.zeros((P,), jnp.float32).at[dest].set(w_sorted)
    blk_e = jnp.minimum(jnp.searchsorted(pad_end, jnp.arange(n_blocks) * MOE_BLOCK, side='right'), N_EXPERTS - 1)

    def body(acc, args):
        toks, rw, e = args
        yb = _swiglu(xf[toks], wg[e], wu[e], wd[e])
        return acc.at[toks].add(_f32(yb) * rw[:, None]), None

    acc, _ = lax.scan(body, jnp.zeros((M, D), jnp.float32),
                      (row_tok.reshape(n_blocks, MOE_BLOCK), row_w.reshape(n_blocks, MOE_BLOCK), blk_e))
    return acc.astype(xf.dtype)


def _moe(x, l, W):
    B, L, D = x.shape
    xf = x.reshape(-1, D)
    scores = jax.nn.sigmoid(_f32(xf @ W['router_w'][l]))
    _, idx = lax.top_k(scores + _f32(W['router_bias'][l]), TOP_K)
    wts = jnp.take_along_axis(scores, idx, axis=-1)
    wts = wts / jnp.sum(wts, -1, keepdims=True) * ROUTED_SCALE
    routed = _grouped_experts(xf, idx, wts, W['e_gate'][l], W['e_up'][l], W['e_down'][l])
    shared = _swiglu(xf, W['sh_gate'][l], W['sh_up'][l], W['sh_down'][l])
    return (routed + shared).reshape(B, L, D)


def _layer(l, x, pos, ret_s0, hg_s0, rw_s0, shift0, kv_past, W):
    B, L, _ = x.shape
    dt = x.dtype
    proj = _matmul(x.reshape(B * L, D_MODEL), W['w_in_bf16'][l], 256).reshape(B, L, N_COLS)
    p_ret, p_hg, p_rw, p_da = jnp.split(proj, [W_RET, W_RET + W_HG, W_RET + W_HG + W_RW], axis=-1)
    heads = lambda t, d: t.reshape(B, L, N_HEADS, d)

    r_q, r_k, r_v, r_g = jnp.split(_f32(p_ret), 4, axis=-1)
    log_gamma = jnp.log1p(-jnp.exp2(-5.0 - jnp.arange(N_HEADS, dtype=jnp.float32)))
    q = _rotary(heads(r_q, HEAD_DIM), pos)
    k = _rotary(heads(r_k, HEAD_DIM), pos) * HEAD_DIM ** -0.5
    o, ret_s = _chunk_scan(functools.partial(_retention_step, log_gamma=log_gamma),
                           (q, k, heads(r_v, HEAD_DIM)), _f32(ret_s0))
    o_a = _rms(o).reshape(B, L, GROUP_W) * jax.nn.silu(r_g)

    h_q, h_f, h_i, h_g = jnp.split(_f32(p_hg), [N_HEADS * HG_DK, 2 * N_HEADS * HG_DK,
                                                2 * N_HEADS * HG_DK + N_HEADS * HEAD_DIM], axis=-1)
    lb_soft = jax.nn.softmax(_f32(W['hg_lb']), axis=0)
    lb = (jnp.cumsum(lb_soft, axis=0) - lb_soft[0])[l]
    if l == 0:
        log_f = jax.nn.log_sigmoid(h_f)
    else:
        log_f = jnp.logaddexp(jnp.log(lb), jnp.log1p(-lb) + jax.nn.log_sigmoid(h_f))
    k_in = (1.0 - lb) * jax.nn.sigmoid(-h_f)
    q_h = jax.nn.silu(h_q) * HG_DK ** -0.5
    o, hg_s = _chunk_scan(_hgrn2_step, (heads(q_h, HG_DK), heads(k_in, HG_DK), heads(log_f, HG_DK),
                                        heads(h_i, HEAD_DIM)), _f32(hg_s0))
    o_b = (_rms(o) * _f32(W['hg_norm_w'][l])).reshape(B, L, GROUP_W) * jax.nn.silu(h_g)

    o_c, rw_s, shift_new = _rwkv7(p_rw, shift0, rw_s0, l, W)

    d_q, d_k, d_v = jnp.split(p_da, 3, axis=-1)
    q5 = _f32(d_q).reshape(B, L, N_HEADS, 2, DA_DQK)
    k5 = _f32(d_k).reshape(B, L, N_HEADS, 2, DA_DQK)
    v4 = _f32(d_v).reshape(B, L, N_HEADS, HEAD_DIM)
    lam_init = 0.8 - 0.6 * math.exp(-0.3 * l)
    lam = (jnp.exp(jnp.sum(_f32(W['da_lq1'][l]) * _f32(W['da_lk1'][l])))
           - jnp.exp(jnp.sum(_f32(W['da_lq2'][l]) * _f32(W['da_lk2'][l]))) + lam_init)
    if kv_past is None:
        o = _diff_attn_prompt(q5, k5, v4, lam)
    else:
        k_past, v_past = kv_past
        o = _diff_attn_sample(q5, k5, v4, k_past.reshape(B, -1, N_HEADS, 2, DA_DQK), v_past, lam)
    o_d = (_rms(o) * _f32(W['da_norm_w'][l]) * (1.0 - lam_init)).reshape(B, L, GROUP_W)

    mix_in = jnp.concatenate([o_a, o_b, o_c, o_d], -1).astype(dt)
    x1 = _matmul_res_ln(mix_in.reshape(B * L, D_MODEL), W['w_o_bf16'][l], x.reshape(B * L, D_MODEL),
                        W['ln1_g'][l], W['ln1_b'][l], 256)
    x = _moe_ln(x1, l, W).reshape(B, L, D_MODEL)
    new = (ret_s.astype(dt), hg_s.astype(dt), rw_s.astype(dt), shift_new,
           d_k.reshape(B, L, N_HEADS, HEAD_DIM), d_v.reshape(B, L, N_HEADS, HEAD_DIM))
    return x, new


def kernel(x_prompt, x_sample, state_ret, state_hgrn, state_rwkv, state_rwkv_shift, cache_k, cache_v,
           page_table, w_in, w_o, hg_lb, hg_norm_w, rw_mu, rw_w0, rw_w_up, rw_a0, rw_a_up, rw_g_up,
           rw_k_k, rw_k_a, rw_r_k, rw_lnx_w, rw_lnx_b, da_lq1, da_lk1, da_lq2, da_lk2, da_norm_w,
           ln1_g, ln1_b, router_w, router_bias, e_gate, e_up, e_down, sh_gate, sh_up, sh_down,
           ln2_g, ln2_b):
    W = {'w_in': w_in, 'w_o': w_o, 'hg_lb': hg_lb, 'hg_norm_w': hg_norm_w, 'rw_mu': rw_mu,
         'rw_w0': rw_w0, 'rw_w_up': rw_w_up, 'rw_a0': rw_a0, 'rw_a_up': rw_a_up, 'rw_g_up': rw_g_up,
         'rw_k_k': rw_k_k, 'rw_k_a': rw_k_a, 'rw_r_k': rw_r_k, 'rw_lnx_w': rw_lnx_w, 'rw_lnx_b': rw_lnx_b,
         'da_lq1': da_lq1, 'da_lk1': da_lk1, 'da_lq2': da_lq2, 'da_lk2': da_lk2, 'da_norm_w': da_norm_w,
         'ln1_g': ln1_g, 'ln1_b': ln1_b, 'router_w': router_w, 'router_bias': router_bias,
         'e_gate': e_gate, 'e_up': e_up, 'e_down': e_down, 'sh_gate': sh_gate, 'sh_up': sh_up,
         'sh_down': sh_down, 'ln2_g': ln2_g, 'ln2_b': ln2_b}
    W['w_in_bf16'] = w_in.astype(jnp.bfloat16)
    W['w_o_bf16'] = w_o.astype(jnp.bfloat16)
    router_pad = jnp.zeros((DEPTH, D_MODEL, 128 - N_EXPERTS), router_w.dtype)
    W['gu_router_bf16'] = jnp.concatenate([sh_gate, sh_up, router_w, router_pad], -1).astype(jnp.bfloat16)
    W['e_gate_bf16'] = e_gate.astype(jnp.bfloat16)
    W['e_up_bf16'] = e_up.astype(jnp.bfloat16)
    W['e_down_bf16'] = e_down.astype(jnp.bfloat16)
    W['sh_down_bf16'] = sh_down.astype(jnp.bfloat16)
    B, S, _ = x_prompt.shape
    DB, L, _ = x_sample.shape
    past_len = page_table.shape[1] * PAGE_SIZE
    pos_p = jnp.arange(S)
    pos_s = past_len + jnp.arange(L)
    zero_ret = jnp.zeros((B, N_HEADS, HEAD_DIM, HEAD_DIM), jnp.float32)
    zero_hg = jnp.zeros((B, N_HEADS, HG_DK, HEAD_DIM), jnp.float32)
    zero_shift = jnp.zeros((B, W_RW), x_prompt.dtype)
    yp, ys = x_prompt, x_sample
    new_p, new_s = [], []
    for l in range(DEPTH):
        yp, st = _layer(l, yp, pos_p, zero_ret, zero_hg, zero_ret, zero_shift, None, W)
        new_p.append(st)
        k_past = cache_k[l][page_table].reshape(DB, past_len, N_HEADS, HEAD_DIM)
        v_past = cache_v[l][page_table].reshape(DB, past_len, N_HEADS, HEAD_DIM)
        ys, st = _layer(l, ys, pos_s, state_ret[l], state_hgrn[l], state_rwkv[l], state_rwkv_shift[l],
                        (k_past, v_past), W)
        new_s.append(st)

    def stk(sts, i):
        return jnp.stack([s[i] for s in sts])

    return (yp, ys, stk(new_p, 0), stk(new_s, 0), stk(new_p, 1), stk(new_s, 1), stk(new_p, 2), stk(new_s, 2),
            stk(new_p, 3), stk(new_s, 3), stk(new_p, 4), stk(new_p, 5), stk(new_s, 4), stk(new_s, 5))
```

```python
import math, functools
import jax, jax.numpy as jnp
from jax import lax
from jax.experimental import pallas as pl
from jax.experimental.pallas import tpu as pltpu

D_MODEL = 1024
DEPTH = 2
PAGE_SIZE = 128
N_MIXERS = 4
GROUP_W = D_MODEL // N_MIXERS
HEAD_DIM = 64
N_HEADS = GROUP_W // HEAD_DIM
ROPE_BASE = 10000.0
HG_DK = 128
RW_DECAY_LORA = 64
RW_A_LORA = 64
RW_GATE_LORA = 128
RW_LNX_EPS = 6.4e-4
DA_DQK = HEAD_DIM // 2
W_RET = 3 * N_HEADS * HEAD_DIM + GROUP_W
W_HG = 2 * N_HEADS * HG_DK + N_HEADS * HEAD_DIM + GROUP_W
W_RW = 3 * GROUP_W + RW_DECAY_LORA + RW_A_LORA + RW_GATE_LORA
W_DA = 2 * (N_HEADS * 2 * DA_DQK) + N_HEADS * HEAD_DIM
N_COLS = W_RET + W_HG + W_RW + W_DA
RW_SPLITS = [GROUP_W, 2 * GROUP_W, 3 * GROUP_W, 3 * GROUP_W + RW_DECAY_LORA, 3 * GROUP_W + RW_DECAY_LORA + RW_A_LORA]
CHUNK = 64
Q_BLOCK = 128
MASK_NEG = -1e30
N_EXPERTS = 64
TOP_K = 8
D_EXPERT = 256
ROUTED_SCALE = 2.5
MOE_BLOCK = 128
ALPHA = (2.0 * DEPTH) ** 0.25
LN_EPS = 1e-5
NORM_EPS = 1e-5

V7X_VMEM_LIMIT_BYTES = 56 * 1024 * 1024


def _f32(t):
    return t.astype(jnp.float32)


def _matmul_body(x_ref, w_ref, o_ref):
    o_ref[...] = jnp.dot(x_ref[...].astype(jnp.bfloat16), w_ref[...],
                         preferred_element_type=jnp.float32)


def _matmul(x, w_bf16, tm):
    M, K = x.shape
    N = w_bf16.shape[1]
    tm = min(tm, M)
    assert M % tm == 0
    return pl.pallas_call(
        _matmul_body,
        grid=(M // tm,),
        in_specs=[pl.BlockSpec((tm, K), lambda i: (i, 0)),
                  pl.BlockSpec((K, N), lambda i: (0, 0))],
        out_specs=pl.BlockSpec((tm, N), lambda i: (i, 0)),
        out_shape=jax.ShapeDtypeStruct((M, N), jnp.float32),
        compiler_params=pltpu.CompilerParams(
            dimension_semantics=("arbitrary",), vmem_limit_bytes=V7X_VMEM_LIMIT_BYTES),
        name="matmul",
    )(x, w_bf16)


def _ln_rows(z, g, b):
    mu = jnp.mean(z, -1, keepdims=True)
    zc = z - mu
    var = jnp.mean(zc * zc, -1, keepdims=True)
    return zc * lax.rsqrt(var + LN_EPS) * g + b


def _matmul_res_ln_body(x_ref, w_ref, res_ref, g_ref, b_ref, o_ref):
    mix = jnp.dot(x_ref[...].astype(jnp.bfloat16), w_ref[...], preferred_element_type=jnp.float32)
    o_ref[...] = _ln_rows(ALPHA * res_ref[...] + mix, g_ref[...], b_ref[...])


def _matmul_res_ln(x, w_bf16, res, g, b, tm):
    M, K = x.shape
    N = w_bf16.shape[1]
    tm = min(tm, M)
    assert M % tm == 0
    row = lambda i: (i, 0)
    fixed = lambda i: (0, 0)
    return pl.pallas_call(
        _matmul_res_ln_body,
        grid=(M // tm,),
        in_specs=[pl.BlockSpec((tm, K), row), pl.BlockSpec((K, N), fixed), pl.BlockSpec((tm, N), row),
                  pl.BlockSpec((1, N), fixed), pl.BlockSpec((1, N), fixed)],
        out_specs=pl.BlockSpec((tm, N), row),
        out_shape=jax.ShapeDtypeStruct((M, N), jnp.float32),
        compiler_params=pltpu.CompilerParams(
            dimension_semantics=("arbitrary",), vmem_limit_bytes=V7X_VMEM_LIMIT_BYTES),
        name="out_proj_ln",
    )(x, w_bf16, res, g.reshape(1, N), b.reshape(1, N))


EXPERT_ROWS = 256


def _start_row_gather(idx_ref, n_rows, src_hbm, dst_at, sem):
    def body(r, carry):
        t = idx_ref[0, 0, r]
        pltpu.make_async_copy(src_hbm.at[pl.ds(t, 1)], dst_at(r), sem).start()
        return carry
    lax.fori_loop(0, n_rows, body, 0, unroll=8)


def _moe_experts_body(blk_e_ref, n_used_ref, tok_ref, tok_next_ref, x_hbm, wg_ref, wu_ref, wd_ref,
                      y_ref, xbuf, sem):
    i = pl.program_id(0)
    n_used = n_used_ref[0]
    slot = i % 2

    def gather(idx_ref, s):
        _start_row_gather(idx_ref, EXPERT_ROWS, x_hbm, lambda r: xbuf.at[s, pl.ds(r, 1)], sem.at[s])

    @pl.when(i == 0)
    def _():
        gather(tok_ref, 0)

    @pl.when(i + 1 < n_used)
    def _():
        gather(tok_next_ref, 1 - slot)

    @pl.when(i < n_used)
    def _():
        pltpu.make_async_copy(x_hbm.at[pl.ds(0, EXPERT_ROWS)], xbuf.at[slot], sem.at[slot]).wait()
        x = xbuf[slot].astype(jnp.bfloat16)
        g = jnp.dot(x, wg_ref[0], preferred_element_type=jnp.float32)
        u = jnp.dot(x, wu_ref[0], preferred_element_type=jnp.float32)
        h = (g * jax.nn.sigmoid(g) * u).astype(jnp.bfloat16)
        y_ref[...] = jnp.dot(h, wd_ref[0], preferred_element_type=jnp.float32)

    @pl.when(i >= n_used)
    def _():
        y_ref[...] = jnp.zeros_like(y_ref)


def _moe_experts(x, row_tok, blk_e, n_used, wg, wu, wd):
    M, D = x.shape
    n_blocks = blk_e.shape[0]
    tok3 = row_tok.reshape(n_blocks, 1, EXPERT_ROWS)
    smem_blk = lambda f: pl.BlockSpec((1, 1, EXPERT_ROWS), f, memory_space=pltpu.SMEM)
    grid_spec = pltpu.PrefetchScalarGridSpec(
        num_scalar_prefetch=2,
        grid=(n_blocks,),
        in_specs=[smem_blk(lambda i, e, n: (i, 0, 0)),
                  smem_blk(lambda i, e, n: (jnp.minimum(i + 1, n_blocks - 1), 0, 0)),
                  pl.BlockSpec(memory_space=pl.ANY),
                  pl.BlockSpec((1, D, D_EXPERT), lambda i, e, n: (e[i], 0, 0)),
                  pl.BlockSpec((1, D, D_EXPERT), lambda i, e, n: (e[i], 0, 0)),
                  pl.BlockSpec((1, D_EXPERT, D), lambda i, e, n: (e[i], 0, 0))],
        out_specs=pl.BlockSpec((EXPERT_ROWS, D), lambda i, e, n: (i, 0)),
        scratch_shapes=[pltpu.VMEM((2, EXPERT_ROWS, D), jnp.float32), pltpu.SemaphoreType.DMA((2,))],
    )
    return pl.pallas_call(
        _moe_experts_body,
        grid_spec=grid_spec,
        out_shape=jax.ShapeDtypeStruct((n_blocks * EXPERT_ROWS, D), jnp.float32),
        compiler_params=pltpu.CompilerParams(
            dimension_semantics=("arbitrary",), vmem_limit_bytes=V7X_VMEM_LIMIT_BYTES),
        name="moe_experts",
    )(blk_e, n_used, tok3, tok3, x, wg, wu, wd)


COMBINE_ROWS = 128


def _moe_combine_body(pos_ref, pos_next_ref, wts_ref, x_ref, gu_ref, shd_ref, g_ref, b_ref, y_hbm,
                      o_ref, buf, sem):
    i = pl.program_id(0)
    n = pl.num_programs(0)
    slot = i % 2
    n_rows = COMBINE_ROWS * TOP_K

    def gather(idx_ref, s):
        def dst(r):
            k = jnp.bitwise_and(r, TOP_K - 1)
            token = lax.shift_right_logical(r, TOP_K.bit_length() - 1)
            return buf.at[s, pl.ds(k * COMBINE_ROWS + token, 1)]
        _start_row_gather(idx_ref, n_rows, y_hbm, dst, sem.at[s])

    @pl.when(i == 0)
    def _():
        gather(pos_ref, 0)

    @pl.when(i + 1 < n)
    def _():
        gather(pos_next_ref, 1 - slot)

    pltpu.make_async_copy(y_hbm.at[pl.ds(0, n_rows)], buf.at[slot], sem.at[slot]).wait()
    wts = wts_ref[...]
    routed = wts[:, 0:1] * buf[slot, pl.ds(0, COMBINE_ROWS)]
    for k in range(1, TOP_K):
        routed = routed + wts[:, k:k + 1] * buf[slot, pl.ds(k * COMBINE_ROWS, COMBINE_ROWS)]
    gu = gu_ref[...]
    g = gu[:, :D_EXPERT]
    u = gu[:, D_EXPERT:]
    h = (g * jax.nn.sigmoid(g) * u).astype(jnp.bfloat16)
    shared = jnp.dot(h, shd_ref[...], preferred_element_type=jnp.float32)
    o_ref[...] = _ln_rows(ALPHA * x_ref[...] + (routed + shared), g_ref[...], b_ref[...])


def _moe_combine(y_sorted, inv_pos, wts, x, gu, shd_bf16, g, b):
    M, D = x.shape
    tt = min(COMBINE_ROWS, M)
    assert tt == COMBINE_ROWS and M % tt == 0
    n_tiles = M // tt
    pos3 = inv_pos.reshape(n_tiles, 1, tt * TOP_K)
    row = lambda i: (i, 0)
    fixed = lambda i: (0, 0)
    smem_blk = lambda f: pl.BlockSpec((1, 1, tt * TOP_K), f, memory_space=pltpu.SMEM)
    return pl.pallas_call(
        _moe_combine_body,
        grid=(n_tiles,),
        in_specs=[smem_blk(lambda i: (i, 0, 0)),
                  smem_blk(lambda i: (jnp.minimum(i + 1, n_tiles - 1), 0, 0)),
                  pl.BlockSpec((tt, TOP_K), row),
                  pl.BlockSpec((tt, D), row),
                  pl.BlockSpec((tt, 2 * D_EXPERT), row),
                  pl.BlockSpec((D_EXPERT, D), fixed),
                  pl.BlockSpec((1, D), fixed), pl.BlockSpec((1, D), fixed),
                  pl.BlockSpec(memory_space=pl.ANY)],
        out_specs=pl.BlockSpec((tt, D), row),
        out_shape=jax.ShapeDtypeStruct((M, D), jnp.float32),
        scratch_shapes=[pltpu.VMEM((2, tt * TOP_K, D), jnp.float32), pltpu.SemaphoreType.DMA((2,))],
        compiler_params=pltpu.CompilerParams(
            dimension_semantics=("arbitrary",), vmem_limit_bytes=V7X_VMEM_LIMIT_BYTES),
        name="moe_combine_ln",
    )(pos3, pos3, wts, x, gu, shd_bf16, g.reshape(1, D), b.reshape(1, D), y_sorted)


def _moe_ln(x, l, W):
    M, D = x.shape
    rgu = _matmul(x, W['gu_router_bf16'][l], 256)
    scores = jax.nn.sigmoid(rgu[:, 2 * D_EXPERT:2 * D_EXPERT + N_EXPERTS])
    _, idx = lax.top_k(scores + _f32(W['router_bias'][l]), TOP_K)
    wts = jnp.take_along_axis(scores, idx, axis=-1)
    wts = wts / jnp.sum(wts, -1, keepdims=True) * ROUTED_SCALE

    A = M * TOP_K
    flat_e = idx.reshape(-1)
    order = jnp.argsort(flat_e)
    e_sorted = flat_e[order]
    tok_sorted = (order // TOP_K).astype(jnp.int32)
    counts = jnp.bincount(flat_e, length=N_EXPERTS)
    padded = (counts + EXPERT_ROWS - 1) // EXPERT_ROWS * EXPERT_ROWS
    pad_end = jnp.cumsum(padded)
    pad_start = pad_end - padded
    start = jnp.cumsum(counts) - counts
    dest = (pad_start[e_sorted] + jnp.arange(A) - start[e_sorted]).astype(jnp.int32)
    n_blocks = -(-A // EXPERT_ROWS) + N_EXPERTS
    P = n_blocks * EXPERT_ROWS
    row_tok = jnp.zeros((P,), jnp.int32).at[dest].set(tok_sorted)
    inv_pos = jnp.zeros((A,), jnp.int32).at[order].set(dest)
    blk_e = jnp.minimum(jnp.searchsorted(pad_end, jnp.arange(n_blocks) * EXPERT_ROWS, side='right'),
                        N_EXPERTS - 1).astype(jnp.int32)
    n_used = (pad_end[-1:] // EXPERT_ROWS).astype(jnp.int32)

    y_sorted = _moe_experts(x, row_tok, blk_e, n_used, W['e_gate_bf16'][l], W['e_up_bf16'][l],
                            W['e_down_bf16'][l])
    return _moe_combine(y_sorted, inv_pos, wts, x, rgu, W['sh_down_bf16'][l], W['ln2_g'][l], W['ln2_b'][l])


DA_COL_BLOCK = (W_RET + W_HG + W_RW) // GROUP_W
ATTN_BLOCK = 256
LOG2E = 1.4426950408889634
BF16 = jnp.bfloat16
F32 = jnp.float32


def _attn_prompt_body(lam_ref, q_ref, k_ref, v_ref, nw_ref, o_ref, kbf, vh, qs, m_scr, l_scr, acc_scr):
    i = pl.program_id(1)
    T = ATTN_BLOCK
    n_hm = 2 * N_HEADS

    @pl.when(i == 0)
    def _():
        kbf[...] = k_ref[...].astype(BF16)
        v = v_ref[...]
        for h in range(N_HEADS):
            vh[h] = v[:, h * HEAD_DIM:(h + 1) * HEAD_DIM].astype(BF16)

    q = q_ref[...] * (DA_DQK ** -0.5 * LOG2E)
    lane = lax.broadcasted_iota(jnp.int32, (T, GROUP_W), 1)
    for hm in range(n_hm):
        keep = (lane >= hm * DA_DQK) & (lane < (hm + 1) * DA_DQK)
        qs[pl.ds(hm * T, T), :] = jnp.where(keep, q, 0.0).astype(BF16)
    m_scr[...] = jnp.full_like(m_scr, MASK_NEG)
    l_scr[...] = jnp.zeros_like(l_scr)
    acc_scr[...] = jnp.zeros_like(acc_scr)

    def kv_block(j, masked):
        kb = kbf[pl.ds(j * T, T), :]
        for h in range(N_HEADS):
            rows = pl.ds(h * 2 * T, 2 * T)
            s = lax.dot_general(qs[rows, :], kb, (((1,), (1,)), ((), ())), preferred_element_type=F32)
            if masked:
                qpos = lax.broadcasted_iota(jnp.int32, (2 * T, T), 0) % T
                kpos = lax.broadcasted_iota(jnp.int32, (2 * T, T), 1)
                s = jnp.where(kpos <= qpos, s, MASK_NEG)
            m_old = m_scr[rows, :]
            m_new = jnp.maximum(m_old, jnp.max(s, -1, keepdims=True))
            alpha = jnp.exp2(m_old - m_new)
            p = jnp.exp2(s - m_new)
            l_scr[rows, :] = alpha * l_scr[rows, :] + jnp.sum(p, -1, keepdims=True)
            pv = jnp.dot(p.astype(BF16), vh[h, pl.ds(j * T, T), :], preferred_element_type=F32)
            acc_scr[rows, :] = alpha * acc_scr[rows, :] + pv
            m_scr[rows, :] = m_new

    def body(j, c):
        kv_block(j, False)
        return c
    lax.fori_loop(0, i, body, 0)
    kv_block(i, True)

    lam = lam_ref[0]
    outs = []
    for h in range(N_HEADS):
        r0 = pl.ds(h * 2 * T, T)
        r1 = pl.ds(h * 2 * T + T, T)
        o = acc_scr[r0, :] / l_scr[r0, :] - lam * (acc_scr[r1, :] / l_scr[r1, :])
        outs.append(o * lax.rsqrt(jnp.mean(o * o, -1, keepdims=True) + NORM_EPS))
    o_ref[...] = jnp.concatenate(outs, -1) * nw_ref[...]


def _attn_prompt(proj, lam, norm_scale, B, S):
    T = ATTN_BLOCK
    assert S % T == 0
    nq = S // T
    n_rows = 2 * N_HEADS * T
    return pl.pallas_call(
        _attn_prompt_body,
        grid=(B, nq),
        in_specs=[pl.BlockSpec(memory_space=pltpu.SMEM),
                  pl.BlockSpec((T, GROUP_W), lambda b, i: (b * nq + i, DA_COL_BLOCK)),
                  pl.BlockSpec((S, GROUP_W), lambda b, i: (b, DA_COL_BLOCK + 1)),
                  pl.BlockSpec((S, GROUP_W), lambda b, i: (b, DA_COL_BLOCK + 2)),
                  pl.BlockSpec((1, GROUP_W), lambda b, i: (0, 0))],
        out_specs=pl.BlockSpec((T, GROUP_W), lambda b, i: (b * nq + i, 0)),
        out_shape=jax.ShapeDtypeStruct((B * S, GROUP_W), F32),
        scratch_shapes=[pltpu.VMEM((S, GROUP_W), BF16),
                        pltpu.VMEM((N_HEADS, S, HEAD_DIM), BF16),
                        pltpu.VMEM((n_rows, GROUP_W), BF16),
                        pltpu.VMEM((n_rows, 1), F32),
                        pltpu.VMEM((n_rows, 1), F32),
                        pltpu.VMEM((n_rows, HEAD_DIM), F32)],
        compiler_params=pltpu.CompilerParams(
            dimension_semantics=("arbitrary", "arbitrary"), vmem_limit_bytes=V7X_VMEM_LIMIT_BYTES),
        name="diff_attn_prompt",
    )(lam.reshape(1), proj, proj, proj, norm_scale.reshape(1, GROUP_W))


RW_COL_BLOCK = (W_RET + W_HG) // GROUP_W
RW_CHUNK = 64


def _dot(a, b):
    return jnp.dot(a, b, preferred_element_type=F32)


def _dot_nt(a, b):
    return lax.dot_general(a, b, (((1,), (1,)), ((), ())), preferred_element_type=F32)


def _dot_tn(a, b):
    return lax.dot_general(a, b, (((0,), (0,)), ((), ())), preferred_element_type=F32)


def _split3(x):
    hi = x.astype(BF16)
    r1 = x - hi.astype(F32)
    mid = r1.astype(BF16)
    lo = (r1 - mid.astype(F32)).astype(BF16)
    return hi, mid, lo


def _dot_exact_rhs(x, m_bf16):
    hi, mid, lo = _split3(x)
    return _dot(hi, m_bf16) + _dot(mid, m_bf16) + _dot(lo, m_bf16)


def _dot_exact_lhs(m_bf16, x):
    hi, mid, lo = _split3(x)
    return _dot(m_bf16, hi) + _dot(m_bf16, mid) + _dot(m_bf16, lo)


def _dot3(a, b):
    ah = a.astype(BF16)
    al = (a - ah.astype(F32)).astype(BF16)
    bh = b.astype(BF16)
    bl = (b - bh.astype(F32)).astype(BF16)
    return _dot(ah, bh) + _dot(ah, bl) + _dot(al, bh)


def _rwkv_body(n_valid, pr_ref, pk_ref, pv_ref, pl_ref, shift0_ref, s0_ref, mu_ref, w0_ref, wup_ref, a0_ref,
               aup_ref, gup_ref, kk_ref, ka_ref, rk_ref, lnw_ref, lnb_ref, o_ref, s_out_ref, sbd, carry):
    c_idx = pl.program_id(1)
    T = RW_CHUNK
    HT = N_HEADS * T

    @pl.when(c_idx == 0)
    def _():
        sbd[...] = s0_ref[0]
        carry[...] = shift0_ref[0]

    row = lax.broadcasted_iota(jnp.int32, (T, GROUP_W), 0)
    lane_head = lax.broadcasted_iota(jnp.int32, (T, GROUP_W), 1) // HEAD_DIM

    def shifted(p_ref, blk):
        cols = slice(blk * GROUP_W, (blk + 1) * GROUP_W)
        p = p_ref[...]
        prev = jnp.where(row == 0, carry[:, cols], pltpu.roll(p, 1, axis=0))
        carry[:, cols] = p[T - 1:T, :]
        return p + (prev - p) * mu_ref[:, cols]

    xr = shifted(pr_ref, 0)
    xk = shifted(pk_ref, 1)
    xv = shifted(pv_ref, 2)
    xl = shifted(pl_ref, 3)

    gi = lax.broadcasted_iota(jnp.int32, (GROUP_W, GROUP_W), 0) // HEAD_DIM
    gj = lax.broadcasted_iota(jnp.int32, (GROUP_W, GROUP_W), 1) // HEAD_DIM
    seg = (gi == gj).astype(BF16)

    z = -(w0_ref[...] + _dot(jnp.tanh(xl).astype(BF16), wup_ref[...]))
    softplus = jnp.maximum(z, 0.0) + jnp.log(1.0 + jnp.exp(-jnp.abs(z)))
    lw = -jnp.exp(-softplus - 0.5)
    a = jax.nn.sigmoid(a0_ref[...] + _dot(xl.astype(BF16), aup_ref[...]))
    g = _dot(jax.nn.sigmoid(xl).astype(BF16), gup_ref[...])
    kk = xk * kk_ref[...]
    kkn = kk / jnp.maximum(jnp.sqrt(_dot_exact_rhs(kk * kk, seg)), 1e-12)
    k_mod = xk * (1.0 + (a - 1.0) * ka_ref[...])
    an = -kkn
    bb = kkn * a
    if n_valid < T:
        valid = row < n_valid
        lw = jnp.where(valid, lw, 0.0)
        an = jnp.where(valid, an, 0.0)
        k_mod = jnp.where(valid, k_mod, 0.0)

    ti = lax.broadcasted_iota(jnp.int32, (T, T), 0)
    tj = lax.broadcasted_iota(jnp.int32, (T, T), 1)
    c = _dot_exact_lhs((tj <= ti).astype(BF16), lw)
    c_last = c[T - 1:T, :]
    inv_dec = jnp.exp(-c)
    to_end = jnp.exp(c_last - c)
    a_t = an * jnp.exp(c - lw)
    b_t = bb * inv_dec
    k_t = k_mod * inv_dec
    r_t = xr * jnp.exp(c)

    def stack(x):
        return jnp.concatenate([jnp.where(lane_head == h, x, 0.0) for h in range(N_HEADS)], axis=0)

    def tile(x):
        return jnp.concatenate([x] * N_HEADS, axis=0)

    ri = lax.broadcasted_iota(jnp.int32, (HT, HT), 0)
    ci = lax.broadcasted_iota(jnp.int32, (HT, HT), 1)
    same_head = (ri // T) == (ci // T)
    m_strict = same_head & ((ci % T) < (ri % T))
    m_incl = same_head & ((ci % T) <= (ri % T))

    a_s = stack(a_t).astype(BF16)
    r_s = stack(r_t).astype(BF16)
    b_tl = tile(b_t).astype(BF16)
    k_tl = tile(k_t).astype(BF16)
    a_ab = jnp.where(m_strict, _dot_nt(a_s, b_tl), 0.0)
    a_ak = jnp.where(m_strict, _dot_nt(a_s, k_tl), 0.0)
    a_rb = jnp.where(m_incl, _dot_nt(r_s, b_tl), 0.0)
    a_rk = jnp.where(m_incl, _dot_nt(r_s, k_tl), 0.0)

    s_old = sbd[...]
    s_bf = s_old.astype(BF16)
    v_mb = stack(xv).astype(BF16)

    u = _dot_nt(a_s, s_bf) + _dot(a_ak.astype(BF16), v_mb)
    pw = a_ab
    n_steps = max(1, (T - 1).bit_length())
    for step in range(n_steps):
        u = u + _dot3(pw, u)
        if step + 1 < n_steps:
            pw = _dot3(pw, pw)

    u_b = u.astype(BF16)
    y_s = _dot_nt(r_s, s_bf) + _dot(a_rb.astype(BF16), u_b) + _dot(a_rk.astype(BF16), v_mb)
    y = y_s[0:T]
    for h in range(1, N_HEADS):
        y = y + y_s[h * T:(h + 1) * T]

    s_new = (s_old * jnp.exp(c_last)
             + _dot_tn(u_b, stack(bb * to_end).astype(BF16)) + _dot_tn(v_mb, stack(k_mod * to_end).astype(BF16)))
    sbd[...] = s_new

    @pl.when(c_idx == pl.num_programs(1) - 1)
    def _():
        s_out_ref[0] = s_new

    inv_n = 1.0 / HEAD_DIM
    yc = y - _dot_exact_rhs(y, seg) * inv_n
    var = _dot_exact_rhs(yc * yc, seg) * inv_n
    yn = yc * lax.rsqrt(var + RW_LNX_EPS) * lnw_ref[...] + lnb_ref[...]
    bonus = _dot_exact_rhs(xr * k_mod * rk_ref[...], seg) * xv
    o_ref[...] = (yn + bonus) * g


def _rwkv_mix(p, col_block0, shift0, s0_bd, n_valid, B, L, wts):
    T = RW_CHUNK
    assert L % T == 0
    nc = L // T
    blk = lambda j: pl.BlockSpec((T, GROUP_W), lambda b, c: (b * nc + c, col_block0 + j))
    fixed = lambda shape: pl.BlockSpec(shape, lambda b, c: (0,) * len(shape))
    vec = fixed((1, GROUP_W))
    mat = fixed((GROUP_W, GROUP_W))
    return pl.pallas_call(
        functools.partial(_rwkv_body, n_valid),
        grid=(B, nc),
        in_specs=[blk(0), blk(1), blk(2), blk(3),
                  pl.BlockSpec((1, 1, W_RW), lambda b, c: (b, 0, 0)),
                  pl.BlockSpec((1, GROUP_W, GROUP_W), lambda b, c: (b, 0, 0)),
                  fixed((1, W_RW)), vec, mat, vec, mat, mat, vec, vec, vec, vec, vec],
        out_specs=[pl.BlockSpec((T, GROUP_W), lambda b, c: (b * nc + c, 0)),
                   pl.BlockSpec((1, GROUP_W, GROUP_W), lambda b, c: (b, 0, 0))],
        out_shape=[jax.ShapeDtypeStruct((B * L, GROUP_W), F32),
                   jax.ShapeDtypeStruct((B, GROUP_W, GROUP_W), F32)],
        scratch_shapes=[pltpu.VMEM((GROUP_W, GROUP_W), F32), pltpu.VMEM((1, W_RW), F32)],
        compiler_params=pltpu.CompilerParams(
            dimension_semantics=("arbitrary", "arbitrary"), vmem_limit_bytes=V7X_VMEM_LIMIT_BYTES),
        name="rwkv7_mix",
    )(p, p, p, p, shift0, s0_bd, *wts)


def _rwkv_weights(W, l):
    z = lambda n: jnp.zeros((n, GROUP_W), F32)
    wup = jnp.concatenate([_f32(W['rw_w_up'][l]), z(GROUP_W - RW_DECAY_LORA)], 0).astype(BF16)
    aup = jnp.concatenate([z(RW_DECAY_LORA), _f32(W['rw_a_up'][l]), z(RW_GATE_LORA)], 0).astype(BF16)
    gup = jnp.concatenate([z(RW_DECAY_LORA + RW_A_LORA), _f32(W['rw_g_up'][l])], 0).astype(BF16)
    r1 = lambda t: _f32(t).reshape(1, -1)
    return (r1(W['rw_mu'][l]), r1(W['rw_w0'][l]), wup, r1(W['rw_a0'][l]), aup, gup, r1(W['rw_k_k'][l]),
            r1(W['rw_k_a'][l]), r1(W['rw_r_k'][l]), r1(W['rw_lnx_w'][l]), r1(W['rw_lnx_b'][l]))


def _state_to_bd(s):
    eye = jnp.eye(N_HEADS, dtype=s.dtype)
    return (s[:, :, :, None, :] * eye[None, :, None, :, None]).reshape(s.shape[0], GROUP_W, GROUP_W)


def _bd_to_state(sbd):
    s5 = sbd.reshape(sbd.shape[0], N_HEADS, HEAD_DIM, N_HEADS, HEAD_DIM)
    return jnp.stack([s5[:, h, :, h, :] for h in range(N_HEADS)], axis=1)


def _layernorm(x, g, b):
    xf = _f32(x)
    mu = jnp.mean(xf, -1, keepdims=True)
    xc = xf - mu
    var = jnp.mean(xc * xc, -1, keepdims=True)
    return (xc * lax.rsqrt(var + LN_EPS) * _f32(g) + _f32(b)).astype(x.dtype)


def _rms(x, eps=NORM_EPS):
    return x * lax.rsqrt(jnp.mean(x * x, -1, keepdims=True) + eps)


def _head_ln(x, eps):
    xc = x - jnp.mean(x, -1, keepdims=True)
    return xc * lax.rsqrt(jnp.mean(xc * xc, -1, keepdims=True) + eps)


def _rotary(x, pos):
    half = x.shape[-1] // 2
    freq = 1.0 / (ROPE_BASE ** jnp.linspace(0.0, 1.0, half, dtype=jnp.float32))
    ang = _f32(pos)[:, None] * freq[None, :]
    cos = jnp.cos(ang)[None, :, None, :]
    sin = jnp.sin(ang)[None, :, None, :]
    x1, x2 = x[..., :half], x[..., half:]
    return jnp.concatenate([x1 * cos - x2 * sin, x1 * sin + x2 * cos], -1)


def _chunk_scan(step, xs, s0):
    B, L = xs[0].shape[:2]
    c = CHUNK if L % CHUNK == 0 else L
    n = L // c
    chunked = tuple(jnp.moveaxis(a.reshape(B, n, c, *a.shape[2:]), 1, 0) for a in xs)
    s, out = lax.scan(step, s0, chunked)
    return jnp.moveaxis(out, 0, 1).reshape(B, L, *out.shape[3:]), s


def _retention_step(S, xs, log_gamma):
    q, k, v = xs
    C = q.shape[1]
    t = jnp.arange(C, dtype=jnp.float32)
    gap = t[:, None] - t[None, :]
    dmask = jnp.where(gap >= 0, jnp.exp(jnp.maximum(gap, 0.0)[None] * log_gamma[:, None, None]), 0.0)
    scores = jnp.einsum('bqhd,bkhd->bhqk', q, k) * dmask[None]
    o = jnp.einsum('bhqk,bkhe->bqhe', scores, v)
    o = o + jnp.einsum('bqhd,bhde->bqhe', q, S) * jnp.exp((t + 1.0)[:, None] * log_gamma[None, :])[None, :, :, None]
    k_w = k * jnp.exp((C - 1.0 - t)[:, None] * log_gamma[None, :])[None, :, :, None]
    S = jnp.exp(C * log_gamma)[None, :, None, None] * S + jnp.einsum('bkhd,bkhe->bhde', k_w, v)
    return S, o


def _hgrn2_step(S, xs):
    q, k, logf, v = xs
    C = q.shape[1]
    b = jnp.cumsum(logf, axis=1)
    causal = jnp.tril(jnp.ones((C, C), bool))[None, :, :, None, None]
    diff = jnp.where(causal, b[:, :, None] - b[:, None, :], 0.0)
    decay = jnp.where(causal, jnp.exp(diff), 0.0)
    attn = jnp.einsum('bqhd,bqkhd,bkhd->bhqk', q, decay, k)
    o = jnp.einsum('bhqk,bkhe->bqhe', attn, v) + jnp.einsum('bqhd,bhde->bqhe', q * jnp.exp(b), S)
    b_last = b[:, -1]
    S = jnp.exp(b_last)[..., None] * S + jnp.einsum('bkhd,bkhe->bhde', k * jnp.exp(b_last[:, None] - b), v)
    return S, o


def _rwkv_scan(S0, r, w, k, v, a, b):
    def step(S, xs):
        r_t, w_t, k_t, v_t, a_t, b_t = xs
        S = (S * w_t[:, :, None, :]
             + jnp.einsum('bhvk,bhk->bhv', S, a_t)[..., None] * b_t[:, :, None, :]
             + v_t[..., None] * k_t[:, :, None, :])
        return S, jnp.einsum('bhvk,bhk->bhv', S, r_t)
    xs = tuple(jnp.moveaxis(t, 1, 0) for t in (r, w, k, v, a, b))
    S, y = lax.scan(step, S0, xs)
    return jnp.moveaxis(y, 0, 1), S


def _rwkv7(p_rw, prev_row, S0, l, W):
    B, L, _ = p_rw.shape
    p = _f32(p_rw)
    prev = jnp.concatenate([_f32(prev_row)[:, None], p[:, :-1]], axis=1)
    xs = p + (prev - p) * _f32(W['rw_mu'][l])
    r, k, v, wd, ad, gd = jnp.split(xs, RW_SPLITS, axis=-1)
    w_log = -jax.nn.softplus(-(_f32(W['rw_w0'][l]) + jnp.tanh(wd) @ _f32(W['rw_w_up'][l]))) - 0.5
    decay = jnp.exp(-jnp.exp(w_log))
    a = jax.nn.sigmoid(_f32(W['rw_a0'][l]) + ad @ _f32(W['rw_a_up'][l]))
    g = jax.nn.sigmoid(gd) @ _f32(W['rw_g_up'][l])
    hd = lambda t: t.reshape(B, L, N_HEADS, HEAD_DIM)
    kk = hd(k * _f32(W['rw_k_k'][l]))
    kk = kk / jnp.maximum(jnp.sqrt(jnp.sum(kk * kk, -1, keepdims=True)), 1e-12)
    k = hd(k * (1.0 + (a - 1.0) * _f32(W['rw_k_a'][l])))
    r, v, a_h = hd(r), hd(v), hd(a)
    y, S = _rwkv_scan(_f32(S0), r, hd(decay), k, v, -kk, kk * a_h)
    y = _head_ln(y, RW_LNX_EPS).reshape(B, L, GROUP_W) * _f32(W['rw_lnx_w'][l]) + _f32(W['rw_lnx_b'][l])
    bonus = (jnp.sum(r * k * _f32(W['rw_r_k'][l]), -1, keepdims=True) * v).reshape(B, L, GROUP_W)
    return (y + bonus) * g, S, p_rw[:, -1]


def _diff_attn_prompt(q, k, v, lam):
    B, S = q.shape[:2]
    qb_sz = Q_BLOCK if S % Q_BLOCK == 0 else S
    nb = S // qb_sz
    qb = jnp.moveaxis(q.reshape(B, nb, qb_sz, *q.shape[2:]), 1, 0)
    kpos = jnp.arange(S)
    scale = DA_DQK ** -0.5

    def block(args):
        qi, i = args
        s = jnp.einsum('bqhmd,bkhmd->bhmqk', qi, k) * scale
        qpos = i * qb_sz + jnp.arange(qb_sz)
        s = jnp.where(kpos[None, :] <= qpos[:, None], s, MASK_NEG)
        p = jax.nn.softmax(s, axis=-1)
        return jnp.einsum('bhqk,bkhe->bqhe', p[:, :, 0] - lam * p[:, :, 1], v)

    o = lax.map(block, (qb, jnp.arange(nb)))
    return jnp.moveaxis(o, 0, 1).reshape(B, S, N_HEADS, HEAD_DIM)


def _diff_attn_sample(q, k, v, k_past, v_past, lam):
    L = q.shape[1]
    P = k_past.shape[1]
    scale = DA_DQK ** -0.5
    s_past = jnp.einsum('bqhmd,bkhmd->bhmqk', q, k_past) * scale
    s_new = jnp.einsum('bqhmd,bkhmd->bhmqk', q, k) * scale
    s_new = jnp.where(jnp.tril(jnp.ones((L, L), bool)), s_new, MASK_NEG)
    p = jax.nn.softmax(jnp.concatenate([_f32(s_past), s_new], -1), axis=-1)
    pd = p[:, :, 0] - lam * p[:, :, 1]
    return (jnp.einsum('bhqk,bkhe->bqhe', pd[..., :P], v_past)
            + jnp.einsum('bhqk,bkhe->bqhe', pd[..., P:], v))


def _swiglu(x, wg, wu, wd):
    return (jax.nn.silu(x @ wg) * (x @ wu)) @ wd


def _grouped_experts(xf, idx, wts, wg, wu, wd):
    M, D = xf.shape
    A = M * TOP_K
    flat_e = idx.reshape(-1)
    order = jnp.argsort(flat_e)
    e_sorted = flat_e[order]
    tok_sorted = (order // TOP_K).astype(jnp.int32)
    w_sorted = wts.reshape(-1)[order]
    counts = jnp.bincount(flat_e, length=N_EXPERTS)
    padded = (counts + MOE_BLOCK - 1) // MOE_BLOCK * MOE_BLOCK
    pad_end = jnp.cumsum(padded)
    pad_start = pad_end - padded
    start = jnp.cumsum(counts) - counts
    dest = pad_start[e_sorted] + jnp.arange(A) - start[e_sorted]
    n_blocks = -(-A // MOE_BLOCK) + N_EXPERTS
    P = n_blocks * MOE_BLOCK
    row_tok = jnp.zeros((P,), jnp.int32).at[dest].set(tok_sorted)
    row_w = jnp.zeros((P,), jnp.float32).at[dest].set(w_sorted)
    blk_e = jnp.minimum(jnp.searchsorted(pad_end, jnp.arange(n_blocks) * MOE_BLOCK, side='right'), N_EXPERTS - 1)

    def body(acc, args):
        toks, rw, e = args
        yb = _swiglu(xf[toks], wg[e], wu[e], wd[e])
        return acc.at[toks].add(_f32(yb) * rw[:, None]), None

    acc, _ = lax.scan(body, jnp.zeros((M, D), jnp.float32),
                      (row_tok.reshape(n_blocks, MOE_BLOCK), row_w.reshape(n_blocks, MOE_BLOCK), blk_e))
    return acc.astype(xf.dtype)


def _moe(x, l, W):
    B, L, D = x.shape
    xf = x.reshape(-1, D)
    scores = jax.nn.sigmoid(_f32(xf @ W['router_w'][l]))
    _, idx = lax.top_k(scores + _f32(W['router_bias'][l]), TOP_K)
    wts = jnp.take_along_axis(scores, idx, axis=-1)
    wts = wts / jnp.sum(wts, -1, keepdims=True) * ROUTED_SCALE
    routed = _grouped_experts(xf, idx, wts, W['e_gate'][l], W['e_up'][l], W['e_down'][l])
    shared = _swiglu(xf, W['sh_gate'][l], W['sh_up'][l], W['sh_down'][l])
    return (routed + shared).reshape(B, L, D)


def _layer(l, x, pos, ret_s0, hg_s0, rw_s0, shift0, kv_past, W):
    B, L, _ = x.shape
    dt = x.dtype
    proj2d = _matmul(x.reshape(B * L, D_MODEL), W['w_in_bf16'][l], 256)
    proj = proj2d.reshape(B, L, N_COLS)
    p_ret, p_hg, p_rw, p_da = jnp.split(proj, [W_RET, W_RET + W_HG, W_RET + W_HG + W_RW], axis=-1)
    heads = lambda t, d: t.reshape(B, L, N_HEADS, d)

    r_q, r_k, r_v, r_g = jnp.split(_f32(p_ret), 4, axis=-1)
    log_gamma = jnp.log1p(-jnp.exp2(-5.0 - jnp.arange(N_HEADS, dtype=jnp.float32)))
    q = _rotary(heads(r_q, HEAD_DIM), pos)
    k = _rotary(heads(r_k, HEAD_DIM), pos) * HEAD_DIM ** -0.5
    o, ret_s = _chunk_scan(functools.partial(_retention_step, log_gamma=log_gamma),
                           (q, k, heads(r_v, HEAD_DIM)), _f32(ret_s0))
    o_a = _rms(o).reshape(B, L, GROUP_W) * jax.nn.silu(r_g)

    h_q, h_f, h_i, h_g = jnp.split(_f32(p_hg), [N_HEADS * HG_DK, 2 * N_HEADS * HG_DK,
                                                2 * N_HEADS * HG_DK + N_HEADS * HEAD_DIM], axis=-1)
    lb_soft = jax.nn.softmax(_f32(W['hg_lb']), axis=0)
    lb = (jnp.cumsum(lb_soft, axis=0) - lb_soft[0])[l]
    if l == 0:
        log_f = jax.nn.log_sigmoid(h_f)
    else:
        log_f = jnp.logaddexp(jnp.log(lb), jnp.log1p(-lb) + jax.nn.log_sigmoid(h_f))
    k_in = (1.0 - lb) * jax.nn.sigmoid(-h_f)
    q_h = jax.nn.silu(h_q) * HG_DK ** -0.5
    o, hg_s = _chunk_scan(_hgrn2_step, (heads(q_h, HG_DK), heads(k_in, HG_DK), heads(log_f, HG_DK),
                                        heads(h_i, HEAD_DIM)), _f32(hg_s0))
    o_b = (_rms(o) * _f32(W['hg_norm_w'][l])).reshape(B, L, GROUP_W) * jax.nn.silu(h_g)

    rw_wts = _rwkv_weights(W, l)
    shift3 = _f32(shift0)[:, None, :]
    if L % RW_CHUNK == 0:
        o_c, rw_bd = _rwkv_mix(proj2d, RW_COL_BLOCK, shift3, _state_to_bd(_f32(rw_s0)), RW_CHUNK, B, L, rw_wts)
    else:
        p_pad = jnp.pad(p_rw, ((0, 0), (0, RW_CHUNK - L), (0, 0))).reshape(B * RW_CHUNK, W_RW)
        o_c, rw_bd = _rwkv_mix(p_pad, 0, shift3, _state_to_bd(_f32(rw_s0)), L, B, RW_CHUNK, rw_wts)
        o_c = o_c.reshape(B, RW_CHUNK, GROUP_W)[:, :L]
    o_c = o_c.reshape(B, L, GROUP_W)
    rw_s = _bd_to_state(rw_bd)
    shift_new = p_rw[:, -1]

    d_q, d_k, d_v = jnp.split(p_da, 3, axis=-1)
    q5 = _f32(d_q).reshape(B, L, N_HEADS, 2, DA_DQK)
    k5 = _f32(d_k).reshape(B, L, N_HEADS, 2, DA_DQK)
    v4 = _f32(d_v).reshape(B, L, N_HEADS, HEAD_DIM)
    lam_init = 0.8 - 0.6 * math.exp(-0.3 * l)
    lam = (jnp.exp(jnp.sum(_f32(W['da_lq1'][l]) * _f32(W['da_lk1'][l])))
           - jnp.exp(jnp.sum(_f32(W['da_lq2'][l]) * _f32(W['da_lk2'][l]))) + lam_init)
    if kv_past is None:
        norm_scale = jnp.tile(_f32(W['da_norm_w'][l]) * (1.0 - lam_init), N_HEADS)
        o_d = _attn_prompt(proj2d, lam, norm_scale, B, L).reshape(B, L, GROUP_W)
    else:
        k_past, v_past = kv_past
        o = _diff_attn_sample(q5, k5, v4, k_past.reshape(B, -1, N_HEADS, 2, DA_DQK), v_past, lam)
        o_d = (_rms(o) * _f32(W['da_norm_w'][l]) * (1.0 - lam_init)).reshape(B, L, GROUP_W)

    mix_in = jnp.concatenate([o_a, o_b, o_c, o_d], -1).astype(dt)
    x1 = _matmul_res_ln(mix_in.reshape(B * L, D_MODEL), W['w_o_bf16'][l], x.reshape(B * L, D_MODEL),
                        W['ln1_g'][l], W['ln1_b'][l], 256)
    x = _moe_ln(x1, l, W).reshape(B, L, D_MODEL)
    new = (ret_s.astype(dt), hg_s.astype(dt), rw_s.astype(dt), shift_new,
           d_k.reshape(B, L, N_HEADS, HEAD_DIM), d_v.reshape(B, L, N_HEADS, HEAD_DIM))
    return x, new


def _prepare_weights(W):
    W = dict(W)
    depth = W['w_in'].shape[0]
    W['w_in_bf16'] = W['w_in'].astype(BF16)
    W['w_o_bf16'] = W['w_o'].astype(BF16)
    router_pad = jnp.zeros((depth, D_MODEL, 128 - N_EXPERTS), W['router_w'].dtype)
    W['gu_router_bf16'] = jnp.concatenate([W['sh_gate'], W['sh_up'], W['router_w'], router_pad], -1).astype(BF16)
    for name in ('e_gate', 'e_up', 'e_down', 'sh_down'):
        W[name + '_bf16'] = W[name].astype(BF16)
    return W


def kernel(x_prompt, x_sample, state_ret, state_hgrn, state_rwkv, state_rwkv_shift, cache_k, cache_v,
           page_table, w_in, w_o, hg_lb, hg_norm_w, rw_mu, rw_w0, rw_w_up, rw_a0, rw_a_up, rw_g_up,
           rw_k_k, rw_k_a, rw_r_k, rw_lnx_w, rw_lnx_b, da_lq1, da_lk1, da_lq2, da_lk2, da_norm_w,
           ln1_g, ln1_b, router_w, router_bias, e_gate, e_up, e_down, sh_gate, sh_up, sh_down,
           ln2_g, ln2_b):
    W = {'w_in': w_in, 'w_o': w_o, 'hg_lb': hg_lb, 'hg_norm_w': hg_norm_w, 'rw_mu': rw_mu,
         'rw_w0': rw_w0, 'rw_w_up': rw_w_up, 'rw_a0': rw_a0, 'rw_a_up': rw_a_up, 'rw_g_up': rw_g_up,
         'rw_k_k': rw_k_k, 'rw_k_a': rw_k_a, 'rw_r_k': rw_r_k, 'rw_lnx_w': rw_lnx_w, 'rw_lnx_b': rw_lnx_b,
         'da_lq1': da_lq1, 'da_lk1': da_lk1, 'da_lq2': da_lq2, 'da_lk2': da_lk2, 'da_norm_w': da_norm_w,
         'ln1_g': ln1_g, 'ln1_b': ln1_b, 'router_w': router_w, 'router_bias': router_bias,
         'e_gate': e_gate, 'e_up': e_up, 'e_down': e_down, 'sh_gate': sh_gate, 'sh_up': sh_up,
         'sh_down': sh_down, 'ln2_g': ln2_g, 'ln2_b': ln2_b}
    W = _prepare_weights(W)
    B, S, _ = x_prompt.shape
    DB, L, _ = x_sample.shape
    past_len = page_table.shape[1] * PAGE_SIZE
    pos_p = jnp.arange(S)
    pos_s = past_len + jnp.arange(L)
    zero_ret = jnp.zeros((B, N_HEADS, HEAD_DIM, HEAD_DIM), jnp.float32)
    zero_hg = jnp.zeros((B, N_HEADS, HG_DK, HEAD_DIM), jnp.float32)
    zero_shift = jnp.zeros((B, W_RW), x_prompt.dtype)
    yp, ys = x_prompt, x_sample
    new_p, new_s = [], []
    for l in range(DEPTH):
        yp, st = _layer(l, yp, pos_p, zero_ret, zero_hg, zero_ret, zero_shift, None, W)
        new_p.append(st)
        k_past = cache_k[l][page_table].reshape(DB, past_len, N_HEADS, HEAD_DIM)
        v_past = cache_v[l][page_table].reshape(DB, past_len, N_HEADS, HEAD_DIM)
        ys, st = _layer(l, ys, pos_s, state_ret[l], state_hgrn[l], state_rwkv[l], state_rwkv_shift[l],
                        (k_past, v_past), W)
        new_s.append(st)

    def stk(sts, i):
        return jnp.stack([s[i] for s in sts])

    return (yp, ys, stk(new_p, 0), stk(new_s, 0), stk(new_p, 1), stk(new_s, 1), stk(new_p, 2), stk(new_s, 2),
            stk(new_p, 3), stk(new_s, 3), stk(new_p, 4), stk(new_p, 5), stk(new_s, 4), stk(new_s, 5))
```

```python
import math, functools
import jax, jax.numpy as jnp
from jax import lax
from jax.experimental import pallas as pl
from jax.experimental.pallas import tpu as pltpu

D_MODEL = 1024
DEPTH = 2
PAGE_SIZE = 128
N_MIXERS = 4
GROUP_W = D_MODEL // N_MIXERS
HEAD_DIM = 64
N_HEADS = GROUP_W // HEAD_DIM
ROPE_BASE = 10000.0
HG_DK = 128
RW_DECAY_LORA = 64
RW_A_LORA = 64
RW_GATE_LORA = 128
RW_LNX_EPS = 6.4e-4
DA_DQK = HEAD_DIM // 2
W_RET = 3 * N_HEADS * HEAD_DIM + GROUP_W
W_HG = 2 * N_HEADS * HG_DK + N_HEADS * HEAD_DIM + GROUP_W
W_RW = 3 * GROUP_W + RW_DECAY_LORA + RW_A_LORA + RW_GATE_LORA
W_DA = 2 * (N_HEADS * 2 * DA_DQK) + N_HEADS * HEAD_DIM
N_COLS = W_RET + W_HG + W_RW + W_DA
RW_SPLITS = [GROUP_W, 2 * GROUP_W, 3 * GROUP_W, 3 * GROUP_W + RW_DECAY_LORA, 3 * GROUP_W + RW_DECAY_LORA + RW_A_LORA]
CHUNK = 64
Q_BLOCK = 128
MASK_NEG = -1e30
N_EXPERTS = 64
TOP_K = 8
D_EXPERT = 256
ROUTED_SCALE = 2.5
MOE_BLOCK = 128
ALPHA = (2.0 * DEPTH) ** 0.25
LN_EPS = 1e-5
NORM_EPS = 1e-5

V7X_VMEM_LIMIT_BYTES = 56 * 1024 * 1024


def _f32(t):
    return t.astype(jnp.float32)


def _matmul_body(x_ref, w_ref, o_ref):
    o_ref[...] = jnp.dot(x_ref[...].astype(jnp.bfloat16), w_ref[...],
                         preferred_element_type=jnp.float32)


def _matmul(x, w_bf16, tm):
    M, K = x.shape
    N = w_bf16.shape[1]
    tm = min(tm, M)
    assert M % tm == 0
    return pl.pallas_call(
        _matmul_body,
        grid=(M // tm,),
        in_specs=[pl.BlockSpec((tm, K), lambda i: (i, 0)),
                  pl.BlockSpec((K, N), lambda i: (0, 0))],
        out_specs=pl.BlockSpec((tm, N), lambda i: (i, 0)),
        out_shape=jax.ShapeDtypeStruct((M, N), jnp.float32),
        compiler_params=pltpu.CompilerParams(
            dimension_semantics=("arbitrary",), vmem_limit_bytes=V7X_VMEM_LIMIT_BYTES),
        name="matmul",
    )(x, w_bf16)


def _in_proj_body(x_ref, w_ref, o_ref, k_ref, v_ref):
    acc = jnp.dot(x_ref[...].astype(jnp.bfloat16), w_ref[...], preferred_element_type=jnp.float32)
    o_ref[...] = acc
    k_ref[...] = acc[:, N_COLS - 2 * GROUP_W:N_COLS - GROUP_W]
    v_ref[...] = acc[:, N_COLS - GROUP_W:]


def _in_proj(x, w_bf16, tm):
    M, K = x.shape
    tm = min(tm, M)
    assert M % tm == 0
    row = lambda i: (i, 0)
    return pl.pallas_call(
        _in_proj_body,
        grid=(M // tm,),
        in_specs=[pl.BlockSpec((tm, K), row), pl.BlockSpec((K, N_COLS), lambda i: (0, 0))],
        out_specs=[pl.BlockSpec((tm, N_COLS), row), pl.BlockSpec((tm, GROUP_W), row),
                   pl.BlockSpec((tm, GROUP_W), row)],
        out_shape=[jax.ShapeDtypeStruct((M, N_COLS), jnp.float32),
                   jax.ShapeDtypeStruct((M, GROUP_W), jnp.float32),
                   jax.ShapeDtypeStruct((M, GROUP_W), jnp.float32)],
        compiler_params=pltpu.CompilerParams(
            dimension_semantics=("arbitrary",), vmem_limit_bytes=V7X_VMEM_LIMIT_BYTES),
        name="in_proj",
    )(x, w_bf16)


def _ln_rows(z, g, b):
    mu = jnp.mean(z, -1, keepdims=True)
    zc = z - mu
    var = jnp.mean(zc * zc, -1, keepdims=True)
    return zc * lax.rsqrt(var + LN_EPS) * g + b


def _out_proj_ln_body(a_ref, b_ref, c_ref, d_ref, w_ref, res_ref, g_ref, beta_ref, o_ref):
    mix = None
    for j, part in enumerate((a_ref, b_ref, c_ref, d_ref)):
        term = jnp.dot(part[...].astype(jnp.bfloat16), w_ref[j * GROUP_W:(j + 1) * GROUP_W, :],
                       preferred_element_type=jnp.float32)
        mix = term if mix is None else mix + term
    o_ref[...] = _ln_rows(ALPHA * res_ref[...] + mix, g_ref[...], beta_ref[...])


def _out_proj_ln(parts, w_bf16, res, g, b, tm):
    M, N = res.shape
    tm = min(tm, M)
    assert M % tm == 0
    row = lambda i: (i, 0)
    fixed = lambda i: (0, 0)
    part = pl.BlockSpec((tm, GROUP_W), row)
    return pl.pallas_call(
        _out_proj_ln_body,
        grid=(M // tm,),
        in_specs=[part, part, part, part, pl.BlockSpec((N, N), fixed), pl.BlockSpec((tm, N), row),
                  pl.BlockSpec((1, N), fixed), pl.BlockSpec((1, N), fixed)],
        out_specs=pl.BlockSpec((tm, N), row),
        out_shape=jax.ShapeDtypeStruct((M, N), jnp.float32),
        compiler_params=pltpu.CompilerParams(
            dimension_semantics=("arbitrary",), vmem_limit_bytes=V7X_VMEM_LIMIT_BYTES),
        name="out_proj_ln",
    )(*parts, w_bf16, res, g.reshape(1, N), b.reshape(1, N))


EXPERT_ROWS = 256


def _start_row_gather(idx_ref, n_rows, src_hbm, dst_at, sem):
    def body(r, carry):
        t = idx_ref[0, 0, r]
        pltpu.make_async_copy(src_hbm.at[pl.ds(t, 1)], dst_at(r), sem).start()
        return carry
    lax.fori_loop(0, n_rows, body, 0, unroll=8)


def _moe_experts_body(n_blocks, blk_ref, exp_ref, lo_ref, hi_ref, first_ref, tok_ref, tok_next_ref, x_hbm,
                      wg_ref, wu_ref, wd_ref, y_ref, xbuf, sem):
    w = pl.program_id(0)
    blk = blk_ref[w]
    slot = blk % 2
    lo = lo_ref[w]
    hi = hi_ref[w]

    def gather(idx_ref, s):
        _start_row_gather(idx_ref, EXPERT_ROWS, x_hbm, lambda r: xbuf.at[s, pl.ds(r, 1)], sem.at[s])

    @pl.when(w == 0)
    def _():
        gather(tok_ref, 0)

    @pl.when(first_ref[w] == 1)
    def _():
        @pl.when(blk + 1 < n_blocks)
        def _():
            gather(tok_next_ref, 1 - slot)
        pltpu.make_async_copy(x_hbm.at[pl.ds(0, EXPERT_ROWS)], xbuf.at[slot], sem.at[slot]).wait()
        y_ref[...] = jnp.zeros_like(y_ref)

    @pl.when(hi > lo)
    def _():
        x = xbuf[slot].astype(jnp.bfloat16)
        g = jnp.dot(x, wg_ref[0], preferred_element_type=jnp.float32)
        u = jnp.dot(x, wu_ref[0], preferred_element_type=jnp.float32)
        h = (g * jax.nn.sigmoid(g) * u).astype(jnp.bfloat16)
        y = jnp.dot(h, wd_ref[0], preferred_element_type=jnp.float32)
        r = lax.broadcasted_iota(jnp.int32, (EXPERT_ROWS, 1), 0)
        y_ref[...] += jnp.where((r >= lo) & (r < hi), y, 0.0)


def _moe_experts(x, tok_sorted, items, wg, wu, wd):
    M, D = x.shape
    A = tok_sorted.shape[0]
    assert A % EXPERT_ROWS == 0
    n_blocks = A // EXPERT_ROWS
    n_items = items[0].shape[0]
    tok3 = tok_sorted.reshape(n_blocks, 1, EXPERT_ROWS)
    smem_blk = lambda f: pl.BlockSpec((1, 1, EXPERT_ROWS), f, memory_space=pltpu.SMEM)
    w_spec = lambda shape: pl.BlockSpec(shape, lambda w, blk, e, *_: (e[w], 0, 0))
    grid_spec = pltpu.PrefetchScalarGridSpec(
        num_scalar_prefetch=5,
        grid=(n_items,),
        in_specs=[smem_blk(lambda w, blk, *_: (blk[w], 0, 0)),
                  smem_blk(lambda w, blk, *_: (jnp.minimum(blk[w] + 1, n_blocks - 1), 0, 0)),
                  pl.BlockSpec(memory_space=pl.ANY),
                  w_spec((1, D, D_EXPERT)), w_spec((1, D, D_EXPERT)), w_spec((1, D_EXPERT, D))],
        out_specs=pl.BlockSpec((EXPERT_ROWS, D), lambda w, blk, *_: (blk[w], 0)),
        scratch_shapes=[pltpu.VMEM((2, EXPERT_ROWS, D), jnp.float32), pltpu.SemaphoreType.DMA((2,))],
    )
    return pl.pallas_call(
        functools.partial(_moe_experts_body, n_blocks),
        grid_spec=grid_spec,
        out_shape=jax.ShapeDtypeStruct((A, D), jnp.float32),
        compiler_params=pltpu.CompilerParams(
            dimension_semantics=("arbitrary",), vmem_limit_bytes=V7X_VMEM_LIMIT_BYTES),
        name="moe_experts",
    )(*items, tok3, tok3, x, wg, wu, wd)


def _moe_work_items(e_sorted, A):
    n_blocks = A // EXPERT_ROWS
    group_end = jnp.searchsorted(e_sorted, jnp.arange(1, N_EXPERTS + 1, dtype=e_sorted.dtype)).astype(jnp.int32)
    starts = jnp.sort(jnp.concatenate([jnp.arange(n_blocks, dtype=jnp.int32) * EXPERT_ROWS, group_end]))
    ends = jnp.concatenate([starts[1:], jnp.full((1,), A, jnp.int32)])
    blk = jnp.minimum(starts // EXPERT_ROWS, n_blocks - 1)
    expert = jnp.minimum(jnp.searchsorted(group_end, starts, side='right'), N_EXPERTS - 1).astype(jnp.int32)
    new_start = jnp.concatenate([jnp.ones((1,), bool), starts[1:] != starts[:-1]])
    first = ((starts % EXPERT_ROWS == 0) & (starts < A) & new_start).astype(jnp.int32)
    base = blk * EXPERT_ROWS
    return blk, expert, starts - base, ends - base, first


COMBINE_ROWS = 128


def _moe_combine_body(pos_ref, pos_next_ref, wts_ref, x_ref, gu_ref, shd_ref, g_ref, b_ref, y_hbm,
                      o_ref, buf, sem):
    i = pl.program_id(0)
    n = pl.num_programs(0)
    slot = i % 2
    n_rows = COMBINE_ROWS * TOP_K

    def gather(idx_ref, s):
        def dst(r):
            k = jnp.bitwise_and(r, TOP_K - 1)
            token = lax.shift_right_logical(r, TOP_K.bit_length() - 1)
            return buf.at[s, pl.ds(k * COMBINE_ROWS + token, 1)]
        _start_row_gather(idx_ref, n_rows, y_hbm, dst, sem.at[s])

    @pl.when(i == 0)
    def _():
        gather(pos_ref, 0)

    @pl.when(i + 1 < n)
    def _():
        gather(pos_next_ref, 1 - slot)

    pltpu.make_async_copy(y_hbm.at[pl.ds(0, n_rows)], buf.at[slot], sem.at[slot]).wait()
    wts = wts_ref[...]
    routed = wts[:, 0:1] * buf[slot, pl.ds(0, COMBINE_ROWS)]
    for k in range(1, TOP_K):
        routed = routed + wts[:, k:k + 1] * buf[slot, pl.ds(k * COMBINE_ROWS, COMBINE_ROWS)]
    gu = gu_ref[...]
    g = gu[:, :D_EXPERT]
    u = gu[:, D_EXPERT:]
    h = (g * jax.nn.sigmoid(g) * u).astype(jnp.bfloat16)
    shared = jnp.dot(h, shd_ref[...], preferred_element_type=jnp.float32)
    o_ref[...] = _ln_rows(ALPHA * x_ref[...] + (routed + shared), g_ref[...], b_ref[...])


def _moe_combine(y_sorted, inv_pos, wts, x, gu, shd_bf16, g, b):
    M, D = x.shape
    tt = min(COMBINE_ROWS, M)
    assert tt == COMBINE_ROWS and M % tt == 0
    n_tiles = M // tt
    pos3 = inv_pos.reshape(n_tiles, 1, tt * TOP_K)
    row = lambda i: (i, 0)
    fixed = lambda i: (0, 0)
    smem_blk = lambda f: pl.BlockSpec((1, 1, tt * TOP_K), f, memory_space=pltpu.SMEM)
    return pl.pallas_call(
        _moe_combine_body,
        grid=(n_tiles,),
        in_specs=[smem_blk(lambda i: (i, 0, 0)),
                  smem_blk(lambda i: (jnp.minimum(i + 1, n_tiles - 1), 0, 0)),
                  pl.BlockSpec((tt, TOP_K), row),
                  pl.BlockSpec((tt, D), row),
                  pl.BlockSpec((tt, 2 * D_EXPERT), row),
                  pl.BlockSpec((D_EXPERT, D), fixed),
                  pl.BlockSpec((1, D), fixed), pl.BlockSpec((1, D), fixed),
                  pl.BlockSpec(memory_space=pl.ANY)],
        out_specs=pl.BlockSpec((tt, D), row),
        out_shape=jax.ShapeDtypeStruct((M, D), jnp.float32),
        scratch_shapes=[pltpu.VMEM((2, tt * TOP_K, D), jnp.float32), pltpu.SemaphoreType.DMA((2,))],
        compiler_params=pltpu.CompilerParams(
            dimension_semantics=("arbitrary",), vmem_limit_bytes=V7X_VMEM_LIMIT_BYTES),
        name="moe_combine_ln",
    )(pos3, pos3, wts, x, gu, shd_bf16, g.reshape(1, D), b.reshape(1, D), y_sorted)


def _moe_ln(x, l, W):
    M, D = x.shape
    rgu = _matmul(x, W['gu_router_bf16'][l], 256)
    scores = jax.nn.sigmoid(rgu[:, 2 * D_EXPERT:2 * D_EXPERT + N_EXPERTS])
    _, idx = lax.top_k(scores + _f32(W['router_bias'][l]), TOP_K)
    wts = jnp.take_along_axis(scores, idx, axis=-1)
    wts = wts / jnp.sum(wts, -1, keepdims=True) * ROUTED_SCALE

    A = M * TOP_K
    flat_e = idx.reshape(-1).astype(jnp.int32)
    assign = jnp.arange(A, dtype=jnp.int32)
    e_sorted, order = lax.sort_key_val(flat_e, assign)
    _, inv_pos = lax.sort_key_val(order, assign)
    tok_sorted = lax.shift_right_logical(order, TOP_K.bit_length() - 1)
    items = _moe_work_items(e_sorted, A)

    y_sorted = _moe_experts(x, tok_sorted, items, W['e_gate_bf16'][l], W['e_up_bf16'][l], W['e_down_bf16'][l])
    return _moe_combine(y_sorted, inv_pos, wts, x, rgu, W['sh_down_bf16'][l], W['ln2_g'][l], W['ln2_b'][l])


DA_COL_BLOCK = (W_RET + W_HG + W_RW) // GROUP_W
ATTN_BLOCK = 256
LOG2E = 1.4426950408889634
BF16 = jnp.bfloat16
F32 = jnp.float32


def _attn_prompt_body(lam_ref, q_ref, k_ref, v_ref, nw_ref, o_ref, kbf, vh, qs, m_scr, l_scr, acc_scr):
    i = pl.program_id(1)
    T = ATTN_BLOCK
    n_hm = 2 * N_HEADS

    @pl.when(i == 0)
    def _():
        kbf[...] = k_ref[...].astype(BF16)
        v = v_ref[...]
        for h in range(N_HEADS):
            vh[h] = v[:, h * HEAD_DIM:(h + 1) * HEAD_DIM].astype(BF16)

    q = q_ref[...]
    lane = lax.broadcasted_iota(jnp.int32, (T, GROUP_W), 1)
    for hm in range(n_hm):
        keep = (lane >= hm * DA_DQK) & (lane < (hm + 1) * DA_DQK)
        qs[pl.ds(hm * T, T), :] = jnp.where(keep, q, 0.0).astype(BF16)
    m_scr[...] = jnp.full_like(m_scr, MASK_NEG)
    l_scr[...] = jnp.zeros_like(l_scr)
    acc_scr[...] = jnp.zeros_like(acc_scr)
    score_scale = DA_DQK ** -0.5 * LOG2E

    def kv_block(j, masked):
        kb = kbf[pl.ds(j * T, T), :]
        for h in range(N_HEADS):
            rows = pl.ds(h * 2 * T, 2 * T)
            s = lax.dot_general(qs[rows, :], kb, (((1,), (1,)), ((), ())), preferred_element_type=F32)
            s = s * score_scale
            if masked:
                qpos = lax.broadcasted_iota(jnp.int32, (2 * T, T), 0) % T
                kpos = lax.broadcasted_iota(jnp.int32, (2 * T, T), 1)
                s = jnp.where(kpos <= qpos, s, MASK_NEG)
            m_old = m_scr[rows, :]
            m_new = jnp.maximum(m_old, jnp.max(s, -1, keepdims=True))
            alpha = jnp.exp2(m_old - m_new)
            p = jnp.exp2(s - m_new)
            l_scr[rows, :] = alpha * l_scr[rows, :] + jnp.sum(p, -1, keepdims=True)
            pv = jnp.dot(p.astype(BF16), vh[h, pl.ds(j * T, T), :], preferred_element_type=F32)
            acc_scr[rows, :] = alpha * acc_scr[rows, :] + pv
            m_scr[rows, :] = m_new

    def body(j, c):
        kv_block(j, False)
        return c
    lax.fori_loop(0, i, body, 0)
    kv_block(i, True)

    lam = lam_ref[0]
    outs = []
    for h in range(N_HEADS):
        r0 = pl.ds(h * 2 * T, T)
        r1 = pl.ds(h * 2 * T + T, T)
        o = acc_scr[r0, :] / l_scr[r0, :] - lam * (acc_scr[r1, :] / l_scr[r1, :])
        outs.append(o * lax.rsqrt(jnp.mean(o * o, -1, keepdims=True) + NORM_EPS))
    o_ref[...] = jnp.concatenate(outs, -1) * nw_ref[...]


def _attn_prompt(proj, lam, norm_scale, B, S):
    T = ATTN_BLOCK
    assert S % T == 0
    nq = S // T
    n_rows = 2 * N_HEADS * T
    return pl.pallas_call(
        _attn_prompt_body,
        grid=(B, nq),
        in_specs=[pl.BlockSpec(memory_space=pltpu.SMEM),
                  pl.BlockSpec((T, GROUP_W), lambda b, i: (b * nq + i, DA_COL_BLOCK)),
                  pl.BlockSpec((S, GROUP_W), lambda b, i: (b, DA_COL_BLOCK + 1)),
                  pl.BlockSpec((S, GROUP_W), lambda b, i: (b, DA_COL_BLOCK + 2)),
                  pl.BlockSpec((1, GROUP_W), lambda b, i: (0, 0))],
        out_specs=pl.BlockSpec((T, GROUP_W), lambda b, i: (b * nq + i, 0)),
        out_shape=jax.ShapeDtypeStruct((B * S, GROUP_W), F32),
        scratch_shapes=[pltpu.VMEM((S, GROUP_W), BF16),
                        pltpu.VMEM((N_HEADS, S, HEAD_DIM), BF16),
                        pltpu.VMEM((n_rows, GROUP_W), BF16),
                        pltpu.VMEM((n_rows, 1), F32),
                        pltpu.VMEM((n_rows, 1), F32),
                        pltpu.VMEM((n_rows, HEAD_DIM), F32)],
        compiler_params=pltpu.CompilerParams(
            dimension_semantics=("arbitrary", "arbitrary"), vmem_limit_bytes=V7X_VMEM_LIMIT_BYTES),
        name="diff_attn_prompt",
    )(lam.reshape(1), proj, proj, proj, norm_scale.reshape(1, GROUP_W))


SAMPLE_Q_ROWS = 8
PAGES_PER_STEP = 16


def _attn_sample_body(n_new, n_pages, pt_ref, lam_ref, q_ref, kn_ref, vn_ref, nw_ref, ck_hbm, cv_hbm, o_ref,
                      kbuf, vbuf, sem, qs, m_scr, l_scr, acc_scr):
    b = pl.program_id(0)
    c = pl.program_id(1)
    nc = pl.num_programs(1)
    step = b * nc + c
    n_steps = pl.num_programs(0) * nc
    slot = step % 2
    PG, TQ = PAGES_PER_STEP, SAMPLE_Q_ROWS
    n_hm = 2 * N_HEADS
    n_rows = n_hm * TQ

    def fetch(bb, cc, s):
        base = bb * n_pages + cc * PG
        for i in range(PG):
            pid = pt_ref[base + i]
            pltpu.make_async_copy(ck_hbm.at[pl.ds(pid, 1)], kbuf.at[s, pl.ds(i, 1)], sem.at[0, s]).start()
            pltpu.make_async_copy(cv_hbm.at[pl.ds(pid, 1)], vbuf.at[s, pl.ds(i, 1)], sem.at[1, s]).start()

    @pl.when(step == 0)
    def _():
        fetch(b, c, 0)

    @pl.when(step + 1 < n_steps)
    def _():
        wrap = c + 1 == nc
        fetch(jnp.where(wrap, b + 1, b), jnp.where(wrap, 0, c + 1), 1 - slot)

    @pl.when(c == 0)
    def _():
        q = q_ref[0] * (DA_DQK ** -0.5 * LOG2E)
        lane = lax.broadcasted_iota(jnp.int32, (TQ, GROUP_W), 1)
        for hm in range(n_hm):
            keep = (lane >= hm * DA_DQK) & (lane < (hm + 1) * DA_DQK)
            qs[pl.ds(hm * TQ, TQ), :] = jnp.where(keep, q, 0.0).astype(BF16)
        m_scr[...] = jnp.full_like(m_scr, MASK_NEG)
        l_scr[...] = jnp.zeros_like(l_scr)
        acc_scr[...] = jnp.zeros_like(acc_scr)

    def update(s, v_bf):
        m_old = m_scr[...]
        m_new = jnp.maximum(m_old, jnp.max(s, -1, keepdims=True))
        alpha = jnp.exp2(m_old - m_new)
        p = jnp.exp2(s - m_new)
        l_scr[...] = alpha * l_scr[...] + jnp.sum(p, -1, keepdims=True)
        acc_scr[...] = alpha * acc_scr[...] + _dot(p.astype(BF16), v_bf)
        m_scr[...] = m_new

    pltpu.make_async_copy(ck_hbm.at[pl.ds(0, PG)], kbuf.at[slot], sem.at[0, slot]).wait()
    pltpu.make_async_copy(cv_hbm.at[pl.ds(0, PG)], vbuf.at[slot], sem.at[1, slot]).wait()
    k_bf = kbuf[slot].reshape(PG * PAGE_SIZE, GROUP_W).astype(BF16)
    v_bf = vbuf[slot].reshape(PG * PAGE_SIZE, GROUP_W).astype(BF16)
    update(_dot_nt(qs[...], k_bf), v_bf)

    @pl.when(c == nc - 1)
    def _():
        s_new = _dot_nt(qs[...], kn_ref[0].astype(BF16))
        t = lax.broadcasted_iota(jnp.int32, (n_rows, TQ), 0) % TQ
        j = lax.broadcasted_iota(jnp.int32, (n_rows, TQ), 1)
        update(jnp.where((j <= t) & (j < n_new), s_new, MASK_NEG), vn_ref[0].astype(BF16))
        lam = lam_ref[0]
        ratio = acc_scr[...] / l_scr[...]
        lane_head = lax.broadcasted_iota(jnp.int32, (TQ, GROUP_W), 1) // HEAD_DIM
        o = jnp.zeros((TQ, GROUP_W), F32)
        for h in range(N_HEADS):
            o_h = ratio[(2 * h) * TQ:(2 * h + 1) * TQ] - lam * ratio[(2 * h + 1) * TQ:(2 * h + 2) * TQ]
            o = o + jnp.where(lane_head == h, o_h, 0.0)
        ms = _dot_exact_rhs(o * o, _head_seg_ones()) * (1.0 / HEAD_DIM)
        o_ref[0] = o * lax.rsqrt(ms + NORM_EPS) * nw_ref[...]


def _attn_sample(q8, k8, v8, cache_k3, cache_v3, page_table, lam, norm_scale, n_new):
    B, n_pages = page_table.shape
    PG, TQ = PAGES_PER_STEP, SAMPLE_Q_ROWS
    assert n_pages % PG == 0 and n_new <= TQ
    n_rows = 2 * N_HEADS * TQ
    seq = lambda: pl.BlockSpec((1, TQ, GROUP_W), lambda b, c, pt: (b, 0, 0))
    grid_spec = pltpu.PrefetchScalarGridSpec(
        num_scalar_prefetch=1,
        grid=(B, n_pages // PG),
        in_specs=[pl.BlockSpec(memory_space=pltpu.SMEM), seq(), seq(), seq(),
                  pl.BlockSpec((1, GROUP_W), lambda b, c, pt: (0, 0)),
                  pl.BlockSpec(memory_space=pl.ANY), pl.BlockSpec(memory_space=pl.ANY)],
        out_specs=seq(),
        scratch_shapes=[pltpu.VMEM((2, PG, PAGE_SIZE, GROUP_W), F32), pltpu.VMEM((2, PG, PAGE_SIZE, GROUP_W), F32),
                        pltpu.SemaphoreType.DMA((2, 2)), pltpu.VMEM((n_rows, GROUP_W), BF16),
                        pltpu.VMEM((n_rows, 1), F32), pltpu.VMEM((n_rows, 1), F32),
                        pltpu.VMEM((n_rows, GROUP_W), F32)],
    )
    return pl.pallas_call(
        functools.partial(_attn_sample_body, n_new, n_pages),
        grid_spec=grid_spec,
        out_shape=jax.ShapeDtypeStruct((B, TQ, GROUP_W), F32),
        compiler_params=pltpu.CompilerParams(
            dimension_semantics=("arbitrary", "arbitrary"), vmem_limit_bytes=V7X_VMEM_LIMIT_BYTES),
        name="diff_attn_sample",
    )(page_table.reshape(-1), lam.reshape(1), q8, k8, v8, norm_scale.reshape(1, GROUP_W), cache_k3, cache_v3)


RW_COL_BLOCK = (W_RET + W_HG) // GROUP_W
RW_CHUNK = 64


def _dot(a, b):
    return jnp.dot(a, b, preferred_element_type=F32)


def _dot_nt(a, b):
    return lax.dot_general(a, b, (((1,), (1,)), ((), ())), preferred_element_type=F32)


def _dot_tn(a, b):
    return lax.dot_general(a, b, (((0,), (0,)), ((), ())), preferred_element_type=F32)


def _split3(x):
    hi = x.astype(BF16)
    r1 = x - hi.astype(F32)
    mid = r1.astype(BF16)
    lo = (r1 - mid.astype(F32)).astype(BF16)
    return hi, mid, lo


def _dot_exact_rhs(x, m_bf16):
    hi, mid, lo = _split3(x)
    return _dot(hi, m_bf16) + _dot(mid, m_bf16) + _dot(lo, m_bf16)


def _dot_exact_lhs(m_bf16, x):
    hi, mid, lo = _split3(x)
    return _dot(m_bf16, hi) + _dot(m_bf16, mid) + _dot(m_bf16, lo)


def _hi_lo(x):
    hi = x.astype(BF16)
    return hi, (x - hi.astype(F32)).astype(BF16)


def _mm3(a_hl, b_hl, dot):
    (ah, al), (bh, bl) = a_hl, b_hl
    return dot(ah, bh) + dot(ah, bl) + dot(al, bh)


def _rwkv_body(n_valid, pr_ref, pk_ref, pv_ref, pl_ref, shift0_ref, s0_ref, mu_ref, w0_ref, wup_ref, a0_ref,
               aup_ref, gup_ref, kk_ref, ka_ref, rk_ref, lnw_ref, lnb_ref, o_ref, s_out_ref, sbd, carry):
    c_idx = pl.program_id(1)
    T = RW_CHUNK
    HT = N_HEADS * T

    @pl.when(c_idx == 0)
    def _():
        sbd[...] = s0_ref[0]
        carry[...] = shift0_ref[0]

    row = lax.broadcasted_iota(jnp.int32, (T, GROUP_W), 0)
    lane_head = lax.broadcasted_iota(jnp.int32, (T, GROUP_W), 1) // HEAD_DIM

    def shifted(p_ref, blk):
        cols = slice(blk * GROUP_W, (blk + 1) * GROUP_W)
        p = p_ref[...]
        prev = jnp.where(row == 0, carry[:, cols], pltpu.roll(p, 1, axis=0))
        carry[:, cols] = p[T - 1:T, :]
        return p + (prev - p) * mu_ref[:, cols]

    xr = shifted(pr_ref, 0)
    xk = shifted(pk_ref, 1)
    xv = shifted(pv_ref, 2)
    xl = shifted(pl_ref, 3)

    gi = lax.broadcasted_iota(jnp.int32, (GROUP_W, GROUP_W), 0) // HEAD_DIM
    gj = lax.broadcasted_iota(jnp.int32, (GROUP_W, GROUP_W), 1) // HEAD_DIM
    seg = (gi == gj).astype(BF16)

    z = -(w0_ref[...] + _dot(jnp.tanh(xl).astype(BF16), wup_ref[...]))
    softplus = jnp.maximum(z, 0.0) + jnp.log(1.0 + jnp.exp(-jnp.abs(z)))
    lw = -jnp.exp(-softplus - 0.5)
    a = jax.nn.sigmoid(a0_ref[...] + _dot(xl.astype(BF16), aup_ref[...]))
    g = _dot(jax.nn.sigmoid(xl).astype(BF16), gup_ref[...])
    kk = xk * kk_ref[...]
    kkn = kk / jnp.maximum(jnp.sqrt(_dot_exact_rhs(kk * kk, seg)), 1e-12)
    k_mod = xk * (1.0 + (a - 1.0) * ka_ref[...])
    an = -kkn
    bb = kkn * a
    if n_valid < T:
        valid = row < n_valid
        lw = jnp.where(valid, lw, 0.0)
        an = jnp.where(valid, an, 0.0)
        k_mod = jnp.where(valid, k_mod, 0.0)

    ti = lax.broadcasted_iota(jnp.int32, (T, T), 0)
    tj = lax.broadcasted_iota(jnp.int32, (T, T), 1)
    c = _dot_exact_lhs((tj <= ti).astype(BF16), lw)
    c_last = c[T - 1:T, :]
    inv_dec = jnp.exp(-c)
    to_end = jnp.exp(c_last - c)
    a_t = an * jnp.exp(c - lw)
    b_t = bb * inv_dec
    k_t = k_mod * inv_dec
    r_t = xr * jnp.exp(c)

    def stack(x):
        return jnp.concatenate([jnp.where(lane_head == h, x, 0.0) for h in range(N_HEADS)], axis=0)

    def tile(x):
        return jnp.concatenate([x] * N_HEADS, axis=0)

    ri = lax.broadcasted_iota(jnp.int32, (HT, HT), 0)
    ci = lax.broadcasted_iota(jnp.int32, (HT, HT), 1)
    same_head = (ri // T) == (ci // T)
    m_strict = same_head & ((ci % T) < (ri % T))
    m_incl = same_head & ((ci % T) <= (ri % T))

    a_p = _hi_lo(stack(a_t))
    r_p = _hi_lo(stack(r_t))
    b_p = _hi_lo(tile(b_t))
    k_p = _hi_lo(tile(k_t))
    a_ab = jnp.where(m_strict, _mm3(a_p, b_p, _dot_nt), 0.0)
    a_ak = jnp.where(m_strict, _mm3(a_p, k_p, _dot_nt), 0.0)
    a_rb = jnp.where(m_incl, _mm3(r_p, b_p, _dot_nt), 0.0)
    a_rk = jnp.where(m_incl, _mm3(r_p, k_p, _dot_nt), 0.0)

    s_old = sbd[...]
    s_p = _hi_lo(s_old)
    v_p = _hi_lo(stack(xv))

    u = _mm3(a_p, s_p, _dot_nt) + _mm3(_hi_lo(a_ak), v_p, _dot)
    pw = a_ab
    n_steps = max(1, (T - 1).bit_length())
    for step in range(n_steps):
        pw_p = _hi_lo(pw)
        u = u + _mm3(pw_p, _hi_lo(u), _dot)
        if step + 1 < n_steps:
            pw = _mm3(pw_p, pw_p, _dot)

    u_p = _hi_lo(u)
    y_s = _mm3(r_p, s_p, _dot_nt) + _mm3(_hi_lo(a_rb), u_p, _dot) + _mm3(_hi_lo(a_rk), v_p, _dot)
    y = y_s[0:T]
    for h in range(1, N_HEADS):
        y = y + y_s[h * T:(h + 1) * T]

    s_new = (s_old * jnp.exp(c_last)
             + _mm3(u_p, _hi_lo(stack(bb * to_end)), _dot_tn) + _mm3(v_p, _hi_lo(stack(k_mod * to_end)), _dot_tn))
    sbd[...] = s_new

    @pl.when(c_idx == pl.num_programs(1) - 1)
    def _():
        s_out_ref[0] = s_new

    inv_n = 1.0 / HEAD_DIM
    yc = y - _dot_exact_rhs(y, seg) * inv_n
    var = _dot_exact_rhs(yc * yc, seg) * inv_n
    yn = yc * lax.rsqrt(var + RW_LNX_EPS) * lnw_ref[...] + lnb_ref[...]
    bonus = _dot_exact_rhs(xr * k_mod * rk_ref[...], seg) * xv
    o_ref[...] = (yn + bonus) * g


def _rwkv_mix(p, col_block0, shift0, s0_bd, n_valid, B, L, wts):
    T = RW_CHUNK
    assert L % T == 0
    nc = L // T
    blk = lambda j: pl.BlockSpec((T, GROUP_W), lambda b, c: (b * nc + c, col_block0 + j))
    fixed = lambda shape: pl.BlockSpec(shape, lambda b, c: (0,) * len(shape))
    vec = fixed((1, GROUP_W))
    mat = fixed((GROUP_W, GROUP_W))
    return pl.pallas_call(
        functools.partial(_rwkv_body, n_valid),
        grid=(B, nc),
        in_specs=[blk(0), blk(1), blk(2), blk(3),
                  pl.BlockSpec((1, 1, W_RW), lambda b, c: (b, 0, 0)),
                  pl.BlockSpec((1, GROUP_W, GROUP_W), lambda b, c: (b, 0, 0)),
                  fixed((1, W_RW)), vec, mat, vec, mat, mat, vec, vec, vec, vec, vec],
        out_specs=[pl.BlockSpec((T, GROUP_W), lambda b, c: (b * nc + c, 0)),
                   pl.BlockSpec((1, GROUP_W, GROUP_W), lambda b, c: (b, 0, 0))],
        out_shape=[jax.ShapeDtypeStruct((B * L, GROUP_W), F32),
                   jax.ShapeDtypeStruct((B, GROUP_W, GROUP_W), F32)],
        scratch_shapes=[pltpu.VMEM((GROUP_W, GROUP_W), F32), pltpu.VMEM((1, W_RW), F32)],
        compiler_params=pltpu.CompilerParams(
            dimension_semantics=("arbitrary", "arbitrary"), vmem_limit_bytes=V7X_VMEM_LIMIT_BYTES),
        name="rwkv7_mix",
    )(p, p, p, p, shift0, s0_bd, *wts)


def _rwkv_weights(W, l):
    z = lambda n: jnp.zeros((n, GROUP_W), F32)
    wup = jnp.concatenate([_f32(W['rw_w_up'][l]), z(GROUP_W - RW_DECAY_LORA)], 0).astype(BF16)
    aup = jnp.concatenate([z(RW_DECAY_LORA), _f32(W['rw_a_up'][l]), z(RW_GATE_LORA)], 0).astype(BF16)
    gup = jnp.concatenate([z(RW_DECAY_LORA + RW_A_LORA), _f32(W['rw_g_up'][l])], 0).astype(BF16)
    r1 = lambda t: _f32(t).reshape(1, -1)
    return (r1(W['rw_mu'][l]), r1(W['rw_w0'][l]), wup, r1(W['rw_a0'][l]), aup, gup, r1(W['rw_k_k'][l]),
            r1(W['rw_k_a'][l]), r1(W['rw_r_k'][l]), r1(W['rw_lnx_w'][l]), r1(W['rw_lnx_b'][l]))


def _state_to_bd(s):
    eye = jnp.eye(N_HEADS, dtype=s.dtype)
    return (s[:, :, :, None, :] * eye[None, :, None, :, None]).reshape(s.shape[0], GROUP_W, GROUP_W)


def _bd_to_state(sbd):
    s5 = sbd.reshape(sbd.shape[0], N_HEADS, HEAD_DIM, N_HEADS, HEAD_DIM)
    return jnp.stack([s5[:, h, :, h, :] for h in range(N_HEADS)], axis=1)


def _head_seg_ones():
    si = lax.broadcasted_iota(jnp.int32, (GROUP_W, GROUP_W), 0) // HEAD_DIM
    sj = lax.broadcasted_iota(jnp.int32, (GROUP_W, GROUP_W), 1) // HEAD_DIM
    return (si == sj).astype(BF16)


HG_QW = N_HEADS * HG_DK
HG_BLOCK = 128
HG_SUB = 16


def _hgrn_body(has_lb, n_valid, hq_ref, hf_ref, hi_ref, hg_ref, s0_ref, lb_ref, nw_ref, o_ref, s_out_ref,
               st, qt_s, kh_s, v_s, dec_s, oi_s):
    c_idx = pl.program_id(1)
    TB, C = HG_BLOCK, HG_SUB

    @pl.when(c_idx == 0)
    def _():
        st[...] = s0_ref[0]

    hf = hf_ref[...]
    log_sig = jnp.minimum(hf, 0.0) - jnp.log(1.0 + jnp.exp(-jnp.abs(hf)))
    if has_lb:
        log_lb = lb_ref[0:1, :]
        t2 = lb_ref[1:2, :] + log_sig
        lf = jnp.maximum(log_lb, t2) + jnp.log(1.0 + jnp.exp(-jnp.abs(log_lb - t2)))
        k_in = lb_ref[2:3, :] * jax.nn.sigmoid(-hf)
    else:
        lf = log_sig
        k_in = jax.nn.sigmoid(-hf)
    hq = hq_ref[...]
    q = hq * jax.nn.sigmoid(hq) * (HG_DK ** -0.5)
    v = hi_ref[...]
    row = lax.broadcasted_iota(jnp.int32, (TB, HG_QW), 0)
    if n_valid < TB:
        lf = jnp.where(row < n_valid, lf, 0.0)
        k_in = jnp.where(row < n_valid, k_in, 0.0)

    ti = lax.broadcasted_iota(jnp.int32, (TB, TB), 0)
    tj = lax.broadcasted_iota(jnp.int32, (TB, TB), 1)
    same = (ti // C) == (tj // C)
    b = _dot_exact_lhs((same & (tj <= ti)).astype(BF16), lf)
    b_last = _dot_exact_lhs(same.astype(BF16), lf)

    gi = lax.broadcasted_iota(jnp.int32, (HG_QW, GROUP_W), 0) // HG_DK
    gj = lax.broadcasted_iota(jnp.int32, (HG_QW, GROUP_W), 1) // HEAD_DIM
    head_sum = (gi == gj).astype(BF16)

    off = row % C
    off_v = lax.broadcasted_iota(jnp.int32, (TB, GROUP_W), 0) % C
    o_intra = _dot((q * k_in).astype(BF16), head_sum) * v
    for d in range(1, C):
        ok = off >= d
        e = jnp.exp(jnp.where(ok, b - pltpu.roll(b, d, axis=0), 0.0))
        p = jnp.where(ok, q * e * pltpu.roll(k_in, d, axis=0), 0.0)
        v_d = jnp.where(off_v >= d, pltpu.roll(v, d, axis=0), 0.0)
        o_intra = o_intra + _dot(p.astype(BF16), head_sum) * v_d

    qt_s[...] = (q * jnp.exp(b)).astype(BF16)
    kh_s[...] = (k_in * jnp.exp(b_last - b)).astype(BF16)
    v_s[...] = v.astype(BF16)
    dec_s[...] = jnp.exp(b_last)

    bi = lax.broadcasted_iota(jnp.int32, (GROUP_W, HG_QW), 0) // HEAD_DIM
    bj = lax.broadcasted_iota(jnp.int32, (GROUP_W, HG_QW), 1) // HG_DK
    block_diag = bi == bj

    def group(s, carry):
        start = pl.multiple_of(s * C, C)
        rows = pl.ds(start, C)
        s_cur = st[...]
        oi_s[rows, :] = _dot_nt(qt_s[rows, :], s_cur.astype(BF16))
        upd = _dot_tn(v_s[rows, :], kh_s[rows, :])
        st[...] = s_cur * dec_s[pl.ds(start, 1), :] + jnp.where(block_diag, upd, 0.0)
        return carry
    lax.fori_loop(0, TB // C, group, 0)

    @pl.when(c_idx == pl.num_programs(1) - 1)
    def _():
        s_out_ref[0] = st[...]

    o = o_intra + oi_s[...]
    ms = _dot_exact_rhs(o * o, _head_seg_ones()) * (1.0 / HEAD_DIM)
    hg = hg_ref[...]
    o_ref[...] = o * lax.rsqrt(ms + NORM_EPS) * nw_ref[...] * (hg * jax.nn.sigmoid(hg))


def _hgrn_mix(p, col0, s0_bd, lb_rows, norm_w4, has_lb, n_valid, B, L):
    TB = HG_BLOCK
    assert L % TB == 0 and col0 % HG_QW == 0
    nc = L // TB
    qb, gb = col0 // HG_QW, (col0 + 2 * HG_QW) // GROUP_W
    fixed = lambda shape: pl.BlockSpec(shape, lambda b, c: (0,) * len(shape))
    rows = lambda width, j: pl.BlockSpec((TB, width), lambda b, c: (b * nc + c, j))
    return pl.pallas_call(
        functools.partial(_hgrn_body, has_lb, n_valid),
        grid=(B, nc),
        in_specs=[rows(HG_QW, qb), rows(HG_QW, qb + 1), rows(GROUP_W, gb), rows(GROUP_W, gb + 1),
                  pl.BlockSpec((1, GROUP_W, HG_QW), lambda b, c: (b, 0, 0)),
                  fixed((8, HG_QW)), fixed((1, GROUP_W))],
        out_specs=[rows(GROUP_W, 0), pl.BlockSpec((1, GROUP_W, HG_QW), lambda b, c: (b, 0, 0))],
        out_shape=[jax.ShapeDtypeStruct((B * L, GROUP_W), F32),
                   jax.ShapeDtypeStruct((B, GROUP_W, HG_QW), F32)],
        scratch_shapes=[pltpu.VMEM((GROUP_W, HG_QW), F32), pltpu.VMEM((TB, HG_QW), BF16),
                        pltpu.VMEM((TB, HG_QW), BF16), pltpu.VMEM((TB, GROUP_W), BF16),
                        pltpu.VMEM((TB, HG_QW), F32), pltpu.VMEM((TB, GROUP_W), F32)],
        compiler_params=pltpu.CompilerParams(
            dimension_semantics=("arbitrary", "arbitrary"), vmem_limit_bytes=V7X_VMEM_LIMIT_BYTES),
        name="hgrn2_mix",
    )(p, p, p, p, s0_bd, lb_rows, norm_w4.reshape(1, GROUP_W))


def _hg_state_to_bd(s):
    eye = jnp.eye(N_HEADS, dtype=s.dtype)
    st = jnp.swapaxes(s, 2, 3)
    return (st[:, :, :, None, :] * eye[None, :, None, :, None]).reshape(s.shape[0], GROUP_W, HG_QW)


def _hg_bd_to_state(sbd):
    s5 = sbd.reshape(sbd.shape[0], N_HEADS, HEAD_DIM, N_HEADS, HG_DK)
    return jnp.swapaxes(jnp.stack([s5[:, h, :, h, :] for h in range(N_HEADS)], axis=1), 2, 3)


def _hg_lb_rows(hg_lb, l):
    lb_soft = jax.nn.softmax(_f32(hg_lb), axis=0)
    lb = (jnp.cumsum(lb_soft, axis=0) - lb_soft[0])[l]
    z = jnp.zeros_like(lb)
    if l == 0:
        return jnp.stack([z] * 8)
    return jnp.stack([jnp.log(lb), jnp.log1p(-lb), 1.0 - lb, z, z, z, z, z])


RET_CHUNK = 64


def _ret_body(q_ref, k_ref, v_ref, g_ref, cos_ref, sin_ref, dmask_ref, qdec_ref, kdec_ref, sdec_ref, s0_ref,
              o_ref, s_out_ref, st):
    c_idx = pl.program_id(1)
    T = RET_CHUNK

    @pl.when(c_idx == 0)
    def _():
        st[...] = s0_ref[0]

    lane = lax.broadcasted_iota(jnp.int32, (T, GROUP_W), 1)
    upper_half = (lane % HEAD_DIM) >= (HEAD_DIM // 2)
    lane_head = lane // HEAD_DIM
    cos = cos_ref[...]
    sin = sin_ref[...]

    def rotary(x):
        swapped = jnp.where(upper_half, pltpu.roll(x, HEAD_DIM // 2, axis=1),
                            pltpu.roll(x, GROUP_W - HEAD_DIM // 2, axis=1))
        return x * cos + swapped * sin

    def stack(x):
        return jnp.concatenate([jnp.where(lane_head == h, x, 0.0) for h in range(N_HEADS)], axis=0)

    q = rotary(q_ref[...])
    k = rotary(k_ref[...]) * (HEAD_DIM ** -0.5)
    q_s = stack(q).astype(BF16)
    k_tl = jnp.concatenate([k] * N_HEADS, axis=0).astype(BF16)
    v_m = stack(v_ref[...]).astype(BF16)
    scores = _dot_nt(q_s, k_tl) * dmask_ref[...]
    s_old = st[...]
    o_s = _dot(scores.astype(BF16), v_m) + _dot(q_s, s_old.astype(BF16)) * qdec_ref[...]
    o = o_s[0:T]
    for h in range(1, N_HEADS):
        o = o + o_s[h * T:(h + 1) * T]
    k_w = (stack(k) * kdec_ref[...]).astype(BF16)
    s_new = s_old * sdec_ref[...] + _dot_tn(k_w, v_m)
    st[...] = s_new

    @pl.when(c_idx == pl.num_programs(1) - 1)
    def _():
        s_out_ref[0] = s_new

    ms = _dot_exact_rhs(o * o, _head_seg_ones()) * (1.0 / HEAD_DIM)
    g = g_ref[...]
    o_ref[...] = o * lax.rsqrt(ms + NORM_EPS) * (g * jax.nn.sigmoid(g))


def _ret_tables(pos, n_valid):
    T = RET_CHUNK
    half = HEAD_DIM // 2
    freq = 1.0 / (ROPE_BASE ** jnp.linspace(0.0, 1.0, half, dtype=F32))
    ang = _f32(pos)[:, None] * freq[None, :]
    cos = jnp.tile(jnp.cos(ang), (1, 2 * N_HEADS))
    sin = jnp.tile(jnp.concatenate([-jnp.sin(ang), jnp.sin(ang)], -1), (1, N_HEADS))
    log_gamma = jnp.log1p(-jnp.exp2(-5.0 - jnp.arange(N_HEADS, dtype=F32)))
    t = jnp.arange(T, dtype=F32)
    gap = t[:, None] - t[None, :]
    dm = jnp.where(gap >= 0, jnp.exp(jnp.maximum(gap, 0.0)[None] * log_gamma[:, None, None]), 0.0)
    eye = jnp.eye(N_HEADS, dtype=F32)
    dmask = (dm[:, :, None, :] * eye[:, None, :, None]).reshape(N_HEADS * T, N_HEADS * T)
    lanes = lambda col: jnp.broadcast_to(col.reshape(-1, 1), (col.size, GROUP_W))
    qdec = jnp.exp((t + 1.0)[None, :] * log_gamma[:, None])
    kdec = jnp.where(t[None, :] < n_valid, jnp.exp((n_valid - 1.0 - t)[None, :] * log_gamma[:, None]), 0.0)
    sdec = jnp.repeat(jnp.exp(n_valid * log_gamma), HEAD_DIM)
    return cos, sin, dmask, lanes(qdec), lanes(kdec), lanes(sdec)


def _ret_mix(p, col0, s0_bd, tables, B, L):
    T = RET_CHUNK
    assert L % T == 0
    nc = L // T
    HT = N_HEADS * T
    blk = lambda j: pl.BlockSpec((T, GROUP_W), lambda b, c: (b * nc + c, col0 + j))
    tab = pl.BlockSpec((T, GROUP_W), lambda b, c: (c, 0))
    fixed = lambda shape: pl.BlockSpec(shape, lambda b, c: (0,) * len(shape))
    state = lambda: pl.BlockSpec((1, GROUP_W, GROUP_W), lambda b, c: (b, 0, 0))
    return pl.pallas_call(
        _ret_body,
        grid=(B, nc),
        in_specs=[blk(0), blk(1), blk(2), blk(3), tab, tab, fixed((HT, HT)), fixed((HT, GROUP_W)),
                  fixed((HT, GROUP_W)), fixed((GROUP_W, GROUP_W)), state()],
        out_specs=[pl.BlockSpec((T, GROUP_W), lambda b, c: (b * nc + c, 0)), state()],
        out_shape=[jax.ShapeDtypeStruct((B * L, GROUP_W), F32), jax.ShapeDtypeStruct((B, GROUP_W, GROUP_W), F32)],
        scratch_shapes=[pltpu.VMEM((GROUP_W, GROUP_W), F32)],
        compiler_params=pltpu.CompilerParams(
            dimension_semantics=("arbitrary", "arbitrary"), vmem_limit_bytes=V7X_VMEM_LIMIT_BYTES),
        name="retention_mix",
    )(p, p, p, p, *tables, s0_bd)


def _layernorm(x, g, b):
    xf = _f32(x)
    mu = jnp.mean(xf, -1, keepdims=True)
    xc = xf - mu
    var = jnp.mean(xc * xc, -1, keepdims=True)
    return (xc * lax.rsqrt(var + LN_EPS) * _f32(g) + _f32(b)).astype(x.dtype)


def _rms(x, eps=NORM_EPS):
    return x * lax.rsqrt(jnp.mean(x * x, -1, keepdims=True) + eps)


def _head_ln(x, eps):
    xc = x - jnp.mean(x, -1, keepdims=True)
    return xc * lax.rsqrt(jnp.mean(xc * xc, -1, keepdims=True) + eps)


def _rotary(x, pos):
    half = x.shape[-1] // 2
    freq = 1.0 / (ROPE_BASE ** jnp.linspace(0.0, 1.0, half, dtype=jnp.float32))
    ang = _f32(pos)[:, None] * freq[None, :]
    cos = jnp.cos(ang)[None, :, None, :]
    sin = jnp.sin(ang)[None, :, None, :]
    x1, x2 = x[..., :half], x[..., half:]
    return jnp.concatenate([x1 * cos - x2 * sin, x1 * sin + x2 * cos], -1)


def _chunk_scan(step, xs, s0):
    B, L = xs[0].shape[:2]
    c = CHUNK if L % CHUNK == 0 else L
    n = L // c
    chunked = tuple(jnp.moveaxis(a.reshape(B, n, c, *a.shape[2:]), 1, 0) for a in xs)
    s, out = lax.scan(step, s0, chunked)
    return jnp.moveaxis(out, 0, 1).reshape(B, L, *out.shape[3:]), s


def _retention_step(S, xs, log_gamma):
    q, k, v = xs
    C = q.shape[1]
    t = jnp.arange(C, dtype=jnp.float32)
    gap = t[:, None] - t[None, :]
    dmask = jnp.where(gap >= 0, jnp.exp(jnp.maximum(gap, 0.0)[None] * log_gamma[:, None, None]), 0.0)
    scores = jnp.einsum('bqhd,bkhd->bhqk', q, k) * dmask[None]
    o = jnp.einsum('bhqk,bkhe->bqhe', scores, v)
    o = o + jnp.einsum('bqhd,bhde->bqhe', q, S) * jnp.exp((t + 1.0)[:, None] * log_gamma[None, :])[None, :, :, None]
    k_w = k * jnp.exp((C - 1.0 - t)[:, None] * log_gamma[None, :])[None, :, :, None]
    S = jnp.exp(C * log_gamma)[None, :, None, None] * S + jnp.einsum('bkhd,bkhe->bhde', k_w, v)
    return S, o


def _hgrn2_step(S, xs):
    q, k, logf, v = xs
    C = q.shape[1]
    b = jnp.cumsum(logf, axis=1)
    causal = jnp.tril(jnp.ones((C, C), bool))[None, :, :, None, None]
    diff = jnp.where(causal, b[:, :, None] - b[:, None, :], 0.0)
    decay = jnp.where(causal, jnp.exp(diff), 0.0)
    attn = jnp.einsum('bqhd,bqkhd,bkhd->bhqk', q, decay, k)
    o = jnp.einsum('bhqk,bkhe->bqhe', attn, v) + jnp.einsum('bqhd,bhde->bqhe', q * jnp.exp(b), S)
    b_last = b[:, -1]
    S = jnp.exp(b_last)[..., None] * S + jnp.einsum('bkhd,bkhe->bhde', k * jnp.exp(b_last[:, None] - b), v)
    return S, o


def _rwkv_scan(S0, r, w, k, v, a, b):
    def step(S, xs):
        r_t, w_t, k_t, v_t, a_t, b_t = xs
        S = (S * w_t[:, :, None, :]
             + jnp.einsum('bhvk,bhk->bhv', S, a_t)[..., None] * b_t[:, :, None, :]
             + v_t[..., None] * k_t[:, :, None, :])
        return S, jnp.einsum('bhvk,bhk->bhv', S, r_t)
    xs = tuple(jnp.moveaxis(t, 1, 0) for t in (r, w, k, v, a, b))
    S, y = lax.scan(step, S0, xs)
    return jnp.moveaxis(y, 0, 1), S


def _rwkv7(p_rw, prev_row, S0, l, W):
    B, L, _ = p_rw.shape
    p = _f32(p_rw)
    prev = jnp.concatenate([_f32(prev_row)[:, None], p[:, :-1]], axis=1)
    xs = p + (prev - p) * _f32(W['rw_mu'][l])
    r, k, v, wd, ad, gd = jnp.split(xs, RW_SPLITS, axis=-1)
    w_log = -jax.nn.softplus(-(_f32(W['rw_w0'][l]) + jnp.tanh(wd) @ _f32(W['rw_w_up'][l]))) - 0.5
    decay = jnp.exp(-jnp.exp(w_log))
    a = jax.nn.sigmoid(_f32(W['rw_a0'][l]) + ad @ _f32(W['rw_a_up'][l]))
    g = jax.nn.sigmoid(gd) @ _f32(W['rw_g_up'][l])
    hd = lambda t: t.reshape(B, L, N_HEADS, HEAD_DIM)
    kk = hd(k * _f32(W['rw_k_k'][l]))
    kk = kk / jnp.maximum(jnp.sqrt(jnp.sum(kk * kk, -1, keepdims=True)), 1e-12)
    k = hd(k * (1.0 + (a - 1.0) * _f32(W['rw_k_a'][l])))
    r, v, a_h = hd(r), hd(v), hd(a)
    y, S = _rwkv_scan(_f32(S0), r, hd(decay), k, v, -kk, kk * a_h)
    y = _head_ln(y, RW_LNX_EPS).reshape(B, L, GROUP_W) * _f32(W['rw_lnx_w'][l]) + _f32(W['rw_lnx_b'][l])
    bonus = (jnp.sum(r * k * _f32(W['rw_r_k'][l]), -1, keepdims=True) * v).reshape(B, L, GROUP_W)
    return (y + bonus) * g, S, p_rw[:, -1]


def _diff_attn_prompt(q, k, v, lam):
    B, S = q.shape[:2]
    qb_sz = Q_BLOCK if S % Q_BLOCK == 0 else S
    nb = S // qb_sz
    qb = jnp.moveaxis(q.reshape(B, nb, qb_sz, *q.shape[2:]), 1, 0)
    kpos = jnp.arange(S)
    scale = DA_DQK ** -0.5

    def block(args):
        qi, i = args
        s = jnp.einsum('bqhmd,bkhmd->bhmqk', qi, k) * scale
        qpos = i * qb_sz + jnp.arange(qb_sz)
        s = jnp.where(kpos[None, :] <= qpos[:, None], s, MASK_NEG)
        p = jax.nn.softmax(s, axis=-1)
        return jnp.einsum('bhqk,bkhe->bqhe', p[:, :, 0] - lam * p[:, :, 1], v)

    o = lax.map(block, (qb, jnp.arange(nb)))
    return jnp.moveaxis(o, 0, 1).reshape(B, S, N_HEADS, HEAD_DIM)


def _diff_attn_sample(q, k, v, k_past, v_past, lam):
    L = q.shape[1]
    P = k_past.shape[1]
    scale = DA_DQK ** -0.5
    s_past = jnp.einsum('bqhmd,bkhmd->bhmqk', q, k_past) * scale
    s_new = jnp.einsum('bqhmd,bkhmd->bhmqk', q, k) * scale
    s_new = jnp.where(jnp.tril(jnp.ones((L, L), bool)), s_new, MASK_NEG)
    p = jax.nn.softmax(jnp.concatenate([_f32(s_past), s_new], -1), axis=-1)
    pd = p[:, :, 0] - lam * p[:, :, 1]
    return (jnp.einsum('bhqk,bkhe->bqhe', pd[..., :P], v_past)
            + jnp.einsum('bhqk,bkhe->bqhe', pd[..., P:], v))


def _swiglu(x, wg, wu, wd):
    return (jax.nn.silu(x @ wg) * (x @ wu)) @ wd


def _grouped_experts(xf, idx, wts, wg, wu, wd):
    M, D = xf.shape
    A = M * TOP_K
    flat_e = idx.reshape(-1)
    order = jnp.argsort(flat_e)
    e_sorted = flat_e[order]
    tok_sorted = (order // TOP_K).astype(jnp.int32)
    w_sorted = wts.reshape(-1)[order]
    counts = jnp.bincount(flat_e, length=N_EXPERTS)
    padded = (counts + MOE_BLOCK - 1) // MOE_BLOCK * MOE_BLOCK
    pad_end = jnp.cumsum(padded)
    pad_start = pad_end - padded
    start = jnp.cumsum(counts) - counts
    dest = pad_start[e_sorted] + jnp.arange(A) - start[e_sorted]
    n_blocks = -(-A // MOE_BLOCK) + N_EXPERTS
    P = n_blocks * MOE_BLOCK
    row_tok = jnp.zeros((P,), jnp.int32).at[dest].set(tok_sorted)
    row_w = jnp.zeros((P,), jnp.float32).at[dest].set(w_sorted)
    blk_e = jnp.minimum(jnp.searchsorted(pad_end, jnp.arange(n_blocks) * MOE_BLOCK, side='right'), N_EXPERTS - 1)

    def body(acc, args):
        toks, rw, e = args
        yb = _swiglu(xf[toks], wg[e], wu[e], wd[e])
        return acc.at[toks].add(_f32(yb) * rw[:, None]), None

    acc, _ = lax.scan(body, jnp.zeros((M, D), jnp.float32),
                      (row_tok.reshape(n_blocks, MOE_BLOCK), row_w.reshape(n_blocks, MOE_BLOCK), blk_e))
    return acc.astype(xf.dtype)


def _moe(x, l, W):
    B, L, D = x.shape
    xf = x.reshape(-1, D)
    scores = jax.nn.sigmoid(_f32(xf @ W['router_w'][l]))
    _, idx = lax.top_k(scores + _f32(W['router_bias'][l]), TOP_K)
    wts = jnp.take_along_axis(scores, idx, axis=-1)
    wts = wts / jnp.sum(wts, -1, keepdims=True) * ROUTED_SCALE
    routed = _grouped_experts(xf, idx, wts, W['e_gate'][l], W['e_up'][l], W['e_down'][l])
    shared = _swiglu(xf, W['sh_gate'][l], W['sh_up'][l], W['sh_down'][l])
    return (routed + shared).reshape(B, L, D)


def _layer(l, x, pos, ret_s0, hg_s0, rw_s0, shift0, kv_past, W):
    B, L, _ = x.shape
    dt = x.dtype
    M = B * L
    proj2d, d_k, d_v = _in_proj(x.reshape(M, D_MODEL), W['w_in_bf16'][l], 256)
    proj = proj2d.reshape(B, L, N_COLS)
    rw_wts = _rwkv_weights(W, l)
    shift3 = _f32(shift0)[:, None, :]
    lb_rows = _hg_lb_rows(W['hg_lb'], l)
    hg_norm = jnp.tile(_f32(W['hg_norm_w'][l]), N_HEADS)
    ret_bd0, hg_bd0, rw_bd0 = _state_to_bd(_f32(ret_s0)), _hg_state_to_bd(_f32(hg_s0)), _state_to_bd(_f32(rw_s0))
    p_rw = proj[:, :, W_RET + W_HG:W_RET + W_HG + W_RW]

    lam_init = 0.8 - 0.6 * math.exp(-0.3 * l)
    lam = (jnp.exp(jnp.sum(_f32(W['da_lq1'][l]) * _f32(W['da_lk1'][l])))
           - jnp.exp(jnp.sum(_f32(W['da_lq2'][l]) * _f32(W['da_lk2'][l]))) + lam_init)
    c_da = W_RET + W_HG + W_RW
    if kv_past is None:
        assert L % HG_BLOCK == 0
        o_a, ret_bd = _ret_mix(proj2d, 0, ret_bd0, _ret_tables(pos, RET_CHUNK), B, L)
        o_b, hg_bd = _hgrn_mix(proj2d, W_RET, hg_bd0, lb_rows, hg_norm, l > 0, HG_BLOCK, B, L)
        o_c, rw_bd = _rwkv_mix(proj2d, RW_COL_BLOCK, shift3, rw_bd0, RW_CHUNK, B, L, rw_wts)
        ret_s, hg_s, rw_s = _bd_to_state(ret_bd), _hg_bd_to_state(hg_bd), _bd_to_state(rw_bd)
        norm_scale = jnp.tile(_f32(W['da_norm_w'][l]) * (1.0 - lam_init), N_HEADS)
        o_d = _attn_prompt(proj2d, lam, norm_scale, B, L)
    else:
        heads = lambda t, d: t.reshape(B, L, N_HEADS, d)
        r_q, r_k, r_v, r_g = jnp.split(proj[:, :, :W_RET], 4, axis=-1)
        log_gamma = jnp.log1p(-jnp.exp2(-5.0 - jnp.arange(N_HEADS, dtype=jnp.float32)))
        q = _rotary(heads(r_q, HEAD_DIM), pos)
        k = _rotary(heads(r_k, HEAD_DIM), pos) * HEAD_DIM ** -0.5
        o, ret_s = _chunk_scan(functools.partial(_retention_step, log_gamma=log_gamma),
                               (q, k, heads(r_v, HEAD_DIM)), _f32(ret_s0))
        o_a = (_rms(o).reshape(B, L, GROUP_W) * jax.nn.silu(r_g)).reshape(M, GROUP_W)

        h_q, h_f, h_i, h_g = jnp.split(proj[:, :, W_RET:W_RET + W_HG],
                                       [HG_QW, 2 * HG_QW, 2 * HG_QW + GROUP_W], axis=-1)
        if l == 0:
            log_f = jax.nn.log_sigmoid(h_f)
        else:
            log_f = jnp.logaddexp(lb_rows[0], lb_rows[1] + jax.nn.log_sigmoid(h_f))
        k_in = (lb_rows[2] if l > 0 else 1.0) * jax.nn.sigmoid(-h_f)
        q_h = jax.nn.silu(h_q) * HG_DK ** -0.5
        o, hg_s = _chunk_scan(_hgrn2_step, (heads(q_h, HG_DK), heads(k_in, HG_DK), heads(log_f, HG_DK),
                                            heads(h_i, HEAD_DIM)), _f32(hg_s0))
        o_b = ((_rms(o) * _f32(W['hg_norm_w'][l])).reshape(B, L, GROUP_W) * jax.nn.silu(h_g)).reshape(M, GROUP_W)

        o_c, rw_s, _ = _rwkv7(p_rw, shift0, rw_s0, l, W)
        o_c = o_c.reshape(M, GROUP_W)

        k_past, v_past = kv_past
        q5 = proj[:, :, c_da:c_da + GROUP_W].reshape(B, L, N_HEADS, 2, DA_DQK)
        o = _diff_attn_sample(q5, d_k.reshape(B, L, N_HEADS, 2, DA_DQK), d_v.reshape(B, L, N_HEADS, HEAD_DIM),
                              k_past.reshape(B, -1, N_HEADS, 2, DA_DQK), v_past, lam)
        o_d = (_rms(o) * _f32(W['da_norm_w'][l]) * (1.0 - lam_init)).reshape(M, GROUP_W)
    shift_new = p_rw[:, -1]

    x1 = _out_proj_ln((o_a, o_b, o_c, o_d), W['w_o_bf16'][l], x.reshape(M, D_MODEL),
                      W['ln1_g'][l], W['ln1_b'][l], 256)
    x = _moe_ln(x1, l, W).reshape(B, L, D_MODEL)
    new = (ret_s.astype(dt), hg_s.astype(dt), rw_s.astype(dt), shift_new,
           d_k.reshape(B, L, N_HEADS, HEAD_DIM), d_v.reshape(B, L, N_HEADS, HEAD_DIM))
    return x, new


def _prepare_weights(W):
    W = dict(W)
    depth = W['w_in'].shape[0]
    W['w_in_bf16'] = W['w_in'].astype(BF16)
    W['w_o_bf16'] = W['w_o'].astype(BF16)
    router_pad = jnp.zeros((depth, D_MODEL, 128 - N_EXPERTS), W['router_w'].dtype)
    W['gu_router_bf16'] = jnp.concatenate([W['sh_gate'], W['sh_up'], W['router_w'], router_pad], -1).astype(BF16)
    for name in ('e_gate', 'e_up', 'e_down', 'sh_down'):
        W[name + '_bf16'] = W[name].astype(BF16)
    return W


def kernel(x_prompt, x_sample, state_ret, state_hgrn, state_rwkv, state_rwkv_shift, cache_k, cache_v,
           page_table, w_in, w_o, hg_lb, hg_norm_w, rw_mu, rw_w0, rw_w_up, rw_a0, rw_a_up, rw_g_up,
           rw_k_k, rw_k_a, rw_r_k, rw_lnx_w, rw_lnx_b, da_lq1, da_lk1, da_lq2, da_lk2, da_norm_w,
           ln1_g, ln1_b, router_w, router_bias, e_gate, e_up, e_down, sh_gate, sh_up, sh_down,
           ln2_g, ln2_b):
    W = {'w_in': w_in, 'w_o': w_o, 'hg_lb': hg_lb, 'hg_norm_w': hg_norm_w, 'rw_mu': rw_mu,
         'rw_w0': rw_w0, 'rw_w_up': rw_w_up, 'rw_a0': rw_a0, 'rw_a_up': rw_a_up, 'rw_g_up': rw_g_up,
         'rw_k_k': rw_k_k, 'rw_k_a': rw_k_a, 'rw_r_k': rw_r_k, 'rw_lnx_w': rw_lnx_w, 'rw_lnx_b': rw_lnx_b,
         'da_lq1': da_lq1, 'da_lk1': da_lk1, 'da_lq2': da_lq2, 'da_lk2': da_lk2, 'da_norm_w': da_norm_w,
         'ln1_g': ln1_g, 'ln1_b': ln1_b, 'router_w': router_w, 'router_bias': router_bias,
         'e_gate': e_gate, 'e_up': e_up, 'e_down': e_down, 'sh_gate': sh_gate, 'sh_up': sh_up,
         'sh_down': sh_down, 'ln2_g': ln2_g, 'ln2_b': ln2_b}
    W = _prepare_weights(W)
    B, S, _ = x_prompt.shape
    DB, L, _ = x_sample.shape
    past_len = page_table.shape[1] * PAGE_SIZE
    pos_p = jnp.arange(S)
    pos_s = past_len + jnp.arange(L)
    zero_ret = jnp.zeros((B, N_HEADS, HEAD_DIM, HEAD_DIM), jnp.float32)
    zero_hg = jnp.zeros((B, N_HEADS, HG_DK, HEAD_DIM), jnp.float32)
    zero_shift = jnp.zeros((B, W_RW), x_prompt.dtype)
    yp, ys = x_prompt, x_sample
    new_p, new_s = [], []
    for l in range(DEPTH):
        yp, st = _layer(l, yp, pos_p, zero_ret, zero_hg, zero_ret, zero_shift, None, W)
        new_p.append(st)
        k_past = cache_k[l][page_table].reshape(DB, past_len, N_HEADS, HEAD_DIM)
        v_past = cache_v[l][page_table].reshape(DB, past_len, N_HEADS, HEAD_DIM)
        ys, st = _layer(l, ys, pos_s, state_ret[l], state_hgrn[l], state_rwkv[l], state_rwkv_shift[l],
                        (k_past, v_past), W)
        new_s.append(st)

    def stk(sts, i):
        return jnp.stack([s[i] for s in sts])

    return (yp, ys, stk(new_p, 0), stk(new_s, 0), stk(new_p, 1), stk(new_s, 1), stk(new_p, 2), stk(new_s, 2),
            stk(new_p, 3), stk(new_s, 3), stk(new_p, 4), stk(new_p, 5), stk(new_s, 4), stk(new_s, 5))
```

```python
import math, functools
import jax, jax.numpy as jnp
from jax import lax
from jax.experimental import pallas as pl
from jax.experimental.pallas import tpu as pltpu

D_MODEL = 1024
DEPTH = 2
PAGE_SIZE = 128
N_MIXERS = 4
GROUP_W = D_MODEL // N_MIXERS
HEAD_DIM = 64
N_HEADS = GROUP_W // HEAD_DIM
ROPE_BASE = 10000.0
HG_DK = 128
RW_DECAY_LORA = 64
RW_A_LORA = 64
RW_GATE_LORA = 128
RW_LNX_EPS = 6.4e-4
DA_DQK = HEAD_DIM // 2
W_RET = 3 * N_HEADS * HEAD_DIM + GROUP_W
W_HG = 2 * N_HEADS * HG_DK + N_HEADS * HEAD_DIM + GROUP_W
W_RW = 3 * GROUP_W + RW_DECAY_LORA + RW_A_LORA + RW_GATE_LORA
W_DA = 2 * (N_HEADS * 2 * DA_DQK) + N_HEADS * HEAD_DIM
N_COLS = W_RET + W_HG + W_RW + W_DA
RW_SPLITS = [GROUP_W, 2 * GROUP_W, 3 * GROUP_W, 3 * GROUP_W + RW_DECAY_LORA, 3 * GROUP_W + RW_DECAY_LORA + RW_A_LORA]
CHUNK = 64
Q_BLOCK = 128
MASK_NEG = -1e30
N_EXPERTS = 64
TOP_K = 8
D_EXPERT = 256
ROUTED_SCALE = 2.5
MOE_BLOCK = 128
ALPHA = (2.0 * DEPTH) ** 0.25
LN_EPS = 1e-5
NORM_EPS = 1e-5

V7X_VMEM_LIMIT_BYTES = 56 * 1024 * 1024


def _f32(t):
    return t.astype(jnp.float32)


def _matmul_body(x_ref, w_ref, o_ref):
    o_ref[...] = jnp.dot(x_ref[...].astype(jnp.bfloat16), w_ref[...],
                         preferred_element_type=jnp.float32)


def _matmul(x, w_bf16, tm):
    M, K = x.shape
    N = w_bf16.shape[1]
    tm = min(tm, M)
    assert M % tm == 0
    return pl.pallas_call(
        _matmul_body,
        grid=(M // tm,),
        in_specs=[pl.BlockSpec((tm, K), lambda i: (i, 0)),
                  pl.BlockSpec((K, N), lambda i: (0, 0))],
        out_specs=pl.BlockSpec((tm, N), lambda i: (i, 0)),
        out_shape=jax.ShapeDtypeStruct((M, N), jnp.float32),
        compiler_params=pltpu.CompilerParams(
            dimension_semantics=("arbitrary",), vmem_limit_bytes=V7X_VMEM_LIMIT_BYTES),
        name="matmul",
    )(x, w_bf16)


def _in_proj_body(x_ref, w_ref, o_ref, k_ref, v_ref):
    acc = jnp.dot(x_ref[...].astype(jnp.bfloat16), w_ref[...], preferred_element_type=jnp.float32)
    o_ref[...] = acc
    k_ref[...] = acc[:, N_COLS - 2 * GROUP_W:N_COLS - GROUP_W]
    v_ref[...] = acc[:, N_COLS - GROUP_W:]


def _in_proj(x, w_bf16, tm):
    M, K = x.shape
    tm = min(tm, M)
    assert M % tm == 0
    row = lambda i: (i, 0)
    return pl.pallas_call(
        _in_proj_body,
        grid=(M // tm,),
        in_specs=[pl.BlockSpec((tm, K), row), pl.BlockSpec((K, N_COLS), lambda i: (0, 0))],
        out_specs=[pl.BlockSpec((tm, N_COLS), row), pl.BlockSpec((tm, GROUP_W), row),
                   pl.BlockSpec((tm, GROUP_W), row)],
        out_shape=[jax.ShapeDtypeStruct((M, N_COLS), jnp.float32),
                   jax.ShapeDtypeStruct((M, GROUP_W), jnp.float32),
                   jax.ShapeDtypeStruct((M, GROUP_W), jnp.float32)],
        compiler_params=pltpu.CompilerParams(
            dimension_semantics=("arbitrary",), vmem_limit_bytes=V7X_VMEM_LIMIT_BYTES),
        name="in_proj",
    )(x, w_bf16)


def _ln_rows(z, g, b):
    mu = jnp.mean(z, -1, keepdims=True)
    zc = z - mu
    var = jnp.mean(zc * zc, -1, keepdims=True)
    return zc * lax.rsqrt(var + LN_EPS) * g + b


def _out_proj_ln_body(a_ref, b_ref, c_ref, d_ref, w_ref, res_ref, g_ref, beta_ref, o_ref):
    mix = None
    for j, part in enumerate((a_ref, b_ref, c_ref, d_ref)):
        term = jnp.dot(part[...].astype(jnp.bfloat16), w_ref[j * GROUP_W:(j + 1) * GROUP_W, :],
                       preferred_element_type=jnp.float32)
        mix = term if mix is None else mix + term
    o_ref[...] = _ln_rows(ALPHA * res_ref[...] + mix, g_ref[...], beta_ref[...])


def _out_proj_ln(parts, w_bf16, res, g, b, tm):
    M, N = res.shape
    tm = min(tm, M)
    assert M % tm == 0
    row = lambda i: (i, 0)
    fixed = lambda i: (0, 0)
    part = pl.BlockSpec((tm, GROUP_W), row)
    return pl.pallas_call(
        _out_proj_ln_body,
        grid=(M // tm,),
        in_specs=[part, part, part, part, pl.BlockSpec((N, N), fixed), pl.BlockSpec((tm, N), row),
                  pl.BlockSpec((1, N), fixed), pl.BlockSpec((1, N), fixed)],
        out_specs=pl.BlockSpec((tm, N), row),
        out_shape=jax.ShapeDtypeStruct((M, N), jnp.float32),
        compiler_params=pltpu.CompilerParams(
            dimension_semantics=("arbitrary",), vmem_limit_bytes=V7X_VMEM_LIMIT_BYTES),
        name="out_proj_ln",
    )(*parts, w_bf16, res, g.reshape(1, N), b.reshape(1, N))


EXPERT_ROWS = 256


def _start_row_gather(idx_ref, n_rows, src_hbm, dst_at, sem):
    def body(pair, carry):
        for priority in range(2):
            r = pair * 2 + priority
            t = idx_ref[0, 0, r]
            pltpu.make_async_copy(src_hbm.at[pl.ds(t, 1)], dst_at(r), sem).start(priority=priority)
        return carry
    lax.fori_loop(0, n_rows // 2, body, 0, unroll=4)


def _moe_experts_body(n_blocks, blk_ref, exp_ref, lo_ref, hi_ref, first_ref, tok_ref, tok_next_ref, x_hbm,
                      wg_ref, wu_ref, wd_ref, y_ref, xbuf, sem):
    w = pl.program_id(0)
    blk = blk_ref[w]
    slot = blk % 2
    lo = lo_ref[w]
    hi = hi_ref[w]

    def gather(idx_ref, s):
        _start_row_gather(idx_ref, EXPERT_ROWS, x_hbm, lambda r: xbuf.at[s, pl.ds(r, 1)], sem.at[s])

    @pl.when(w == 0)
    def _():
        gather(tok_ref, 0)

    @pl.when(first_ref[w] == 1)
    def _():
        @pl.when(blk + 1 < n_blocks)
        def _():
            gather(tok_next_ref, 1 - slot)
        pltpu.make_async_copy(x_hbm.at[pl.ds(0, EXPERT_ROWS)], xbuf.at[slot], sem.at[slot]).wait()
        y_ref[...] = jnp.zeros_like(y_ref)

    @pl.when(hi > lo)
    def _():
        x = xbuf[slot].astype(jnp.bfloat16)
        g = jnp.dot(x, wg_ref[0], preferred_element_type=jnp.float32)
        u = jnp.dot(x, wu_ref[0], preferred_element_type=jnp.float32)
        h = (g * jax.nn.sigmoid(g) * u).astype(jnp.bfloat16)
        y = jnp.dot(h, wd_ref[0], preferred_element_type=jnp.float32)
        r = lax.broadcasted_iota(jnp.int32, (EXPERT_ROWS, 1), 0)
        y_ref[...] += jnp.where((r >= lo) & (r < hi), y, 0.0)


def _moe_experts(x, tok_sorted, items, wg, wu, wd):
    M, D = x.shape
    A = tok_sorted.shape[0]
    assert A % EXPERT_ROWS == 0
    n_blocks = A // EXPERT_ROWS
    n_items = items[0].shape[0]
    tok3 = tok_sorted.reshape(n_blocks, 1, EXPERT_ROWS)
    smem_blk = lambda f: pl.BlockSpec((1, 1, EXPERT_ROWS), f, memory_space=pltpu.SMEM)
    w_spec = lambda shape: pl.BlockSpec(shape, lambda w, blk, e, *_: (e[w], 0, 0))
    grid_spec = pltpu.PrefetchScalarGridSpec(
        num_scalar_prefetch=5,
        grid=(n_items,),
        in_specs=[smem_blk(lambda w, blk, *_: (blk[w], 0, 0)),
                  smem_blk(lambda w, blk, *_: (jnp.minimum(blk[w] + 1, n_blocks - 1), 0, 0)),
                  pl.BlockSpec(memory_space=pl.ANY),
                  w_spec((1, D, D_EXPERT)), w_spec((1, D, D_EXPERT)), w_spec((1, D_EXPERT, D))],
        out_specs=pl.BlockSpec((EXPERT_ROWS, D), lambda w, blk, *_: (blk[w], 0)),
        scratch_shapes=[pltpu.VMEM((2, EXPERT_ROWS, D), jnp.float32), pltpu.SemaphoreType.DMA((2,))],
    )
    return pl.pallas_call(
        functools.partial(_moe_experts_body, n_blocks),
        grid_spec=grid_spec,
        out_shape=jax.ShapeDtypeStruct((A, D), jnp.float32),
        compiler_params=pltpu.CompilerParams(
            dimension_semantics=("arbitrary",), vmem_limit_bytes=V7X_VMEM_LIMIT_BYTES),
        name="moe_experts",
    )(*items, tok3, tok3, x, wg, wu, wd)


def _moe_work_items(e_sorted, A):
    n_blocks = A // EXPERT_ROWS
    group_end = jnp.searchsorted(e_sorted, jnp.arange(1, N_EXPERTS + 1, dtype=e_sorted.dtype)).astype(jnp.int32)
    starts = jnp.sort(jnp.concatenate([jnp.arange(n_blocks, dtype=jnp.int32) * EXPERT_ROWS, group_end]))
    ends = jnp.concatenate([starts[1:], jnp.full((1,), A, jnp.int32)])
    blk = jnp.minimum(starts // EXPERT_ROWS, n_blocks - 1)
    expert = jnp.minimum(jnp.searchsorted(group_end, starts, side='right'), N_EXPERTS - 1).astype(jnp.int32)
    new_start = jnp.concatenate([jnp.ones((1,), bool), starts[1:] != starts[:-1]])
    first = ((starts % EXPERT_ROWS == 0) & (starts < A) & new_start).astype(jnp.int32)
    base = blk * EXPERT_ROWS
    return blk, expert, starts - base, ends - base, first


COMBINE_ROWS = 128


def _moe_combine_body(pos_ref, pos_next_ref, wts_ref, x_ref, gu_ref, shd_ref, g_ref, b_ref, y_hbm,
                      o_ref, buf, sem):
    i = pl.program_id(0)
    n = pl.num_programs(0)
    slot = i % 2
    n_rows = COMBINE_ROWS * TOP_K

    def gather(idx_ref, s):
        def dst(r):
            k = jnp.bitwise_and(r, TOP_K - 1)
            token = lax.shift_right_logical(r, TOP_K.bit_length() - 1)
            return buf.at[s, pl.ds(k * COMBINE_ROWS + token, 1)]
        _start_row_gather(idx_ref, n_rows, y_hbm, dst, sem.at[s])

    @pl.when(i == 0)
    def _():
        gather(pos_ref, 0)

    @pl.when(i + 1 < n)
    def _():
        gather(pos_next_ref, 1 - slot)

    pltpu.make_async_copy(y_hbm.at[pl.ds(0, n_rows)], buf.at[slot], sem.at[slot]).wait()
    wts = wts_ref[...]
    routed = wts[:, 0:1] * buf[slot, pl.ds(0, COMBINE_ROWS)]
    for k in range(1, TOP_K):
        routed = routed + wts[:, k:k + 1] * buf[slot, pl.ds(k * COMBINE_ROWS, COMBINE_ROWS)]
    gu = gu_ref[...]
    g = gu[:, :D_EXPERT]
    u = gu[:, D_EXPERT:]
    h = (g * jax.nn.sigmoid(g) * u).astype(jnp.bfloat16)
    shared = jnp.dot(h, shd_ref[...], preferred_element_type=jnp.float32)
    o_ref[...] = _ln_rows(ALPHA * x_ref[...] + (routed + shared), g_ref[...], b_ref[...])


def _moe_combine(y_sorted, inv_pos, wts, x, gu, shd_bf16, g, b):
    M, D = x.shape
    tt = min(COMBINE_ROWS, M)
    assert tt == COMBINE_ROWS and M % tt == 0
    n_tiles = M // tt
    pos3 = inv_pos.reshape(n_tiles, 1, tt * TOP_K)
    row = lambda i: (i, 0)
    fixed = lambda i: (0, 0)
    smem_blk = lambda f: pl.BlockSpec((1, 1, tt * TOP_K), f, memory_space=pltpu.SMEM)
    return pl.pallas_call(
        _moe_combine_body,
        grid=(n_tiles,),
        in_specs=[smem_blk(lambda i: (i, 0, 0)),
                  smem_blk(lambda i: (jnp.minimum(i + 1, n_tiles - 1), 0, 0)),
                  pl.BlockSpec((tt, TOP_K), row),
                  pl.BlockSpec((tt, D), row),
                  pl.BlockSpec((tt, 2 * D_EXPERT), row),
                  pl.BlockSpec((D_EXPERT, D), fixed),
                  pl.BlockSpec((1, D), fixed), pl.BlockSpec((1, D), fixed),
                  pl.BlockSpec(memory_space=pl.ANY)],
        out_specs=pl.BlockSpec((tt, D), row),
        out_shape=jax.ShapeDtypeStruct((M, D), jnp.float32),
        scratch_shapes=[pltpu.VMEM((2, tt * TOP_K, D), jnp.float32), pltpu.SemaphoreType.DMA((2,))],
        compiler_params=pltpu.CompilerParams(
            dimension_semantics=("arbitrary",), vmem_limit_bytes=V7X_VMEM_LIMIT_BYTES),
        name="moe_combine_ln",
    )(pos3, pos3, wts, x, gu, shd_bf16, g.reshape(1, D), b.reshape(1, D), y_sorted)


def _moe_ln(x, l, W):
    M, D = x.shape
    rgu = _matmul(x, W['gu_router_bf16'][l], 256)
    scores = jax.nn.sigmoid(rgu[:, 2 * D_EXPERT:2 * D_EXPERT + N_EXPERTS])
    _, idx = lax.top_k(scores + _f32(W['router_bias'][l]), TOP_K)
    wts = jnp.take_along_axis(scores, idx, axis=-1)
    wts = wts / jnp.sum(wts, -1, keepdims=True) * ROUTED_SCALE

    A = M * TOP_K
    flat_e = idx.reshape(-1).astype(jnp.int32)
    assign = jnp.arange(A, dtype=jnp.int32)
    e_sorted, order = lax.sort_key_val(flat_e, assign)
    _, inv_pos = lax.sort_key_val(order, assign)
    tok_sorted = lax.shift_right_logical(order, TOP_K.bit_length() - 1)
    items = _moe_work_items(e_sorted, A)

    y_sorted = _moe_experts(x, tok_sorted, items, W['e_gate_bf16'][l], W['e_up_bf16'][l], W['e_down_bf16'][l])
    return _moe_combine(y_sorted, inv_pos, wts, x, rgu, W['sh_down_bf16'][l], W['ln2_g'][l], W['ln2_b'][l])


DA_COL_BLOCK = (W_RET + W_HG + W_RW) // GROUP_W
ATTN_BLOCK = 256
LOG2E = 1.4426950408889634
BF16 = jnp.bfloat16
F32 = jnp.float32


ACC_W = 128


def _attn_prompt_body(lam_ref, q_ref, k_ref, v_ref, nw_ref, o_ref, kbf, vh, qs, m_scr, acc_scr):
    i = pl.program_id(1)
    T = ATTN_BLOCK
    n_hm = 2 * N_HEADS
    S = k_ref.shape[0]

    @pl.when(i == 0)
    def _():
        kbf[...] = k_ref[...].astype(BF16)
        v = v_ref[...]
        ones_col = (lax.broadcasted_iota(jnp.int32, (S, ACC_W - HEAD_DIM), 1) == 0).astype(F32)
        for h in range(N_HEADS):
            vh[h] = jnp.concatenate([v[:, h * HEAD_DIM:(h + 1) * HEAD_DIM], ones_col], -1).astype(BF16)

    q = q_ref[...]
    lane = lax.broadcasted_iota(jnp.int32, (T, GROUP_W), 1)
    for hm in range(n_hm):
        keep = (lane >= hm * DA_DQK) & (lane < (hm + 1) * DA_DQK)
        qs[pl.ds(hm * T, T), :] = jnp.where(keep, q, 0.0).astype(BF16)
    m_scr[...] = jnp.full_like(m_scr, MASK_NEG)
    acc_scr[...] = jnp.zeros_like(acc_scr)
    score_scale = DA_DQK ** -0.5 * LOG2E

    def kv_block(j, masked):
        kb = kbf[pl.ds(j * T, T), :]
        for h in range(N_HEADS):
            rows = pl.ds(h * 2 * T, 2 * T)
            s = lax.dot_general(qs[rows, :], kb, (((1,), (1,)), ((), ())), preferred_element_type=F32)
            s = s * score_scale
            if masked:
                qpos = lax.broadcasted_iota(jnp.int32, (2 * T, T), 0) % T
                kpos = lax.broadcasted_iota(jnp.int32, (2 * T, T), 1)
                s = jnp.where(kpos <= qpos, s, MASK_NEG)
            m_old = m_scr[rows, :]
            m_new = jnp.maximum(m_old, jnp.max(s, -1, keepdims=True))
            alpha = jnp.exp2(m_old - m_new)
            p = jnp.exp2(s - jnp.concatenate([m_new] * (T // ACC_W), -1))
            pv = jnp.dot(p.astype(BF16), vh[h, pl.ds(j * T, T), :], preferred_element_type=F32)
            acc_scr[rows, :] = alpha * acc_scr[rows, :] + pv
            m_scr[rows, :] = m_new

    def body(j, c):
        kv_block(j, False)
        return c
    lax.fori_loop(0, i, body, 0)
    kv_block(i, True)

    lam = lam_ref[0]
    outs = []
    for h in range(N_HEADS):
        a0 = acc_scr[pl.ds(h * 2 * T, T), :]
        a1 = acc_scr[pl.ds(h * 2 * T + T, T), :]
        o = (a0[:, :HEAD_DIM] / a0[:, HEAD_DIM:HEAD_DIM + 1]
             - lam * (a1[:, :HEAD_DIM] / a1[:, HEAD_DIM:HEAD_DIM + 1]))
        outs.append(o * lax.rsqrt(jnp.mean(o * o, -1, keepdims=True) + NORM_EPS))
    o_ref[...] = jnp.concatenate(outs, -1) * nw_ref[...]


def _attn_prompt(proj, lam, norm_scale, B, S):
    T = ATTN_BLOCK
    assert S % T == 0
    nq = S // T
    n_rows = 2 * N_HEADS * T
    return pl.pallas_call(
        _attn_prompt_body,
        grid=(B, nq),
        in_specs=[pl.BlockSpec(memory_space=pltpu.SMEM),
                  pl.BlockSpec((T, GROUP_W), lambda b, i: (b * nq + i, DA_COL_BLOCK)),
                  pl.BlockSpec((S, GROUP_W), lambda b, i: (b, DA_COL_BLOCK + 1)),
                  pl.BlockSpec((S, GROUP_W), lambda b, i: (b, DA_COL_BLOCK + 2)),
                  pl.BlockSpec((1, GROUP_W), lambda b, i: (0, 0))],
        out_specs=pl.BlockSpec((T, GROUP_W), lambda b, i: (b * nq + i, 0)),
        out_shape=jax.ShapeDtypeStruct((B * S, GROUP_W), F32),
        scratch_shapes=[pltpu.VMEM((S, GROUP_W), BF16),
                        pltpu.VMEM((N_HEADS, S, ACC_W), BF16),
                        pltpu.VMEM((n_rows, GROUP_W), BF16),
                        pltpu.VMEM((n_rows, ACC_W), F32),
                        pltpu.VMEM((n_rows, ACC_W), F32)],
        compiler_params=pltpu.CompilerParams(
            dimension_semantics=("arbitrary", "arbitrary"), vmem_limit_bytes=V7X_VMEM_LIMIT_BYTES),
        name="diff_attn_prompt",
    )(lam.reshape(1), proj, proj, proj, norm_scale.reshape(1, GROUP_W))


SAMPLE_Q_ROWS = 8
PAGES_PER_STEP = 16


def _attn_sample_body(n_new, n_pages, pt_ref, lam_ref, q_ref, kn_ref, vn_ref, nw_ref, ck_hbm, cv_hbm, o_ref,
                      kbuf, vbuf, sem, qs, m_scr, l_scr, acc_scr):
    b = pl.program_id(0)
    c = pl.program_id(1)
    nc = pl.num_programs(1)
    step = b * nc + c
    n_steps = pl.num_programs(0) * nc
    slot = step % 2
    PG, TQ = PAGES_PER_STEP, SAMPLE_Q_ROWS
    n_hm = 2 * N_HEADS
    n_rows = n_hm * TQ

    def fetch(bb, cc, s):
        base = bb * n_pages + cc * PG
        for i in range(PG):
            pid = pt_ref[base + i]
            pltpu.make_async_copy(ck_hbm.at[pl.ds(pid, 1)], kbuf.at[s, pl.ds(i, 1)], sem.at[0, s]).start()
            pltpu.make_async_copy(cv_hbm.at[pl.ds(pid, 1)], vbuf.at[s, pl.ds(i, 1)], sem.at[1, s]).start()

    @pl.when(step == 0)
    def _():
        fetch(b, c, 0)

    @pl.when(step + 1 < n_steps)
    def _():
        wrap = c + 1 == nc
        fetch(jnp.where(wrap, b + 1, b), jnp.where(wrap, 0, c + 1), 1 - slot)

    @pl.when(c == 0)
    def _():
        q = q_ref[0] * (DA_DQK ** -0.5 * LOG2E)
        lane = lax.broadcasted_iota(jnp.int32, (TQ, GROUP_W), 1)
        for hm in range(n_hm):
            keep = (lane >= hm * DA_DQK) & (lane < (hm + 1) * DA_DQK)
            qs[pl.ds(hm * TQ, TQ), :] = jnp.where(keep, q, 0.0).astype(BF16)
        m_scr[...] = jnp.full_like(m_scr, MASK_NEG)
        l_scr[...] = jnp.zeros_like(l_scr)
        acc_scr[...] = jnp.zeros_like(acc_scr)

    def update(s, v_bf):
        m_old = m_scr[...]
        m_new = jnp.maximum(m_old, jnp.max(s, -1, keepdims=True))
        alpha = jnp.exp2(m_old - m_new)
        p = jnp.exp2(s - m_new)
        l_scr[...] = alpha * l_scr[...] + jnp.sum(p, -1, keepdims=True)
        acc_scr[...] = alpha * acc_scr[...] + _dot(p.astype(BF16), v_bf)
        m_scr[...] = m_new

    pltpu.make_async_copy(ck_hbm.at[pl.ds(0, PG)], kbuf.at[slot], sem.at[0, slot]).wait()
    pltpu.make_async_copy(cv_hbm.at[pl.ds(0, PG)], vbuf.at[slot], sem.at[1, slot]).wait()
    k_bf = kbuf[slot].reshape(PG * PAGE_SIZE, GROUP_W).astype(BF16)
    v_bf = vbuf[slot].reshape(PG * PAGE_SIZE, GROUP_W).astype(BF16)
    update(_dot_nt(qs[...], k_bf), v_bf)

    @pl.when(c == nc - 1)
    def _():
        s_new = _dot_nt(qs[...], kn_ref[0].astype(BF16))
        t = lax.broadcasted_iota(jnp.int32, (n_rows, TQ), 0) % TQ
        j = lax.broadcasted_iota(jnp.int32, (n_rows, TQ), 1)
        update(jnp.where((j <= t) & (j < n_new), s_new, MASK_NEG), vn_ref[0].astype(BF16))
        lam = lam_ref[0]
        ratio = acc_scr[...] / l_scr[...]
        lane_head = lax.broadcasted_iota(jnp.int32, (TQ, GROUP_W), 1) // HEAD_DIM
        o = jnp.zeros((TQ, GROUP_W), F32)
        for h in range(N_HEADS):
            o_h = ratio[(2 * h) * TQ:(2 * h + 1) * TQ] - lam * ratio[(2 * h + 1) * TQ:(2 * h + 2) * TQ]
            o = o + jnp.where(lane_head == h, o_h, 0.0)
        ms = _dot_exact_rhs(o * o, _head_seg_ones()) * (1.0 / HEAD_DIM)
        o_ref[0] = o * lax.rsqrt(ms + NORM_EPS) * nw_ref[...]


def _attn_sample(q8, k8, v8, cache_k3, cache_v3, page_table, lam, norm_scale, n_new):
    B, n_pages = page_table.shape
    PG, TQ = PAGES_PER_STEP, SAMPLE_Q_ROWS
    assert n_pages % PG == 0 and n_new <= TQ
    n_rows = 2 * N_HEADS * TQ
    seq = lambda: pl.BlockSpec((1, TQ, GROUP_W), lambda b, c, pt: (b, 0, 0))
    grid_spec = pltpu.PrefetchScalarGridSpec(
        num_scalar_prefetch=1,
        grid=(B, n_pages // PG),
        in_specs=[pl.BlockSpec(memory_space=pltpu.SMEM), seq(), seq(), seq(),
                  pl.BlockSpec((1, GROUP_W), lambda b, c, pt: (0, 0)),
                  pl.BlockSpec(memory_space=pl.ANY), pl.BlockSpec(memory_space=pl.ANY)],
        out_specs=seq(),
        scratch_shapes=[pltpu.VMEM((2, PG, PAGE_SIZE, GROUP_W), F32), pltpu.VMEM((2, PG, PAGE_SIZE, GROUP_W), F32),
                        pltpu.SemaphoreType.DMA((2, 2)), pltpu.VMEM((n_rows, GROUP_W), BF16),
                        pltpu.VMEM((n_rows, 1), F32), pltpu.VMEM((n_rows, 1), F32),
                        pltpu.VMEM((n_rows, GROUP_W), F32)],
    )
    return pl.pallas_call(
        functools.partial(_attn_sample_body, n_new, n_pages),
        grid_spec=grid_spec,
        out_shape=jax.ShapeDtypeStruct((B, TQ, GROUP_W), F32),
        compiler_params=pltpu.CompilerParams(
            dimension_semantics=("arbitrary", "arbitrary"), vmem_limit_bytes=V7X_VMEM_LIMIT_BYTES),
        name="diff_attn_sample",
    )(page_table.reshape(-1), lam.reshape(1), q8, k8, v8, norm_scale.reshape(1, GROUP_W), cache_k3, cache_v3)


RW_COL_BLOCK = (W_RET + W_HG) // GROUP_W
RW_CHUNK = 64


def _dot(a, b):
    return jnp.dot(a, b, preferred_element_type=F32)


def _dot_nt(a, b):
    return lax.dot_general(a, b, (((1,), (1,)), ((), ())), preferred_element_type=F32)


def _dot_tn(a, b):
    return lax.dot_general(a, b, (((0,), (0,)), ((), ())), preferred_element_type=F32)


def _split3(x):
    hi = x.astype(BF16)
    r1 = x - hi.astype(F32)
    mid = r1.astype(BF16)
    lo = (r1 - mid.astype(F32)).astype(BF16)
    return hi, mid, lo


def _dot_exact_rhs(x, m_bf16):
    hi, mid, lo = _split3(x)
    return _dot(hi, m_bf16) + _dot(mid, m_bf16) + _dot(lo, m_bf16)


def _dot_exact_lhs(m_bf16, x):
    hi, mid, lo = _split3(x)
    return _dot(m_bf16, hi) + _dot(m_bf16, mid) + _dot(m_bf16, lo)


def _hi_lo(x):
    hi = x.astype(BF16)
    return hi, (x - hi.astype(F32)).astype(BF16)


def _mm3(a_hl, b_hl, dot):
    (ah, al), (bh, bl) = a_hl, b_hl
    return dot(ah, bh) + dot(ah, bl) + dot(al, bh)


def _rwkv_body(n_valid, pr_ref, pk_ref, pv_ref, pl_ref, shift0_ref, s0_ref, mu_ref, w0_ref, wup_ref, a0_ref,
               aup_ref, gup_ref, kk_ref, ka_ref, rk_ref, lnw_ref, lnb_ref, o_ref, s_out_ref, sbd, carry):
    c_idx = pl.program_id(1)
    T = RW_CHUNK
    HT = N_HEADS * T

    @pl.when(c_idx == 0)
    def _():
        sbd[...] = s0_ref[0]
        carry[...] = shift0_ref[0]

    row = lax.broadcasted_iota(jnp.int32, (T, GROUP_W), 0)
    lane_head = lax.broadcasted_iota(jnp.int32, (T, GROUP_W), 1) // HEAD_DIM

    def shifted(p_ref, blk):
        cols = slice(blk * GROUP_W, (blk + 1) * GROUP_W)
        p = p_ref[...]
        prev = jnp.where(row == 0, carry[:, cols], pltpu.roll(p, 1, axis=0))
        carry[:, cols] = p[T - 1:T, :]
        return p + (prev - p) * mu_ref[:, cols]

    xr = shifted(pr_ref, 0)
    xk = shifted(pk_ref, 1)
    xv = shifted(pv_ref, 2)
    xl = shifted(pl_ref, 3)

    gi = lax.broadcasted_iota(jnp.int32, (GROUP_W, GROUP_W), 0) // HEAD_DIM
    gj = lax.broadcasted_iota(jnp.int32, (GROUP_W, GROUP_W), 1) // HEAD_DIM
    seg = (gi == gj).astype(BF16)

    z = -(w0_ref[...] + _dot(jnp.tanh(xl).astype(BF16), wup_ref[...]))
    softplus = jnp.maximum(z, 0.0) + jnp.log(1.0 + jnp.exp(-jnp.abs(z)))
    lw = -jnp.exp(-softplus - 0.5)
    a = jax.nn.sigmoid(a0_ref[...] + _dot(xl.astype(BF16), aup_ref[...]))
    g = _dot(jax.nn.sigmoid(xl).astype(BF16), gup_ref[...])
    kk = xk * kk_ref[...]
    kkn = kk / jnp.maximum(jnp.sqrt(_dot_exact_rhs(kk * kk, seg)), 1e-12)
    k_mod = xk * (1.0 + (a - 1.0) * ka_ref[...])
    an = -kkn
    bb = kkn * a
    if n_valid < T:
        valid = row < n_valid
        lw = jnp.where(valid, lw, 0.0)
        an = jnp.where(valid, an, 0.0)
        k_mod = jnp.where(valid, k_mod, 0.0)

    ti = lax.broadcasted_iota(jnp.int32, (T, T), 0)
    tj = lax.broadcasted_iota(jnp.int32, (T, T), 1)
    c = _dot_exact_lhs((tj <= ti).astype(BF16), lw)
    c_last = c[T - 1:T, :]
    inv_dec = jnp.exp(-c)
    to_end = jnp.exp(c_last - c)
    a_t = an * jnp.exp(c - lw)
    b_t = bb * inv_dec
    k_t = k_mod * inv_dec
    r_t = xr * jnp.exp(c)

    def stack(x):
        return jnp.concatenate([jnp.where(lane_head == h, x, 0.0) for h in range(N_HEADS)], axis=0)

    def tile(x):
        return jnp.concatenate([x] * N_HEADS, axis=0)

    ri = lax.broadcasted_iota(jnp.int32, (HT, HT), 0)
    ci = lax.broadcasted_iota(jnp.int32, (HT, HT), 1)
    same_head = (ri // T) == (ci // T)
    m_strict = same_head & ((ci % T) < (ri % T))
    m_incl = same_head & ((ci % T) <= (ri % T))

    a_p = _hi_lo(stack(a_t))
    r_p = _hi_lo(stack(r_t))
    b_p = _hi_lo(tile(b_t))
    k_p = _hi_lo(tile(k_t))
    a_ab = jnp.where(m_strict, _mm3(a_p, b_p, _dot_nt), 0.0)
    a_ak = jnp.where(m_strict, _mm3(a_p, k_p, _dot_nt), 0.0)
    a_rb = jnp.where(m_incl, _mm3(r_p, b_p, _dot_nt), 0.0)
    a_rk = jnp.where(m_incl, _mm3(r_p, k_p, _dot_nt), 0.0)

    s_old = sbd[...]
    s_p = _hi_lo(s_old)
    v_p = _hi_lo(stack(xv))

    u = _mm3(a_p, s_p, _dot_nt) + _mm3(_hi_lo(a_ak), v_p, _dot)
    pw = a_ab
    n_steps = max(1, (T - 1).bit_length())
    for step in range(n_steps):
        pw_p = _hi_lo(pw)
        u = u + _mm3(pw_p, _hi_lo(u), _dot)
        if step + 1 < n_steps:
            pw = _mm3(pw_p, pw_p, _dot)

    u_p = _hi_lo(u)
    y_s = _mm3(r_p, s_p, _dot_nt) + _mm3(_hi_lo(a_rb), u_p, _dot) + _mm3(_hi_lo(a_rk), v_p, _dot)
    y = y_s[0:T]
    for h in range(1, N_HEADS):
        y = y + y_s[h * T:(h + 1) * T]

    s_new = (s_old * jnp.exp(c_last)
             + _mm3(u_p, _hi_lo(stack(bb * to_end)), _dot_tn) + _mm3(v_p, _hi_lo(stack(k_mod * to_end)), _dot_tn))
    sbd[...] = s_new

    @pl.when(c_idx == pl.num_programs(1) - 1)
    def _():
        s_out_ref[0] = s_new

    inv_n = 1.0 / HEAD_DIM
    yc = y - _dot_exact_rhs(y, seg) * inv_n
    var = _dot_exact_rhs(yc * yc, seg) * inv_n
    yn = yc * lax.rsqrt(var + RW_LNX_EPS) * lnw_ref[...] + lnb_ref[...]
    bonus = _dot_exact_rhs(xr * k_mod * rk_ref[...], seg) * xv
    o_ref[...] = (yn + bonus) * g


def _rwkv_mix(p, col_block0, shift0, s0_bd, n_valid, B, L, wts):
    T = RW_CHUNK
    assert L % T == 0
    nc = L // T
    blk = lambda j: pl.BlockSpec((T, GROUP_W), lambda b, c: (b * nc + c, col_block0 + j))
    fixed = lambda shape: pl.BlockSpec(shape, lambda b, c: (0,) * len(shape))
    vec = fixed((1, GROUP_W))
    mat = fixed((GROUP_W, GROUP_W))
    return pl.pallas_call(
        functools.partial(_rwkv_body, n_valid),
        grid=(B, nc),
        in_specs=[blk(0), blk(1), blk(2), blk(3),
                  pl.BlockSpec((1, 1, W_RW), lambda b, c: (b, 0, 0)),
                  pl.BlockSpec((1, GROUP_W, GROUP_W), lambda b, c: (b, 0, 0)),
                  fixed((1, W_RW)), vec, mat, vec, mat, mat, vec, vec, vec, vec, vec],
        out_specs=[pl.BlockSpec((T, GROUP_W), lambda b, c: (b * nc + c, 0)),
                   pl.BlockSpec((1, GROUP_W, GROUP_W), lambda b, c: (b, 0, 0))],
        out_shape=[jax.ShapeDtypeStruct((B * L, GROUP_W), F32),
                   jax.ShapeDtypeStruct((B, GROUP_W, GROUP_W), F32)],
        scratch_shapes=[pltpu.VMEM((GROUP_W, GROUP_W), F32), pltpu.VMEM((1, W_RW), F32)],
        compiler_params=pltpu.CompilerParams(
            dimension_semantics=("arbitrary", "arbitrary"), vmem_limit_bytes=V7X_VMEM_LIMIT_BYTES),
        name="rwkv7_mix",
    )(p, p, p, p, shift0, s0_bd, *wts)


def _rwkv_weights(W, l):
    z = lambda n: jnp.zeros((n, GROUP_W), F32)
    wup = jnp.concatenate([_f32(W['rw_w_up'][l]), z(GROUP_W - RW_DECAY_LORA)], 0).astype(BF16)
    aup = jnp.concatenate([z(RW_DECAY_LORA), _f32(W['rw_a_up'][l]), z(RW_GATE_LORA)], 0).astype(BF16)
    gup = jnp.concatenate([z(RW_DECAY_LORA + RW_A_LORA), _f32(W['rw_g_up'][l])], 0).astype(BF16)
    r1 = lambda t: _f32(t).reshape(1, -1)
    return (r1(W['rw_mu'][l]), r1(W['rw_w0'][l]), wup, r1(W['rw_a0'][l]), aup, gup, r1(W['rw_k_k'][l]),
            r1(W['rw_k_a'][l]), r1(W['rw_r_k'][l]), r1(W['rw_lnx_w'][l]), r1(W['rw_lnx_b'][l]))


def _state_to_bd(s):
    eye = jnp.eye(N_HEADS, dtype=s.dtype)
    return (s[:, :, :, None, :] * eye[None, :, None, :, None]).reshape(s.shape[0], GROUP_W, GROUP_W)


def _bd_to_state(sbd):
    s5 = sbd.reshape(sbd.shape[0], N_HEADS, HEAD_DIM, N_HEADS, HEAD_DIM)
    return jnp.stack([s5[:, h, :, h, :] for h in range(N_HEADS)], axis=1)


def _head_seg_ones():
    si = lax.broadcasted_iota(jnp.int32, (GROUP_W, GROUP_W), 0) // HEAD_DIM
    sj = lax.broadcasted_iota(jnp.int32, (GROUP_W, GROUP_W), 1) // HEAD_DIM
    return (si == sj).astype(BF16)


HG_QW = N_HEADS * HG_DK
HG_BLOCK = 128
HG_SUB = 16


def _hgrn_body(has_lb, n_valid, hq_ref, hf_ref, hi_ref, hg_ref, s0_ref, lb_ref, nw_ref, o_ref, s_out_ref,
               st, qt_s, kh_s, v_s, dec_s, oi_s):
    c_idx = pl.program_id(1)
    TB, C = HG_BLOCK, HG_SUB

    @pl.when(c_idx == 0)
    def _():
        st[...] = s0_ref[0]

    hf = hf_ref[...]
    log_sig = jnp.minimum(hf, 0.0) - jnp.log(1.0 + jnp.exp(-jnp.abs(hf)))
    if has_lb:
        log_lb = lb_ref[0:1, :]
        t2 = lb_ref[1:2, :] + log_sig
        lf = jnp.maximum(log_lb, t2) + jnp.log(1.0 + jnp.exp(-jnp.abs(log_lb - t2)))
        k_in = lb_ref[2:3, :] * jax.nn.sigmoid(-hf)
    else:
        lf = log_sig
        k_in = jax.nn.sigmoid(-hf)
    hq = hq_ref[...]
    q = hq * jax.nn.sigmoid(hq) * (HG_DK ** -0.5)
    v = hi_ref[...]
    row = lax.broadcasted_iota(jnp.int32, (TB, HG_QW), 0)
    if n_valid < TB:
        lf = jnp.where(row < n_valid, lf, 0.0)
        k_in = jnp.where(row < n_valid, k_in, 0.0)

    ti = lax.broadcasted_iota(jnp.int32, (TB, TB), 0)
    tj = lax.broadcasted_iota(jnp.int32, (TB, TB), 1)
    same = (ti // C) == (tj // C)
    b = _dot_exact_lhs((same & (tj <= ti)).astype(BF16), lf)
    b_last = _dot_exact_lhs(same.astype(BF16), lf)

    gi = lax.broadcasted_iota(jnp.int32, (HG_QW, GROUP_W), 0) // HG_DK
    gj = lax.broadcasted_iota(jnp.int32, (HG_QW, GROUP_W), 1) // HEAD_DIM
    head_sum = (gi == gj).astype(BF16)

    off = row % C
    off_v = lax.broadcasted_iota(jnp.int32, (TB, GROUP_W), 0) % C
    o_intra = _dot((q * k_in).astype(BF16), head_sum) * v
    for d in range(1, C):
        ok = off >= d
        e = jnp.exp(jnp.where(ok, b - pltpu.roll(b, d, axis=0), 0.0))
        p = jnp.where(ok, q * e * pltpu.roll(k_in, d, axis=0), 0.0)
        v_d = jnp.where(off_v >= d, pltpu.roll(v, d, axis=0), 0.0)
        o_intra = o_intra + _dot(p.astype(BF16), head_sum) * v_d

    qt_s[...] = (q * jnp.exp(b)).astype(BF16)
    kh_s[...] = (k_in * jnp.exp(b_last - b)).astype(BF16)
    v_s[...] = v.astype(BF16)
    dec_s[...] = jnp.exp(b_last)

    bi = lax.broadcasted_iota(jnp.int32, (GROUP_W, HG_QW), 0) // HEAD_DIM
    bj = lax.broadcasted_iota(jnp.int32, (GROUP_W, HG_QW), 1) // HG_DK
    block_diag = bi == bj

    def group(s, carry):
        start = pl.multiple_of(s * C, C)
        rows = pl.ds(start, C)
        s_cur = st[...]
        oi_s[rows, :] = _dot_nt(qt_s[rows, :], s_cur.astype(BF16))
        upd = _dot_tn(v_s[rows, :], kh_s[rows, :])
        st[...] = s_cur * dec_s[pl.ds(start, 1), :] + jnp.where(block_diag, upd, 0.0)
        return carry
    lax.fori_loop(0, TB // C, group, 0)

    @pl.when(c_idx == pl.num_programs(1) - 1)
    def _():
        s_out_ref[0] = st[...]

    o = o_intra + oi_s[...]
    ms = _dot_exact_rhs(o * o, _head_seg_ones()) * (1.0 / HEAD_DIM)
    hg = hg_ref[...]
    o_ref[...] = o * lax.rsqrt(ms + NORM_EPS) * nw_ref[...] * (hg * jax.nn.sigmoid(hg))


def _hgrn_mix(p, col0, s0_bd, lb_rows, norm_w4, has_lb, n_valid, B, L):
    TB = HG_BLOCK
    assert L % TB == 0 and col0 % HG_QW == 0
    nc = L // TB
    qb, gb = col0 // HG_QW, (col0 + 2 * HG_QW) // GROUP_W
    fixed = lambda shape: pl.BlockSpec(shape, lambda b, c: (0,) * len(shape))
    rows = lambda width, j: pl.BlockSpec((TB, width), lambda b, c: (b * nc + c, j))
    return pl.pallas_call(
        functools.partial(_hgrn_body, has_lb, n_valid),
        grid=(B, nc),
        in_specs=[rows(HG_QW, qb), rows(HG_QW, qb + 1), rows(GROUP_W, gb), rows(GROUP_W, gb + 1),
                  pl.BlockSpec((1, GROUP_W, HG_QW), lambda b, c: (b, 0, 0)),
                  fixed((8, HG_QW)), fixed((1, GROUP_W))],
        out_specs=[rows(GROUP_W, 0), pl.BlockSpec((1, GROUP_W, HG_QW), lambda b, c: (b, 0, 0))],
        out_shape=[jax.ShapeDtypeStruct((B * L, GROUP_W), F32),
                   jax.ShapeDtypeStruct((B, GROUP_W, HG_QW), F32)],
        scratch_shapes=[pltpu.VMEM((GROUP_W, HG_QW), F32), pltpu.VMEM((TB, HG_QW), BF16),
                        pltpu.VMEM((TB, HG_QW), BF16), pltpu.VMEM((TB, GROUP_W), BF16),
                        pltpu.VMEM((TB, HG_QW), F32), pltpu.VMEM((TB, GROUP_W), F32)],
        compiler_params=pltpu.CompilerParams(
            dimension_semantics=("arbitrary", "arbitrary"), vmem_limit_bytes=V7X_VMEM_LIMIT_BYTES),
        name="hgrn2_mix",
    )(p, p, p, p, s0_bd, lb_rows, norm_w4.reshape(1, GROUP_W))


def _hg_state_to_bd(s):
    eye = jnp.eye(N_HEADS, dtype=s.dtype)
    st = jnp.swapaxes(s, 2, 3)
    return (st[:, :, :, None, :] * eye[None, :, None, :, None]).reshape(s.shape[0], GROUP_W, HG_QW)


def _hg_bd_to_state(sbd):
    s5 = sbd.reshape(sbd.shape[0], N_HEADS, HEAD_DIM, N_HEADS, HG_DK)
    return jnp.swapaxes(jnp.stack([s5[:, h, :, h, :] for h in range(N_HEADS)], axis=1), 2, 3)


def _hg_lb_rows(hg_lb, l):
    lb_soft = jax.nn.softmax(_f32(hg_lb), axis=0)
    lb = (jnp.cumsum(lb_soft, axis=0) - lb_soft[0])[l]
    z = jnp.zeros_like(lb)
    if l == 0:
        return jnp.stack([z] * 8)
    return jnp.stack([jnp.log(lb), jnp.log1p(-lb), 1.0 - lb, z, z, z, z, z])


RET_CHUNK = 64


def _ret_body(q_ref, k_ref, v_ref, g_ref, cos_ref, sin_ref, dmask_ref, qdec_ref, kdec_ref, sdec_ref, s0_ref,
              o_ref, s_out_ref, st):
    c_idx = pl.program_id(1)
    T = RET_CHUNK

    @pl.when(c_idx == 0)
    def _():
        st[...] = s0_ref[0]

    lane = lax.broadcasted_iota(jnp.int32, (T, GROUP_W), 1)
    upper_half = (lane % HEAD_DIM) >= (HEAD_DIM // 2)
    lane_head = lane // HEAD_DIM
    cos = cos_ref[...]
    sin = sin_ref[...]

    def rotary(x):
        swapped = jnp.where(upper_half, pltpu.roll(x, HEAD_DIM // 2, axis=1),
                            pltpu.roll(x, GROUP_W - HEAD_DIM // 2, axis=1))
        return x * cos + swapped * sin

    def stack(x):
        return jnp.concatenate([jnp.where(lane_head == h, x, 0.0) for h in range(N_HEADS)], axis=0)

    q = rotary(q_ref[...])
    k = rotary(k_ref[...]) * (HEAD_DIM ** -0.5)
    q_s = stack(q).astype(BF16)
    k_tl = jnp.concatenate([k] * N_HEADS, axis=0).astype(BF16)
    v_m = stack(v_ref[...]).astype(BF16)
    scores = _dot_nt(q_s, k_tl) * dmask_ref[...]
    s_old = st[...]
    o_s = _dot(scores.astype(BF16), v_m) + _dot(q_s, s_old.astype(BF16)) * qdec_ref[...]
    o = o_s[0:T]
    for h in range(1, N_HEADS):
        o = o + o_s[h * T:(h + 1) * T]
    k_w = (stack(k) * kdec_ref[...]).astype(BF16)
    s_new = s_old * sdec_ref[...] + _dot_tn(k_w, v_m)
    st[...] = s_new

    @pl.when(c_idx == pl.num_programs(1) - 1)
    def _():
        s_out_ref[0] = s_new

    ms = _dot_exact_rhs(o * o, _head_seg_ones()) * (1.0 / HEAD_DIM)
    g = g_ref[...]
    o_ref[...] = o * lax.rsqrt(ms + NORM_EPS) * (g * jax.nn.sigmoid(g))


def _ret_tables(pos, n_valid):
    T = RET_CHUNK
    half = HEAD_DIM // 2
    freq = 1.0 / (ROPE_BASE ** jnp.linspace(0.0, 1.0, half, dtype=F32))
    ang = _f32(pos)[:, None] * freq[None, :]
    cos = jnp.tile(jnp.cos(ang), (1, 2 * N_HEADS))
    sin = jnp.tile(jnp.concatenate([-jnp.sin(ang), jnp.sin(ang)], -1), (1, N_HEADS))
    log_gamma = jnp.log1p(-jnp.exp2(-5.0 - jnp.arange(N_HEADS, dtype=F32)))
    t = jnp.arange(T, dtype=F32)
    gap = t[:, None] - t[None, :]
    dm = jnp.where(gap >= 0, jnp.exp(jnp.maximum(gap, 0.0)[None] * log_gamma[:, None, None]), 0.0)
    eye = jnp.eye(N_HEADS, dtype=F32)
    dmask = (dm[:, :, None, :] * eye[:, None, :, None]).reshape(N_HEADS * T, N_HEADS * T)
    lanes = lambda col: jnp.broadcast_to(col.reshape(-1, 1), (col.size, GROUP_W))
    qdec = jnp.exp((t + 1.0)[None, :] * log_gamma[:, None])
    kdec = jnp.where(t[None, :] < n_valid, jnp.exp((n_valid - 1.0 - t)[None, :] * log_gamma[:, None]), 0.0)
    sdec = jnp.repeat(jnp.exp(n_valid * log_gamma), HEAD_DIM)
    return cos, sin, dmask, lanes(qdec), lanes(kdec), lanes(sdec)


def _ret_mix(p, col0, s0_bd, tables, B, L):
    T = RET_CHUNK
    assert L % T == 0
    nc = L // T
    HT = N_HEADS * T
    blk = lambda j: pl.BlockSpec((T, GROUP_W), lambda b, c: (b * nc + c, col0 + j))
    tab = pl.BlockSpec((T, GROUP_W), lambda b, c: (c, 0))
    fixed = lambda shape: pl.BlockSpec(shape, lambda b, c: (0,) * len(shape))
    state = lambda: pl.BlockSpec((1, GROUP_W, GROUP_W), lambda b, c: (b, 0, 0))
    return pl.pallas_call(
        _ret_body,
        grid=(B, nc),
        in_specs=[blk(0), blk(1), blk(2), blk(3), tab, tab, fixed((HT, HT)), fixed((HT, GROUP_W)),
                  fixed((HT, GROUP_W)), fixed((GROUP_W, GROUP_W)), state()],
        out_specs=[pl.BlockSpec((T, GROUP_W), lambda b, c: (b * nc + c, 0)), state()],
        out_shape=[jax.ShapeDtypeStruct((B * L, GROUP_W), F32), jax.ShapeDtypeStruct((B, GROUP_W, GROUP_W), F32)],
        scratch_shapes=[pltpu.VMEM((GROUP_W, GROUP_W), F32)],
        compiler_params=pltpu.CompilerParams(
            dimension_semantics=("arbitrary", "arbitrary"), vmem_limit_bytes=V7X_VMEM_LIMIT_BYTES),
        name="retention_mix",
    )(p, p, p, p, *tables, s0_bd)


def _layernorm(x, g, b):
    xf = _f32(x)
    mu = jnp.mean(xf, -1, keepdims=True)
    xc = xf - mu
    var = jnp.mean(xc * xc, -1, keepdims=True)
    return (xc * lax.rsqrt(var + LN_EPS) * _f32(g) + _f32(b)).astype(x.dtype)


def _rms(x, eps=NORM_EPS):
    return x * lax.rsqrt(jnp.mean(x * x, -1, keepdims=True) + eps)


def _head_ln(x, eps):
    xc = x - jnp.mean(x, -1, keepdims=True)
    return xc * lax.rsqrt(jnp.mean(xc * xc, -1, keepdims=True) + eps)


def _rotary(x, pos):
    half = x.shape[-1] // 2
    freq = 1.0 / (ROPE_BASE ** jnp.linspace(0.0, 1.0, half, dtype=jnp.float32))
    ang = _f32(pos)[:, None] * freq[None, :]
    cos = jnp.cos(ang)[None, :, None, :]
    sin = jnp.sin(ang)[None, :, None, :]
    x1, x2 = x[..., :half], x[..., half:]
    return jnp.concatenate([x1 * cos - x2 * sin, x1 * sin + x2 * cos], -1)


def _chunk_scan(step, xs, s0):
    B, L = xs[0].shape[:2]
    c = CHUNK if L % CHUNK == 0 else L
    n = L // c
    chunked = tuple(jnp.moveaxis(a.reshape(B, n, c, *a.shape[2:]), 1, 0) for a in xs)
    s, out = lax.scan(step, s0, chunked)
    return jnp.moveaxis(out, 0, 1).reshape(B, L, *out.shape[3:]), s


def _retention_step(S, xs, log_gamma):
    q, k, v = xs
    C = q.shape[1]
    t = jnp.arange(C, dtype=jnp.float32)
    gap = t[:, None] - t[None, :]
    dmask = jnp.where(gap >= 0, jnp.exp(jnp.maximum(gap, 0.0)[None] * log_gamma[:, None, None]), 0.0)
    scores = jnp.einsum('bqhd,bkhd->bhqk', q, k) * dmask[None]
    o = jnp.einsum('bhqk,bkhe->bqhe', scores, v)
    o = o + jnp.einsum('bqhd,bhde->bqhe', q, S) * jnp.exp((t + 1.0)[:, None] * log_gamma[None, :])[None, :, :, None]
    k_w = k * jnp.exp((C - 1.0 - t)[:, None] * log_gamma[None, :])[None, :, :, None]
    S = jnp.exp(C * log_gamma)[None, :, None, None] * S + jnp.einsum('bkhd,bkhe->bhde', k_w, v)
    return S, o


def _hgrn2_step(S, xs):
    q, k, logf, v = xs
    C = q.shape[1]
    b = jnp.cumsum(logf, axis=1)
    causal = jnp.tril(jnp.ones((C, C), bool))[None, :, :, None, None]
    diff = jnp.where(causal, b[:, :, None] - b[:, None, :], 0.0)
    decay = jnp.where(causal, jnp.exp(diff), 0.0)
    attn = jnp.einsum('bqhd,bqkhd,bkhd->bhqk', q, decay, k)
    o = jnp.einsum('bhqk,bkhe->bqhe', attn, v) + jnp.einsum('bqhd,bhde->bqhe', q * jnp.exp(b), S)
    b_last = b[:, -1]
    S = jnp.exp(b_last)[..., None] * S + jnp.einsum('bkhd,bkhe->bhde', k * jnp.exp(b_last[:, None] - b), v)
    return S, o


def _rwkv_scan(S0, r, w, k, v, a, b):
    def step(S, xs):
        r_t, w_t, k_t, v_t, a_t, b_t = xs
        S = (S * w_t[:, :, None, :]
             + jnp.einsum('bhvk,bhk->bhv', S, a_t)[..., None] * b_t[:, :, None, :]
             + v_t[..., None] * k_t[:, :, None, :])
        return S, jnp.einsum('bhvk,bhk->bhv', S, r_t)
    xs = tuple(jnp.moveaxis(t, 1, 0) for t in (r, w, k, v, a, b))
    S, y = lax.scan(step, S0, xs)
    return jnp.moveaxis(y, 0, 1), S


def _rwkv7(p_rw, prev_row, S0, l, W):
    B, L, _ = p_rw.shape
    p = _f32(p_rw)
    prev = jnp.concatenate([_f32(prev_row)[:, None], p[:, :-1]], axis=1)
    xs = p + (prev - p) * _f32(W['rw_mu'][l])
    r, k, v, wd, ad, gd = jnp.split(xs, RW_SPLITS, axis=-1)
    w_log = -jax.nn.softplus(-(_f32(W['rw_w0'][l]) + jnp.tanh(wd) @ _f32(W['rw_w_up'][l]))) - 0.5
    decay = jnp.exp(-jnp.exp(w_log))
    a = jax.nn.sigmoid(_f32(W['rw_a0'][l]) + ad @ _f32(W['rw_a_up'][l]))
    g = jax.nn.sigmoid(gd) @ _f32(W['rw_g_up'][l])
    hd = lambda t: t.reshape(B, L, N_HEADS, HEAD_DIM)
    kk = hd(k * _f32(W['rw_k_k'][l]))
    kk = kk / jnp.maximum(jnp.sqrt(jnp.sum(kk * kk, -1, keepdims=True)), 1e-12)
    k = hd(k * (1.0 + (a - 1.0) * _f32(W['rw_k_a'][l])))
    r, v, a_h = hd(r), hd(v), hd(a)
    y, S = _rwkv_scan(_f32(S0), r, hd(decay), k, v, -kk, kk * a_h)
    y = _head_ln(y, RW_LNX_EPS).reshape(B, L, GROUP_W) * _f32(W['rw_lnx_w'][l]) + _f32(W['rw_lnx_b'][l])
    bonus = (jnp.sum(r * k * _f32(W['rw_r_k'][l]), -1, keepdims=True) * v).reshape(B, L, GROUP_W)
    return (y + bonus) * g, S, p_rw[:, -1]


def _diff_attn_prompt(q, k, v, lam):
    B, S = q.shape[:2]
    qb_sz = Q_BLOCK if S % Q_BLOCK == 0 else S
    nb = S // qb_sz
    qb = jnp.moveaxis(q.reshape(B, nb, qb_sz, *q.shape[2:]), 1, 0)
    kpos = jnp.arange(S)
    scale = DA_DQK ** -0.5

    def block(args):
        qi, i = args
        s = jnp.einsum('bqhmd,bkhmd->bhmqk', qi, k) * scale
        qpos = i * qb_sz + jnp.arange(qb_sz)
        s = jnp.where(kpos[None, :] <= qpos[:, None], s, MASK_NEG)
        p = jax.nn.softmax(s, axis=-1)
        return jnp.einsum('bhqk,bkhe->bqhe', p[:, :, 0] - lam * p[:, :, 1], v)

    o = lax.map(block, (qb, jnp.arange(nb)))
    return jnp.moveaxis(o, 0, 1).reshape(B, S, N_HEADS, HEAD_DIM)


def _diff_attn_sample(q, k, v, k_past, v_past, lam):
    L = q.shape[1]
    P = k_past.shape[1]
    scale = DA_DQK ** -0.5
    s_past = jnp.einsum('bqhmd,bkhmd->bhmqk', q, k_past) * scale
    s_new = jnp.einsum('bqhmd,bkhmd->bhmqk', q, k) * scale
    s_new = jnp.where(jnp.tril(jnp.ones((L, L), bool)), s_new, MASK_NEG)
    p = jax.nn.softmax(jnp.concatenate([_f32(s_past), s_new], -1), axis=-1)
    pd = p[:, :, 0] - lam * p[:, :, 1]
    return (jnp.einsum('bhqk,bkhe->bqhe', pd[..., :P], v_past)
            + jnp.einsum('bhqk,bkhe->bqhe', pd[..., P:], v))


def _swiglu(x, wg, wu, wd):
    return (jax.nn.silu(x @ wg) * (x @ wu)) @ wd


def _grouped_experts(xf, idx, wts, wg, wu, wd):
    M, D = xf.shape
    A = M * TOP_K
    flat_e = idx.reshape(-1)
    order = jnp.argsort(flat_e)
    e_sorted = flat_e[order]
    tok_sorted = (order // TOP_K).astype(jnp.int32)
    w_sorted = wts.reshape(-1)[order]
    counts = jnp.bincount(flat_e, length=N_EXPERTS)
    padded = (counts + MOE_BLOCK - 1) // MOE_BLOCK * MOE_BLOCK
    pad_end = jnp.cumsum(padded)
    pad_start = pad_end - padded
    start = jnp.cumsum(counts) - counts
    dest = pad_start[e_sorted] + jnp.arange(A) - start[e_sorted]
    n_blocks = -(-A // MOE_BLOCK) + N_EXPERTS
    P = n_blocks * MOE_BLOCK
    row_tok = jnp.zeros((P,), jnp.int32).at[dest].set(tok_sorted)
    row_w = jnp.zeros((P,), jnp.float32).at[dest].set(w_sorted)
    blk_e = jnp.minimum(jnp.searchsorted(pad_end, jnp.arange(n_blocks) * MOE_BLOCK, side='right'), N_EXPERTS - 1)

    def body(acc, args):
        toks, rw, e = args
        yb = _swiglu(xf[toks], wg[e], wu[e], wd[e])
        return acc.at[toks].add(_f32(yb) * rw[:, None]), None

    acc, _ = lax.scan(body, jnp.zeros((M, D), jnp.float32),
                      (row_tok.reshape(n_blocks, MOE_BLOCK), row_w.reshape(n_blocks, MOE_BLOCK), blk_e))
    return acc.astype(xf.dtype)


def _moe(x, l, W):
    B, L, D = x.shape
    xf = x.reshape(-1, D)
    scores = jax.nn.sigmoid(_f32(xf @ W['router_w'][l]))
    _, idx = lax.top_k(scores + _f32(W['router_bias'][l]), TOP_K)
    wts = jnp.take_along_axis(scores, idx, axis=-1)
    wts = wts / jnp.sum(wts, -1, keepdims=True) * ROUTED_SCALE
    routed = _grouped_experts(xf, idx, wts, W['e_gate'][l], W['e_up'][l], W['e_down'][l])
    shared = _swiglu(xf, W['sh_gate'][l], W['sh_up'][l], W['sh_down'][l])
    return (routed + shared).reshape(B, L, D)


def _layer(l, x, pos, ret_s0, hg_s0, rw_s0, shift0, kv_past, W):
    B, L, _ = x.shape
    dt = x.dtype
    M = B * L
    proj2d, d_k, d_v = _in_proj(x.reshape(M, D_MODEL), W['w_in_bf16'][l], 256)
    proj = proj2d.reshape(B, L, N_COLS)
    rw_wts = _rwkv_weights(W, l)
    shift3 = _f32(shift0)[:, None, :]
    lb_rows = _hg_lb_rows(W['hg_lb'], l)
    hg_norm = jnp.tile(_f32(W['hg_norm_w'][l]), N_HEADS)
    ret_bd0, hg_bd0, rw_bd0 = _state_to_bd(_f32(ret_s0)), _hg_state_to_bd(_f32(hg_s0)), _state_to_bd(_f32(rw_s0))
    p_rw = proj[:, :, W_RET + W_HG:W_RET + W_HG + W_RW]

    lam_init = 0.8 - 0.6 * math.exp(-0.3 * l)
    lam = (jnp.exp(jnp.sum(_f32(W['da_lq1'][l]) * _f32(W['da_lk1'][l])))
           - jnp.exp(jnp.sum(_f32(W['da_lq2'][l]) * _f32(W['da_lk2'][l]))) + lam_init)
    c_da = W_RET + W_HG + W_RW
    if kv_past is None:
        assert L % HG_BLOCK == 0
        o_a, ret_bd = _ret_mix(proj2d, 0, ret_bd0, _ret_tables(pos, RET_CHUNK), B, L)
        o_b, hg_bd = _hgrn_mix(proj2d, W_RET, hg_bd0, lb_rows, hg_norm, l > 0, HG_BLOCK, B, L)
        o_c, rw_bd = _rwkv_mix(proj2d, RW_COL_BLOCK, shift3, rw_bd0, RW_CHUNK, B, L, rw_wts)
        ret_s, hg_s, rw_s = _bd_to_state(ret_bd), _hg_bd_to_state(hg_bd), _bd_to_state(rw_bd)
        norm_scale = jnp.tile(_f32(W['da_norm_w'][l]) * (1.0 - lam_init), N_HEADS)
        o_d = _attn_prompt(proj2d, lam, norm_scale, B, L)
    else:
        heads = lambda t, d: t.reshape(B, L, N_HEADS, d)
        r_q, r_k, r_v, r_g = jnp.split(proj[:, :, :W_RET], 4, axis=-1)
        log_gamma = jnp.log1p(-jnp.exp2(-5.0 - jnp.arange(N_HEADS, dtype=jnp.float32)))
        q = _rotary(heads(r_q, HEAD_DIM), pos)
        k = _rotary(heads(r_k, HEAD_DIM), pos) * HEAD_DIM ** -0.5
        o, ret_s = _chunk_scan(functools.partial(_retention_step, log_gamma=log_gamma),
                               (q, k, heads(r_v, HEAD_DIM)), _f32(ret_s0))
        o_a = (_rms(o).reshape(B, L, GROUP_W) * jax.nn.silu(r_g)).reshape(M, GROUP_W)

        h_q, h_f, h_i, h_g = jnp.split(proj[:, :, W_RET:W_RET + W_HG],
                                       [HG_QW, 2 * HG_QW, 2 * HG_QW + GROUP_W], axis=-1)
        if l == 0:
            log_f = jax.nn.log_sigmoid(h_f)
        else:
            log_f = jnp.logaddexp(lb_rows[0], lb_rows[1] + jax.nn.log_sigmoid(h_f))
        k_in = (lb_rows[2] if l > 0 else 1.0) * jax.nn.sigmoid(-h_f)
        q_h = jax.nn.silu(h_q) * HG_DK ** -0.5
        o, hg_s = _chunk_scan(_hgrn2_step, (heads(q_h, HG_DK), heads(k_in, HG_DK), heads(log_f, HG_DK),
                                            heads(h_i, HEAD_DIM)), _f32(hg_s0))
        o_b = ((_rms(o) * _f32(W['hg_norm_w'][l])).reshape(B, L, GROUP_W) * jax.nn.silu(h_g)).reshape(M, GROUP_W)

        o_c, rw_s, _ = _rwkv7(p_rw, shift0, rw_s0, l, W)
        o_c = o_c.reshape(M, GROUP_W)

        k_past, v_past = kv_past
        q5 = proj[:, :, c_da:c_da + GROUP_W].reshape(B, L, N_HEADS, 2, DA_DQK)
        o = _diff_attn_sample(q5, d_k.reshape(B, L, N_HEADS, 2, DA_DQK), d_v.reshape(B, L, N_HEADS, HEAD_DIM),
                              k_past.reshape(B, -1, N_HEADS, 2, DA_DQK), v_past, lam)
        o_d = (_rms(o) * _f32(W['da_norm_w'][l]) * (1.0 - lam_init)).reshape(M, GROUP_W)
    shift_new = p_rw[:, -1]

    x1 = _out_proj_ln((o_a, o_b, o_c, o_d), W['w_o_bf16'][l], x.reshape(M, D_MODEL),
                      W['ln1_g'][l], W['ln1_b'][l], 256)
    x = _moe_ln(x1, l, W).reshape(B, L, D_MODEL)
    new = (ret_s.astype(dt), hg_s.astype(dt), rw_s.astype(dt), shift_new,
           d_k.reshape(B, L, N_HEADS, HEAD_DIM), d_v.reshape(B, L, N_HEADS, HEAD_DIM))
    return x, new


def _prepare_weights(W):
    W = dict(W)
    depth = W['w_in'].shape[0]
    W['w_in_bf16'] = W['w_in'].astype(BF16)
    W['w_o_bf16'] = W['w_o'].astype(BF16)
    router_pad = jnp.zeros((depth, D_MODEL, 128 - N_EXPERTS), W['router_w'].dtype)
    W['gu_router_bf16'] = jnp.concatenate([W['sh_gate'], W['sh_up'], W['router_w'], router_pad], -1).astype(BF16)
    for name in ('e_gate', 'e_up', 'e_down', 'sh_down'):
        W[name + '_bf16'] = W[name].astype(BF16)
    return W


def kernel(x_prompt, x_sample, state_ret, state_hgrn, state_rwkv, state_rwkv_shift, cache_k, cache_v,
           page_table, w_in, w_o, hg_lb, hg_norm_w, rw_mu, rw_w0, rw_w_up, rw_a0, rw_a_up, rw_g_up,
           rw_k_k, rw_k_a, rw_r_k, rw_lnx_w, rw_lnx_b, da_lq1, da_lk1, da_lq2, da_lk2, da_norm_w,
           ln1_g, ln1_b, router_w, router_bias, e_gate, e_up, e_down, sh_gate, sh_up, sh_down,
           ln2_g, ln2_b):
    W = {'w_in': w_in, 'w_o': w_o, 'hg_lb': hg_lb, 'hg_norm_w': hg_norm_w, 'rw_mu': rw_mu,
         'rw_w0': rw_w0, 'rw_w_up': rw_w_up, 'rw_a0': rw_a0, 'rw_a_up': rw_a_up, 'rw_g_up': rw_g_up,
         'rw_k_k': rw_k_k, 'rw_k_a': rw_k_a, 'rw_r_k': rw_r_k, 'rw_lnx_w': rw_lnx_w, 'rw_lnx_b': rw_lnx_b,
         'da_lq1': da_lq1, 'da_lk1': da_lk1, 'da_lq2': da_lq2, 'da_lk2': da_lk2, 'da_norm_w': da_norm_w,
         'ln1_g': ln1_g, 'ln1_b': ln1_b, 'router_w': router_w, 'router_bias': router_bias,
         'e_gate': e_gate, 'e_up': e_up, 'e_down': e_down, 'sh_gate': sh_gate, 'sh_up': sh_up,
         'sh_down': sh_down, 'ln2_g': ln2_g, 'ln2_b': ln2_b}
    W = _prepare_weights(W)
    B, S, _ = x_prompt.shape
    DB, L, _ = x_sample.shape
    past_len = page_table.shape[1] * PAGE_SIZE
    pos_p = jnp.arange(S)
    pos_s = past_len + jnp.arange(L)
    zero_ret = jnp.zeros((B, N_HEADS, HEAD_DIM, HEAD_DIM), jnp.float32)
    zero_hg = jnp.zeros((B, N_HEADS, HG_DK, HEAD_DIM), jnp.float32)
    zero_shift = jnp.zeros((B, W_RW), x_prompt.dtype)
    yp, ys = x_prompt, x_sample
    new_p, new_s = [], []
    for l in range(DEPTH):
        yp, st = _layer(l, yp, pos_p, zero_ret, zero_hg, zero_ret, zero_shift, None, W)
        new_p.append(st)
        k_past = cache_k[l][page_table].reshape(DB, past_len, N_HEADS, HEAD_DIM)
        v_past = cache_v[l][page_table].reshape(DB, past_len, N_HEADS, HEAD_DIM)
        ys, st = _layer(l, ys, pos_s, state_ret[l], state_hgrn[l], state_rwkv[l], state_rwkv_shift[l],
                        (k_past, v_past), W)
        new_s.append(st)

    def stk(sts, i):
        return jnp.stack([s[i] for s in sts])

    return (yp, ys, stk(new_p, 0), stk(new_s, 0), stk(new_p, 1), stk(new_s, 1), stk(new_p, 2), stk(new_s, 2),
            stk(new_p, 3), stk(new_s, 3), stk(new_p, 4), stk(new_p, 5), stk(new_s, 4), stk(new_s, 5))
```

```python
import math, functools
import jax, jax.numpy as jnp
from jax import lax
from jax.experimental import pallas as pl
from jax.experimental.pallas import tpu as pltpu

D_MODEL = 1024
DEPTH = 2
PAGE_SIZE = 128
N_MIXERS = 4
GROUP_W = D_MODEL // N_MIXERS
HEAD_DIM = 64
N_HEADS = GROUP_W // HEAD_DIM
ROPE_BASE = 10000.0
HG_DK = 128
RW_DECAY_LORA = 64
RW_A_LORA = 64
RW_GATE_LORA = 128
RW_LNX_EPS = 6.4e-4
DA_DQK = HEAD_DIM // 2
W_RET = 3 * N_HEADS * HEAD_DIM + GROUP_W
W_HG = 2 * N_HEADS * HG_DK + N_HEADS * HEAD_DIM + GROUP_W
W_RW = 3 * GROUP_W + RW_DECAY_LORA + RW_A_LORA + RW_GATE_LORA
W_DA = 2 * (N_HEADS * 2 * DA_DQK) + N_HEADS * HEAD_DIM
N_COLS = W_RET + W_HG + W_RW + W_DA
RW_SPLITS = [GROUP_W, 2 * GROUP_W, 3 * GROUP_W, 3 * GROUP_W + RW_DECAY_LORA, 3 * GROUP_W + RW_DECAY_LORA + RW_A_LORA]
CHUNK = 64
Q_BLOCK = 128
MASK_NEG = -1e30
N_EXPERTS = 64
TOP_K = 8
D_EXPERT = 256
ROUTED_SCALE = 2.5
MOE_BLOCK = 128
ALPHA = (2.0 * DEPTH) ** 0.25
LN_EPS = 1e-5
NORM_EPS = 1e-5

V7X_VMEM_LIMIT_BYTES = 56 * 1024 * 1024


def _f32(t):
    return t.astype(jnp.float32)


def _matmul_body(x_ref, w_ref, o_ref):
    o_ref[...] = jnp.dot(x_ref[...].astype(jnp.bfloat16), w_ref[...],
                         preferred_element_type=jnp.float32)


def _matmul(x, w_bf16, tm):
    M, K = x.shape
    N = w_bf16.shape[1]
    tm = min(tm, M)
    assert M % tm == 0
    return pl.pallas_call(
        _matmul_body,
        grid=(M // tm,),
        in_specs=[pl.BlockSpec((tm, K), lambda i: (i, 0)),
                  pl.BlockSpec((K, N), lambda i: (0, 0))],
        out_specs=pl.BlockSpec((tm, N), lambda i: (i, 0)),
        out_shape=jax.ShapeDtypeStruct((M, N), jnp.float32),
        compiler_params=pltpu.CompilerParams(
            dimension_semantics=("arbitrary",), vmem_limit_bytes=V7X_VMEM_LIMIT_BYTES),
        name="matmul",
    )(x, w_bf16)


def _in_proj_body(x_ref, w_ref, o_ref, k_ref, v_ref):
    acc = jnp.dot(x_ref[...].astype(jnp.bfloat16), w_ref[...], preferred_element_type=jnp.float32)
    o_ref[...] = acc
    k0, v0 = N_COLS - 2 * GROUP_W, N_COLS - GROUP_W
    for h in range(N_HEADS):
        k_ref[:, h, :] = acc[:, k0 + h * HEAD_DIM:k0 + (h + 1) * HEAD_DIM]
        v_ref[:, h, :] = acc[:, v0 + h * HEAD_DIM:v0 + (h + 1) * HEAD_DIM]


def _in_proj(x, w_bf16, tm):
    M, K = x.shape
    tm = min(tm, M)
    assert M % tm == 0
    row = lambda i: (i, 0)
    row3 = lambda i: (i, 0, 0)
    return pl.pallas_call(
        _in_proj_body,
        grid=(M // tm,),
        in_specs=[pl.BlockSpec((tm, K), row), pl.BlockSpec((K, N_COLS), lambda i: (0, 0))],
        out_specs=[pl.BlockSpec((tm, N_COLS), row), pl.BlockSpec((tm, N_HEADS, HEAD_DIM), row3),
                   pl.BlockSpec((tm, N_HEADS, HEAD_DIM), row3)],
        out_shape=[jax.ShapeDtypeStruct((M, N_COLS), jnp.float32),
                   jax.ShapeDtypeStruct((M, N_HEADS, HEAD_DIM), jnp.float32),
                   jax.ShapeDtypeStruct((M, N_HEADS, HEAD_DIM), jnp.float32)],
        compiler_params=pltpu.CompilerParams(
            dimension_semantics=("arbitrary",), vmem_limit_bytes=V7X_VMEM_LIMIT_BYTES),
        name="in_proj",
    )(x, w_bf16)


def _ln_rows(z, g, b):
    mu = jnp.mean(z, -1, keepdims=True)
    zc = z - mu
    var = jnp.mean(zc * zc, -1, keepdims=True)
    return zc * lax.rsqrt(var + LN_EPS) * g + b


def _out_proj_ln_body(a_ref, b_ref, c_ref, d_ref, w_ref, res_ref, g_ref, beta_ref, o_ref):
    mix = None
    for j, part in enumerate((a_ref, b_ref, c_ref, d_ref)):
        term = jnp.dot(part[...].astype(jnp.bfloat16), w_ref[j * GROUP_W:(j + 1) * GROUP_W, :],
                       preferred_element_type=jnp.float32)
        mix = term if mix is None else mix + term
    o_ref[...] = _ln_rows(ALPHA * res_ref[...] + mix, g_ref[...], beta_ref[...])


def _out_proj_ln(parts, w_bf16, res, g, b, tm):
    M, N = res.shape
    tm = min(tm, M)
    assert M % tm == 0
    row = lambda i: (i, 0)
    fixed = lambda i: (0, 0)
    part = pl.BlockSpec((tm, GROUP_W), row)
    return pl.pallas_call(
        _out_proj_ln_body,
        grid=(M // tm,),
        in_specs=[part, part, part, part, pl.BlockSpec((N, N), fixed), pl.BlockSpec((tm, N), row),
                  pl.BlockSpec((1, N), fixed), pl.BlockSpec((1, N), fixed)],
        out_specs=pl.BlockSpec((tm, N), row),
        out_shape=jax.ShapeDtypeStruct((M, N), jnp.float32),
        compiler_params=pltpu.CompilerParams(
            dimension_semantics=("arbitrary",), vmem_limit_bytes=V7X_VMEM_LIMIT_BYTES),
        name="out_proj_ln",
    )(*parts, w_bf16, res, g.reshape(1, N), b.reshape(1, N))


EXPERT_ROWS = 256


def _start_row_gather(idx_ref, n_rows, src_hbm, dst_at, sem):
    def body(pair, carry):
        for priority in range(2):
            r = pair * 2 + priority
            t = idx_ref[0, 0, r]
            pltpu.make_async_copy(src_hbm.at[pl.ds(t, 1)], dst_at(r), sem).start(priority=priority)
        return carry
    lax.fori_loop(0, n_rows // 2, body, 0, unroll=4)


def _moe_experts_body(n_blocks, blk_ref, exp_ref, lo_ref, hi_ref, first_ref, tok_ref, tok_next_ref, x_hbm,
                      wg_ref, wu_ref, wd_ref, y_ref, xbuf, sem):
    w = pl.program_id(0)
    blk = blk_ref[w]
    slot = blk % 2
    lo = lo_ref[w]
    hi = hi_ref[w]

    def gather(idx_ref, s):
        _start_row_gather(idx_ref, EXPERT_ROWS, x_hbm, lambda r: xbuf.at[s, pl.ds(r, 1)], sem.at[s])

    @pl.when(w == 0)
    def _():
        gather(tok_ref, 0)

    @pl.when(first_ref[w] == 1)
    def _():
        @pl.when(blk + 1 < n_blocks)
        def _():
            gather(tok_next_ref, 1 - slot)
        pltpu.make_async_copy(x_hbm.at[pl.ds(0, EXPERT_ROWS)], xbuf.at[slot], sem.at[slot]).wait()
        y_ref[...] = jnp.zeros_like(y_ref)

    @pl.when(hi > lo)
    def _():
        x = xbuf[slot].astype(jnp.bfloat16)
        g = jnp.dot(x, wg_ref[0], preferred_element_type=jnp.float32)
        u = jnp.dot(x, wu_ref[0], preferred_element_type=jnp.float32)
        h = (g * jax.nn.sigmoid(g) * u).astype(jnp.bfloat16)
        y = jnp.dot(h, wd_ref[0], preferred_element_type=jnp.float32)
        r = lax.broadcasted_iota(jnp.int32, (EXPERT_ROWS, 1), 0)
        y_ref[...] += jnp.where((r >= lo) & (r < hi), y, 0.0)


def _moe_experts(x, tok_sorted, items, wg, wu, wd):
    M, D = x.shape
    A = tok_sorted.shape[0]
    assert A % EXPERT_ROWS == 0
    n_blocks = A // EXPERT_ROWS
    n_items = items[0].shape[0]
    tok3 = tok_sorted.reshape(n_blocks, 1, EXPERT_ROWS)
    smem_blk = lambda f: pl.BlockSpec((1, 1, EXPERT_ROWS), f, memory_space=pltpu.SMEM)
    w_spec = lambda shape: pl.BlockSpec(shape, lambda w, blk, e, *_: (e[w], 0, 0))
    grid_spec = pltpu.PrefetchScalarGridSpec(
        num_scalar_prefetch=5,
        grid=(n_items,),
        in_specs=[smem_blk(lambda w, blk, *_: (blk[w], 0, 0)),
                  smem_blk(lambda w, blk, *_: (jnp.minimum(blk[w] + 1, n_blocks - 1), 0, 0)),
                  pl.BlockSpec(memory_space=pl.ANY),
                  w_spec((1, D, D_EXPERT)), w_spec((1, D, D_EXPERT)), w_spec((1, D_EXPERT, D))],
        out_specs=pl.BlockSpec((EXPERT_ROWS, D), lambda w, blk, *_: (blk[w], 0)),
        scratch_shapes=[pltpu.VMEM((2, EXPERT_ROWS, D), jnp.float32), pltpu.SemaphoreType.DMA((2,))],
    )
    return pl.pallas_call(
        functools.partial(_moe_experts_body, n_blocks),
        grid_spec=grid_spec,
        out_shape=jax.ShapeDtypeStruct((A, D), jnp.float32),
        compiler_params=pltpu.CompilerParams(
            dimension_semantics=("arbitrary",), vmem_limit_bytes=V7X_VMEM_LIMIT_BYTES),
        name="moe_experts",
    )(*items, tok3, tok3, x, wg, wu, wd)


def _moe_work_items(e_sorted, A):
    n_blocks = A // EXPERT_ROWS
    group_end = jnp.searchsorted(e_sorted, jnp.arange(1, N_EXPERTS + 1, dtype=e_sorted.dtype)).astype(jnp.int32)
    starts = jnp.sort(jnp.concatenate([jnp.arange(n_blocks, dtype=jnp.int32) * EXPERT_ROWS, group_end]))
    ends = jnp.concatenate([starts[1:], jnp.full((1,), A, jnp.int32)])
    blk = jnp.minimum(starts // EXPERT_ROWS, n_blocks - 1)
    expert = jnp.minimum(jnp.searchsorted(group_end, starts, side='right'), N_EXPERTS - 1).astype(jnp.int32)
    new_start = jnp.concatenate([jnp.ones((1,), bool), starts[1:] != starts[:-1]])
    first = ((starts % EXPERT_ROWS == 0) & (starts < A) & new_start).astype(jnp.int32)
    base = blk * EXPERT_ROWS
    return blk, expert, starts - base, ends - base, first


COMBINE_ROWS = 128


def _moe_combine_body(pos_ref, pos_next_ref, wts_ref, x_ref, gu_ref, shd_ref, g_ref, b_ref, y_hbm,
                      o_ref, buf, sem):
    i = pl.program_id(0)
    n = pl.num_programs(0)
    slot = i % 2
    n_rows = COMBINE_ROWS * TOP_K

    def gather(idx_ref, s):
        def dst(r):
            k = jnp.bitwise_and(r, TOP_K - 1)
            token = lax.shift_right_logical(r, TOP_K.bit_length() - 1)
            return buf.at[s, pl.ds(k * COMBINE_ROWS + token, 1)]
        _start_row_gather(idx_ref, n_rows, y_hbm, dst, sem.at[s])

    @pl.when(i == 0)
    def _():
        gather(pos_ref, 0)

    @pl.when(i + 1 < n)
    def _():
        gather(pos_next_ref, 1 - slot)

    pltpu.make_async_copy(y_hbm.at[pl.ds(0, n_rows)], buf.at[slot], sem.at[slot]).wait()
    wts = wts_ref[...]
    routed = wts[:, 0:1] * buf[slot, pl.ds(0, COMBINE_ROWS)]
    for k in range(1, TOP_K):
        routed = routed + wts[:, k:k + 1] * buf[slot, pl.ds(k * COMBINE_ROWS, COMBINE_ROWS)]
    gu = gu_ref[...]
    g = gu[:, :D_EXPERT]
    u = gu[:, D_EXPERT:]
    h = (g * jax.nn.sigmoid(g) * u).astype(jnp.bfloat16)
    shared = jnp.dot(h, shd_ref[...], preferred_element_type=jnp.float32)
    o_ref[...] = _ln_rows(ALPHA * x_ref[...] + (routed + shared), g_ref[...], b_ref[...])


def _moe_combine(y_sorted, inv_pos, wts, x, gu, shd_bf16, g, b):
    M, D = x.shape
    tt = min(COMBINE_ROWS, M)
    assert tt == COMBINE_ROWS and M % tt == 0
    n_tiles = M // tt
    pos3 = inv_pos.reshape(n_tiles, 1, tt * TOP_K)
    row = lambda i: (i, 0)
    fixed = lambda i: (0, 0)
    smem_blk = lambda f: pl.BlockSpec((1, 1, tt * TOP_K), f, memory_space=pltpu.SMEM)
    return pl.pallas_call(
        _moe_combine_body,
        grid=(n_tiles,),
        in_specs=[smem_blk(lambda i: (i, 0, 0)),
                  smem_blk(lambda i: (jnp.minimum(i + 1, n_tiles - 1), 0, 0)),
                  pl.BlockSpec((tt, TOP_K), row),
                  pl.BlockSpec((tt, D), row),
                  pl.BlockSpec((tt, 2 * D_EXPERT), row),
                  pl.BlockSpec((D_EXPERT, D), fixed),
                  pl.BlockSpec((1, D), fixed), pl.BlockSpec((1, D), fixed),
                  pl.BlockSpec(memory_space=pl.ANY)],
        out_specs=pl.BlockSpec((tt, D), row),
        out_shape=jax.ShapeDtypeStruct((M, D), jnp.float32),
        scratch_shapes=[pltpu.VMEM((2, tt * TOP_K, D), jnp.float32), pltpu.SemaphoreType.DMA((2,))],
        compiler_params=pltpu.CompilerParams(
            dimension_semantics=("arbitrary",), vmem_limit_bytes=V7X_VMEM_LIMIT_BYTES),
        name="moe_combine_ln",
    )(pos3, pos3, wts, x, gu, shd_bf16, g.reshape(1, D), b.reshape(1, D), y_sorted)


def _moe_ln(x, l, W):
    M, D = x.shape
    rgu = _matmul(x, W['gu_router_bf16'][l], 256)
    scores = jax.nn.sigmoid(rgu[:, 2 * D_EXPERT:2 * D_EXPERT + N_EXPERTS])
    _, idx = lax.top_k(scores + _f32(W['router_bias'][l]), TOP_K)
    wts = jnp.take_along_axis(scores, idx, axis=-1)
    wts = wts / jnp.sum(wts, -1, keepdims=True) * ROUTED_SCALE

    A = M * TOP_K
    flat_e = idx.reshape(-1).astype(jnp.int32)
    assign = jnp.arange(A, dtype=jnp.int32)
    e_sorted, order = lax.sort_key_val(flat_e, assign)
    _, inv_pos = lax.sort_key_val(order, assign)
    tok_sorted = lax.shift_right_logical(order, TOP_K.bit_length() - 1)
    items = _moe_work_items(e_sorted, A)

    y_sorted = _moe_experts(x, tok_sorted, items, W['e_gate_bf16'][l], W['e_up_bf16'][l], W['e_down_bf16'][l])
    return _moe_combine(y_sorted, inv_pos, wts, x, rgu, W['sh_down_bf16'][l], W['ln2_g'][l], W['ln2_b'][l])


DA_COL_BLOCK = (W_RET + W_HG + W_RW) // GROUP_W
ATTN_BLOCK = 256
LOG2E = 1.4426950408889634
BF16 = jnp.bfloat16
F32 = jnp.float32


ACC_W = 128


def _attn_prompt_body(lam_ref, q_ref, k_ref, v_ref, nw_ref, o_ref, kbf, vh, qs, m_scr, acc_scr):
    i = pl.program_id(1)
    T = ATTN_BLOCK
    n_hm = 2 * N_HEADS
    S = k_ref.shape[0]

    @pl.when(i == 0)
    def _():
        kbf[...] = k_ref[...].astype(BF16)
        v = v_ref[...]
        ones_col = (lax.broadcasted_iota(jnp.int32, (S, ACC_W - HEAD_DIM), 1) == 0).astype(F32)
        for h in range(N_HEADS):
            vh[h] = jnp.concatenate([v[:, h * HEAD_DIM:(h + 1) * HEAD_DIM], ones_col], -1).astype(BF16)

    q = q_ref[...]
    lane = lax.broadcasted_iota(jnp.int32, (T, GROUP_W), 1)
    for hm in range(n_hm):
        keep = (lane >= hm * DA_DQK) & (lane < (hm + 1) * DA_DQK)
        qs[pl.ds(hm * T, T), :] = jnp.where(keep, q, 0.0).astype(BF16)
    m_scr[...] = jnp.full_like(m_scr, MASK_NEG)
    acc_scr[...] = jnp.zeros_like(acc_scr)
    score_scale = DA_DQK ** -0.5 * LOG2E

    def kv_block(j, masked):
        kb = kbf[pl.ds(j * T, T), :]
        for h in range(N_HEADS):
            rows = pl.ds(h * 2 * T, 2 * T)
            s = lax.dot_general(qs[rows, :], kb, (((1,), (1,)), ((), ())), preferred_element_type=F32)
            s = s * score_scale
            if masked:
                qpos = lax.broadcasted_iota(jnp.int32, (2 * T, T), 0) % T
                kpos = lax.broadcasted_iota(jnp.int32, (2 * T, T), 1)
                s = jnp.where(kpos <= qpos, s, MASK_NEG)
            m_old = m_scr[rows, :]
            m_new = jnp.maximum(m_old, jnp.max(s, -1, keepdims=True))
            alpha = jnp.exp2(m_old - m_new)
            p = jnp.exp2(s - jnp.concatenate([m_new] * (T // ACC_W), -1))
            pv = jnp.dot(p.astype(BF16), vh[h, pl.ds(j * T, T), :], preferred_element_type=F32)
            acc_scr[rows, :] = alpha * acc_scr[rows, :] + pv
            m_scr[rows, :] = m_new

    def body(j, c):
        kv_block(j, False)
        return c
    lax.fori_loop(0, i, body, 0)
    kv_block(i, True)

    lam = lam_ref[0]
    outs = []
    for h in range(N_HEADS):
        a0 = acc_scr[pl.ds(h * 2 * T, T), :]
        a1 = acc_scr[pl.ds(h * 2 * T + T, T), :]
        o = (a0[:, :HEAD_DIM] / a0[:, HEAD_DIM:HEAD_DIM + 1]
             - lam * (a1[:, :HEAD_DIM] / a1[:, HEAD_DIM:HEAD_DIM + 1]))
        outs.append(o * lax.rsqrt(jnp.mean(o * o, -1, keepdims=True) + NORM_EPS))
    o_ref[...] = jnp.concatenate(outs, -1) * nw_ref[...]


def _attn_prompt(proj, lam, norm_scale, B, S):
    T = ATTN_BLOCK
    assert S % T == 0
    nq = S // T
    n_rows = 2 * N_HEADS * T
    return pl.pallas_call(
        _attn_prompt_body,
        grid=(B, nq),
        in_specs=[pl.BlockSpec(memory_space=pltpu.SMEM),
                  pl.BlockSpec((T, GROUP_W), lambda b, i: (b * nq + i, DA_COL_BLOCK)),
                  pl.BlockSpec((S, GROUP_W), lambda b, i: (b, DA_COL_BLOCK + 1)),
                  pl.BlockSpec((S, GROUP_W), lambda b, i: (b, DA_COL_BLOCK + 2)),
                  pl.BlockSpec((1, GROUP_W), lambda b, i: (0, 0))],
        out_specs=pl.BlockSpec((T, GROUP_W), lambda b, i: (b * nq + i, 0)),
        out_shape=jax.ShapeDtypeStruct((B * S, GROUP_W), F32),
        scratch_shapes=[pltpu.VMEM((S, GROUP_W), BF16),
                        pltpu.VMEM((N_HEADS, S, ACC_W), BF16),
                        pltpu.VMEM((n_rows, GROUP_W), BF16),
                        pltpu.VMEM((n_rows, ACC_W), F32),
                        pltpu.VMEM((n_rows, ACC_W), F32)],
        compiler_params=pltpu.CompilerParams(
            dimension_semantics=("arbitrary", "arbitrary"), vmem_limit_bytes=V7X_VMEM_LIMIT_BYTES),
        name="diff_attn_prompt",
    )(lam.reshape(1), proj, proj, proj, norm_scale.reshape(1, GROUP_W))


RW_COL_BLOCK = (W_RET + W_HG) // GROUP_W
RW_CHUNK = 64


def _dot(a, b):
    return jnp.dot(a, b, preferred_element_type=F32)


def _dot_nt(a, b):
    return lax.dot_general(a, b, (((1,), (1,)), ((), ())), preferred_element_type=F32)


def _dot_tn(a, b):
    return lax.dot_general(a, b, (((0,), (0,)), ((), ())), preferred_element_type=F32)


def _split3(x):
    hi = x.astype(BF16)
    r1 = x - hi.astype(F32)
    mid = r1.astype(BF16)
    lo = (r1 - mid.astype(F32)).astype(BF16)
    return hi, mid, lo


def _dot_exact_rhs(x, m_bf16):
    hi, mid, lo = _split3(x)
    return _dot(hi, m_bf16) + _dot(mid, m_bf16) + _dot(lo, m_bf16)


def _dot_exact_lhs(m_bf16, x):
    hi, mid, lo = _split3(x)
    return _dot(m_bf16, hi) + _dot(m_bf16, mid) + _dot(m_bf16, lo)


def _hi_lo(x):
    hi = x.astype(BF16)
    return hi, (x - hi.astype(F32)).astype(BF16)


def _mm3(a_hl, b_hl, dot):
    (ah, al), (bh, bl) = a_hl, b_hl
    return dot(ah, bh) + dot(ah, bl) + dot(al, bh)


def _rwkv_body(n_valid, pr_ref, pk_ref, pv_ref, pl_ref, shift0_ref, s0_ref, mu_ref, w0_ref, wup_ref, a0_ref,
               aup_ref, gup_ref, kk_ref, ka_ref, rk_ref, lnw_ref, lnb_ref, o_ref, s_out_ref, sbd, carry):
    c_idx = pl.program_id(1)
    T = RW_CHUNK
    HT = N_HEADS * T

    @pl.when(c_idx == 0)
    def _():
        sbd[...] = s0_ref[0]
        carry[...] = shift0_ref[0]

    row = lax.broadcasted_iota(jnp.int32, (T, GROUP_W), 0)
    lane_head = lax.broadcasted_iota(jnp.int32, (T, GROUP_W), 1) // HEAD_DIM

    def shifted(p_ref, blk):
        cols = slice(blk * GROUP_W, (blk + 1) * GROUP_W)
        p = p_ref[...]
        prev = jnp.where(row == 0, carry[:, cols], pltpu.roll(p, 1, axis=0))
        carry[:, cols] = p[T - 1:T, :]
        return p + (prev - p) * mu_ref[:, cols]

    xr = shifted(pr_ref, 0)
    xk = shifted(pk_ref, 1)
    xv = shifted(pv_ref, 2)
    xl = shifted(pl_ref, 3)

    gi = lax.broadcasted_iota(jnp.int32, (GROUP_W, GROUP_W), 0) // HEAD_DIM
    gj = lax.broadcasted_iota(jnp.int32, (GROUP_W, GROUP_W), 1) // HEAD_DIM
    seg = (gi == gj).astype(BF16)

    z = -(w0_ref[...] + _dot(jnp.tanh(xl).astype(BF16), wup_ref[...]))
    softplus = jnp.maximum(z, 0.0) + jnp.log(1.0 + jnp.exp(-jnp.abs(z)))
    lw = -jnp.exp(-softplus - 0.5)
    a = jax.nn.sigmoid(a0_ref[...] + _dot(xl.astype(BF16), aup_ref[...]))
    g = _dot(jax.nn.sigmoid(xl).astype(BF16), gup_ref[...])
    kk = xk * kk_ref[...]
    kkn = kk / jnp.maximum(jnp.sqrt(_dot_exact_rhs(kk * kk, seg)), 1e-12)
    k_mod = xk * (1.0 + (a - 1.0) * ka_ref[...])
    an = -kkn
    bb = kkn * a
    if n_valid < T:
        valid = row < n_valid
        lw = jnp.where(valid, lw, 0.0)
        an = jnp.where(valid, an, 0.0)
        k_mod = jnp.where(valid, k_mod, 0.0)

    ti = lax.broadcasted_iota(jnp.int32, (T, T), 0)
    tj = lax.broadcasted_iota(jnp.int32, (T, T), 1)
    c = _dot_exact_lhs((tj <= ti).astype(BF16), lw)
    c_last = c[T - 1:T, :]
    inv_dec = jnp.exp(-c)
    to_end = jnp.exp(c_last - c)
    a_t = an * jnp.exp(c - lw)
    b_t = bb * inv_dec
    k_t = k_mod * inv_dec
    r_t = xr * jnp.exp(c)

    def stack(x):
        return jnp.concatenate([jnp.where(lane_head == h, x, 0.0) for h in range(N_HEADS)], axis=0)

    def tile(x):
        return jnp.concatenate([x] * N_HEADS, axis=0)

    ri = lax.broadcasted_iota(jnp.int32, (HT, HT), 0)
    ci = lax.broadcasted_iota(jnp.int32, (HT, HT), 1)
    same_head = (ri // T) == (ci // T)
    m_strict = same_head & ((ci % T) < (ri % T))
    m_incl = same_head & ((ci % T) <= (ri % T))

    a_p = _hi_lo(stack(a_t))
    r_p = _hi_lo(stack(r_t))
    b_p = _hi_lo(tile(b_t))
    k_p = _hi_lo(tile(k_t))
    a_ab = jnp.where(m_strict, _mm3(a_p, b_p, _dot_nt), 0.0)
    a_ak = jnp.where(m_strict, _mm3(a_p, k_p, _dot_nt), 0.0)
    a_rb = jnp.where(m_incl, _mm3(r_p, b_p, _dot_nt), 0.0)
    a_rk = jnp.where(m_incl, _mm3(r_p, k_p, _dot_nt), 0.0)

    s_old = sbd[...]
    s_p = _hi_lo(s_old)
    v_p = _hi_lo(stack(xv))

    u = _mm3(a_p, s_p, _dot_nt) + _mm3(_hi_lo(a_ak), v_p, _dot)
    pw = a_ab
    n_steps = max(1, (T - 1).bit_length())
    for step in range(n_steps):
        pw_p = _hi_lo(pw)
        u = u + _mm3(pw_p, _hi_lo(u), _dot)
        if step + 1 < n_steps:
            pw = _mm3(pw_p, pw_p, _dot)

    u_p = _hi_lo(u)
    y_s = _mm3(r_p, s_p, _dot_nt) + _mm3(_hi_lo(a_rb), u_p, _dot) + _mm3(_hi_lo(a_rk), v_p, _dot)
    y = y_s[0:T]
    for h in range(1, N_HEADS):
        y = y + y_s[h * T:(h + 1) * T]

    s_new = (s_old * jnp.exp(c_last)
             + _mm3(u_p, _hi_lo(stack(bb * to_end)), _dot_tn) + _mm3(v_p, _hi_lo(stack(k_mod * to_end)), _dot_tn))
    sbd[...] = s_new

    @pl.when(c_idx == pl.num_programs(1) - 1)
    def _():
        s_out_ref[0] = s_new

    inv_n = 1.0 / HEAD_DIM
    yc = y - _dot_exact_rhs(y, seg) * inv_n
    var = _dot_exact_rhs(yc * yc, seg) * inv_n
    yn = yc * lax.rsqrt(var + RW_LNX_EPS) * lnw_ref[...] + lnb_ref[...]
    bonus = _dot_exact_rhs(xr * k_mod * rk_ref[...], seg) * xv
    o_ref[...] = (yn + bonus) * g


def _rwkv_mix(p, col_block0, shift0, s0_bd, n_valid, B, L, wts):
    T = RW_CHUNK
    assert L % T == 0
    nc = L // T
    blk = lambda j: pl.BlockSpec((T, GROUP_W), lambda b, c: (b * nc + c, col_block0 + j))
    fixed = lambda shape: pl.BlockSpec(shape, lambda b, c: (0,) * len(shape))
    vec = fixed((1, GROUP_W))
    mat = fixed((GROUP_W, GROUP_W))
    return pl.pallas_call(
        functools.partial(_rwkv_body, n_valid),
        grid=(B, nc),
        in_specs=[blk(0), blk(1), blk(2), blk(3),
                  pl.BlockSpec((1, 1, W_RW), lambda b, c: (b, 0, 0)),
                  pl.BlockSpec((1, GROUP_W, GROUP_W), lambda b, c: (b, 0, 0)),
                  fixed((1, W_RW)), vec, mat, vec, mat, mat, vec, vec, vec, vec, vec],
        out_specs=[pl.BlockSpec((T, GROUP_W), lambda b, c: (b * nc + c, 0)),
                   pl.BlockSpec((1, GROUP_W, GROUP_W), lambda b, c: (b, 0, 0))],
        out_shape=[jax.ShapeDtypeStruct((B * L, GROUP_W), F32),
                   jax.ShapeDtypeStruct((B, GROUP_W, GROUP_W), F32)],
        scratch_shapes=[pltpu.VMEM((GROUP_W, GROUP_W), F32), pltpu.VMEM((1, W_RW), F32)],
        compiler_params=pltpu.CompilerParams(
            dimension_semantics=("arbitrary", "arbitrary"), vmem_limit_bytes=V7X_VMEM_LIMIT_BYTES),
        name="rwkv7_mix",
    )(p, p, p, p, shift0, s0_bd, *wts)


def _rwkv_weights(W, l):
    z = lambda n: jnp.zeros((n, GROUP_W), F32)
    wup = jnp.concatenate([_f32(W['rw_w_up'][l]), z(GROUP_W - RW_DECAY_LORA)], 0).astype(BF16)
    aup = jnp.concatenate([z(RW_DECAY_LORA), _f32(W['rw_a_up'][l]), z(RW_GATE_LORA)], 0).astype(BF16)
    gup = jnp.concatenate([z(RW_DECAY_LORA + RW_A_LORA), _f32(W['rw_g_up'][l])], 0).astype(BF16)
    r1 = lambda t: _f32(t).reshape(1, -1)
    return (r1(W['rw_mu'][l]), r1(W['rw_w0'][l]), wup, r1(W['rw_a0'][l]), aup, gup, r1(W['rw_k_k'][l]),
            r1(W['rw_k_a'][l]), r1(W['rw_r_k'][l]), r1(W['rw_lnx_w'][l]), r1(W['rw_lnx_b'][l]))


def _state_to_bd(s):
    eye = jnp.eye(N_HEADS, dtype=s.dtype)
    return (s[:, :, :, None, :] * eye[None, :, None, :, None]).reshape(s.shape[0], GROUP_W, GROUP_W)


def _bd_to_state(sbd):
    s5 = sbd.reshape(sbd.shape[0], N_HEADS, HEAD_DIM, N_HEADS, HEAD_DIM)
    return jnp.stack([s5[:, h, :, h, :] for h in range(N_HEADS)], axis=1)


def _head_seg_ones():
    si = lax.broadcasted_iota(jnp.int32, (GROUP_W, GROUP_W), 0) // HEAD_DIM
    sj = lax.broadcasted_iota(jnp.int32, (GROUP_W, GROUP_W), 1) // HEAD_DIM
    return (si == sj).astype(BF16)


HG_QW = N_HEADS * HG_DK
HG_BLOCK = 128
HG_SUB = 16


def _hgrn_body(has_lb, n_valid, hq_ref, hf_ref, hi_ref, hg_ref, s0_ref, lb_ref, nw_ref, o_ref, s_out_ref,
               st, qt_s, kh_s, v_s, dec_s, oi_s):
    c_idx = pl.program_id(1)
    TB, C = HG_BLOCK, HG_SUB

    @pl.when(c_idx == 0)
    def _():
        st[...] = s0_ref[0]

    hf = hf_ref[...]
    log_sig = jnp.minimum(hf, 0.0) - jnp.log(1.0 + jnp.exp(-jnp.abs(hf)))
    if has_lb:
        log_lb = lb_ref[0:1, :]
        t2 = lb_ref[1:2, :] + log_sig
        lf = jnp.maximum(log_lb, t2) + jnp.log(1.0 + jnp.exp(-jnp.abs(log_lb - t2)))
        k_in = lb_ref[2:3, :] * jax.nn.sigmoid(-hf)
    else:
        lf = log_sig
        k_in = jax.nn.sigmoid(-hf)
    hq = hq_ref[...]
    q = hq * jax.nn.sigmoid(hq) * (HG_DK ** -0.5)
    v = hi_ref[...]
    row = lax.broadcasted_iota(jnp.int32, (TB, HG_QW), 0)
    if n_valid < TB:
        lf = jnp.where(row < n_valid, lf, 0.0)
        k_in = jnp.where(row < n_valid, k_in, 0.0)

    ti = lax.broadcasted_iota(jnp.int32, (TB, TB), 0)
    tj = lax.broadcasted_iota(jnp.int32, (TB, TB), 1)
    same = (ti // C) == (tj // C)
    b = _dot_exact_lhs((same & (tj <= ti)).astype(BF16), lf)
    b_last = _dot_exact_lhs(same.astype(BF16), lf)

    gi = lax.broadcasted_iota(jnp.int32, (HG_QW, GROUP_W), 0) // HG_DK
    gj = lax.broadcasted_iota(jnp.int32, (HG_QW, GROUP_W), 1) // HEAD_DIM
    head_sum = (gi == gj).astype(BF16)

    off = row % C
    off_v = lax.broadcasted_iota(jnp.int32, (TB, GROUP_W), 0) % C
    o_intra = _dot((q * k_in).astype(BF16), head_sum) * v
    for d in range(1, C):
        ok = off >= d
        e = jnp.exp(jnp.where(ok, b - pltpu.roll(b, d, axis=0), 0.0))
        p = jnp.where(ok, q * e * pltpu.roll(k_in, d, axis=0), 0.0)
        v_d = jnp.where(off_v >= d, pltpu.roll(v, d, axis=0), 0.0)
        o_intra = o_intra + _dot(p.astype(BF16), head_sum) * v_d

    qt_s[...] = (q * jnp.exp(b)).astype(BF16)
    kh_s[...] = (k_in * jnp.exp(b_last - b)).astype(BF16)
    v_s[...] = v.astype(BF16)
    dec_s[...] = jnp.exp(b_last)

    bi = lax.broadcasted_iota(jnp.int32, (GROUP_W, HG_QW), 0) // HEAD_DIM
    bj = lax.broadcasted_iota(jnp.int32, (GROUP_W, HG_QW), 1) // HG_DK
    block_diag = bi == bj

    def group(s, carry):
        start = pl.multiple_of(s * C, C)
        rows = pl.ds(start, C)
        s_cur = st[...]
        oi_s[rows, :] = _dot_nt(qt_s[rows, :], s_cur.astype(BF16))
        upd = _dot_tn(v_s[rows, :], kh_s[rows, :])
        st[...] = s_cur * dec_s[pl.ds(start, 1), :] + jnp.where(block_diag, upd, 0.0)
        return carry
    lax.fori_loop(0, TB // C, group, 0)

    @pl.when(c_idx == pl.num_programs(1) - 1)
    def _():
        s_out_ref[0] = st[...]

    o = o_intra + oi_s[...]
    ms = _dot_exact_rhs(o * o, _head_seg_ones()) * (1.0 / HEAD_DIM)
    hg = hg_ref[...]
    o_ref[...] = o * lax.rsqrt(ms + NORM_EPS) * nw_ref[...] * (hg * jax.nn.sigmoid(hg))


def _hgrn_mix(p, col0, s0_bd, lb_rows, norm_w4, has_lb, n_valid, B, L):
    TB = HG_BLOCK
    assert L % TB == 0 and col0 % HG_QW == 0
    nc = L // TB
    qb, gb = col0 // HG_QW, (col0 + 2 * HG_QW) // GROUP_W
    fixed = lambda shape: pl.BlockSpec(shape, lambda b, c: (0,) * len(shape))
    rows = lambda width, j: pl.BlockSpec((TB, width), lambda b, c: (b * nc + c, j))
    return pl.pallas_call(
        functools.partial(_hgrn_body, has_lb, n_valid),
        grid=(B, nc),
        in_specs=[rows(HG_QW, qb), rows(HG_QW, qb + 1), rows(GROUP_W, gb), rows(GROUP_W, gb + 1),
                  pl.BlockSpec((1, GROUP_W, HG_QW), lambda b, c: (b, 0, 0)),
                  fixed((8, HG_QW)), fixed((1, GROUP_W))],
        out_specs=[rows(GROUP_W, 0), pl.BlockSpec((1, GROUP_W, HG_QW), lambda b, c: (b, 0, 0))],
        out_shape=[jax.ShapeDtypeStruct((B * L, GROUP_W), F32),
                   jax.ShapeDtypeStruct((B, GROUP_W, HG_QW), F32)],
        scratch_shapes=[pltpu.VMEM((GROUP_W, HG_QW), F32), pltpu.VMEM((TB, HG_QW), BF16),
                        pltpu.VMEM((TB, HG_QW), BF16), pltpu.VMEM((TB, GROUP_W), BF16),
                        pltpu.VMEM((TB, HG_QW), F32), pltpu.VMEM((TB, GROUP_W), F32)],
        compiler_params=pltpu.CompilerParams(
            dimension_semantics=("arbitrary", "arbitrary"), vmem_limit_bytes=V7X_VMEM_LIMIT_BYTES),
        name="hgrn2_mix",
    )(p, p, p, p, s0_bd, lb_rows, norm_w4.reshape(1, GROUP_W))


def _hg_state_to_bd(s):
    eye = jnp.eye(N_HEADS, dtype=s.dtype)
    st = jnp.swapaxes(s, 2, 3)
    return (st[:, :, :, None, :] * eye[None, :, None, :, None]).reshape(s.shape[0], GROUP_W, HG_QW)


def _hg_bd_to_state(sbd):
    s5 = sbd.reshape(sbd.shape[0], N_HEADS, HEAD_DIM, N_HEADS, HG_DK)
    return jnp.swapaxes(jnp.stack([s5[:, h, :, h, :] for h in range(N_HEADS)], axis=1), 2, 3)


def _hg_lb_rows(hg_lb, l):
    lb_soft = jax.nn.softmax(_f32(hg_lb), axis=0)
    lb = (jnp.cumsum(lb_soft, axis=0) - lb_soft[0])[l]
    z = jnp.zeros_like(lb)
    if l == 0:
        return jnp.stack([z] * 8)
    return jnp.stack([jnp.log(lb), jnp.log1p(-lb), 1.0 - lb, z, z, z, z, z])


RET_CHUNK = 64


def _ret_body(q_ref, k_ref, v_ref, g_ref, cos_ref, sin_ref, dmask_ref, qdec_ref, kdec_ref, sdec_ref, s0_ref,
              o_ref, s_out_ref, st):
    c_idx = pl.program_id(1)
    T = RET_CHUNK

    @pl.when(c_idx == 0)
    def _():
        st[...] = s0_ref[0]

    lane = lax.broadcasted_iota(jnp.int32, (T, GROUP_W), 1)
    upper_half = (lane % HEAD_DIM) >= (HEAD_DIM // 2)
    lane_head = lane // HEAD_DIM
    cos = cos_ref[...]
    sin = sin_ref[...]

    def rotary(x):
        swapped = jnp.where(upper_half, pltpu.roll(x, HEAD_DIM // 2, axis=1),
                            pltpu.roll(x, GROUP_W - HEAD_DIM // 2, axis=1))
        return x * cos + swapped * sin

    def stack(x):
        return jnp.concatenate([jnp.where(lane_head == h, x, 0.0) for h in range(N_HEADS)], axis=0)

    q = rotary(q_ref[...])
    k = rotary(k_ref[...]) * (HEAD_DIM ** -0.5)
    q_s = stack(q).astype(BF16)
    k_tl = jnp.concatenate([k] * N_HEADS, axis=0).astype(BF16)
    v_m = stack(v_ref[...]).astype(BF16)
    scores = _dot_nt(q_s, k_tl) * dmask_ref[...]
    s_old = st[...]
    o_s = _dot(scores.astype(BF16), v_m) + _dot(q_s, s_old.astype(BF16)) * qdec_ref[...]
    o = o_s[0:T]
    for h in range(1, N_HEADS):
        o = o + o_s[h * T:(h + 1) * T]
    k_w = (stack(k) * kdec_ref[...]).astype(BF16)
    s_new = s_old * sdec_ref[...] + _dot_tn(k_w, v_m)
    st[...] = s_new

    @pl.when(c_idx == pl.num_programs(1) - 1)
    def _():
        s_out_ref[0] = s_new

    ms = _dot_exact_rhs(o * o, _head_seg_ones()) * (1.0 / HEAD_DIM)
    g = g_ref[...]
    o_ref[...] = o * lax.rsqrt(ms + NORM_EPS) * (g * jax.nn.sigmoid(g))


def _ret_tables(pos, n_valid):
    T = RET_CHUNK
    half = HEAD_DIM // 2
    freq = 1.0 / (ROPE_BASE ** jnp.linspace(0.0, 1.0, half, dtype=F32))
    ang = _f32(pos)[:, None] * freq[None, :]
    cos = jnp.tile(jnp.cos(ang), (1, 2 * N_HEADS))
    sin = jnp.tile(jnp.concatenate([-jnp.sin(ang), jnp.sin(ang)], -1), (1, N_HEADS))
    log_gamma = jnp.log1p(-jnp.exp2(-5.0 - jnp.arange(N_HEADS, dtype=F32)))
    t = jnp.arange(T, dtype=F32)
    gap = t[:, None] - t[None, :]
    dm = jnp.where(gap >= 0, jnp.exp(jnp.maximum(gap, 0.0)[None] * log_gamma[:, None, None]), 0.0)
    eye = jnp.eye(N_HEADS, dtype=F32)
    dmask = (dm[:, :, None, :] * eye[:, None, :, None]).reshape(N_HEADS * T, N_HEADS * T)
    lanes = lambda col: jnp.broadcast_to(col.reshape(-1, 1), (col.size, GROUP_W))
    qdec = jnp.exp((t + 1.0)[None, :] * log_gamma[:, None])
    kdec = jnp.where(t[None, :] < n_valid, jnp.exp((n_valid - 1.0 - t)[None, :] * log_gamma[:, None]), 0.0)
    sdec = jnp.repeat(jnp.exp(n_valid * log_gamma), HEAD_DIM)
    return cos, sin, dmask, lanes(qdec), lanes(kdec), lanes(sdec)


def _ret_mix(p, col0, s0_bd, tables, B, L):
    T = RET_CHUNK
    assert L % T == 0
    nc = L // T
    HT = N_HEADS * T
    blk = lambda j: pl.BlockSpec((T, GROUP_W), lambda b, c: (b * nc + c, col0 + j))
    tab = pl.BlockSpec((T, GROUP_W), lambda b, c: (c, 0))
    fixed = lambda shape: pl.BlockSpec(shape, lambda b, c: (0,) * len(shape))
    state = lambda: pl.BlockSpec((1, GROUP_W, GROUP_W), lambda b, c: (b, 0, 0))
    return pl.pallas_call(
        _ret_body,
        grid=(B, nc),
        in_specs=[blk(0), blk(1), blk(2), blk(3), tab, tab, fixed((HT, HT)), fixed((HT, GROUP_W)),
                  fixed((HT, GROUP_W)), fixed((GROUP_W, GROUP_W)), state()],
        out_specs=[pl.BlockSpec((T, GROUP_W), lambda b, c: (b * nc + c, 0)), state()],
        out_shape=[jax.ShapeDtypeStruct((B * L, GROUP_W), F32), jax.ShapeDtypeStruct((B, GROUP_W, GROUP_W), F32)],
        scratch_shapes=[pltpu.VMEM((GROUP_W, GROUP_W), F32)],
        compiler_params=pltpu.CompilerParams(
            dimension_semantics=("arbitrary", "arbitrary"), vmem_limit_bytes=V7X_VMEM_LIMIT_BYTES),
        name="retention_mix",
    )(p, p, p, p, *tables, s0_bd)


def _layernorm(x, g, b):
    xf = _f32(x)
    mu = jnp.mean(xf, -1, keepdims=True)
    xc = xf - mu
    var = jnp.mean(xc * xc, -1, keepdims=True)
    return (xc * lax.rsqrt(var + LN_EPS) * _f32(g) + _f32(b)).astype(x.dtype)


def _rms(x, eps=NORM_EPS):
    return x * lax.rsqrt(jnp.mean(x * x, -1, keepdims=True) + eps)


def _head_ln(x, eps):
    xc = x - jnp.mean(x, -1, keepdims=True)
    return xc * lax.rsqrt(jnp.mean(xc * xc, -1, keepdims=True) + eps)


def _rotary(x, pos):
    half = x.shape[-1] // 2
    freq = 1.0 / (ROPE_BASE ** jnp.linspace(0.0, 1.0, half, dtype=jnp.float32))
    ang = _f32(pos)[:, None] * freq[None, :]
    cos = jnp.cos(ang)[None, :, None, :]
    sin = jnp.sin(ang)[None, :, None, :]
    x1, x2 = x[..., :half], x[..., half:]
    return jnp.concatenate([x1 * cos - x2 * sin, x1 * sin + x2 * cos], -1)


def _chunk_scan(step, xs, s0):
    B, L = xs[0].shape[:2]
    c = CHUNK if L % CHUNK == 0 else L
    n = L // c
    chunked = tuple(jnp.moveaxis(a.reshape(B, n, c, *a.shape[2:]), 1, 0) for a in xs)
    s, out = lax.scan(step, s0, chunked)
    return jnp.moveaxis(out, 0, 1).reshape(B, L, *out.shape[3:]), s


def _retention_step(S, xs, log_gamma):
    q, k, v = xs
    C = q.shape[1]
    t = jnp.arange(C, dtype=jnp.float32)
    gap = t[:, None] - t[None, :]
    dmask = jnp.where(gap >= 0, jnp.exp(jnp.maximum(gap, 0.0)[None] * log_gamma[:, None, None]), 0.0)
    scores = jnp.einsum('bqhd,bkhd->bhqk', q, k) * dmask[None]
    o = jnp.einsum('bhqk,bkhe->bqhe', scores, v)
    o = o + jnp.einsum('bqhd,bhde->bqhe', q, S) * jnp.exp((t + 1.0)[:, None] * log_gamma[None, :])[None, :, :, None]
    k_w = k * jnp.exp((C - 1.0 - t)[:, None] * log_gamma[None, :])[None, :, :, None]
    S = jnp.exp(C * log_gamma)[None, :, None, None] * S + jnp.einsum('bkhd,bkhe->bhde', k_w, v)
    return S, o


def _hgrn2_step(S, xs):
    q, k, logf, v = xs
    C = q.shape[1]
    b = jnp.cumsum(logf, axis=1)
    causal = jnp.tril(jnp.ones((C, C), bool))[None, :, :, None, None]
    diff = jnp.where(causal, b[:, :, None] - b[:, None, :], 0.0)
    decay = jnp.where(causal, jnp.exp(diff), 0.0)
    attn = jnp.einsum('bqhd,bqkhd,bkhd->bhqk', q, decay, k)
    o = jnp.einsum('bhqk,bkhe->bqhe', attn, v) + jnp.einsum('bqhd,bhde->bqhe', q * jnp.exp(b), S)
    b_last = b[:, -1]
    S = jnp.exp(b_last)[..., None] * S + jnp.einsum('bkhd,bkhe->bhde', k * jnp.exp(b_last[:, None] - b), v)
    return S, o


def _rwkv_scan(S0, r, w, k, v, a, b):
    def step(S, xs):
        r_t, w_t, k_t, v_t, a_t, b_t = xs
        S = (S * w_t[:, :, None, :]
             + jnp.einsum('bhvk,bhk->bhv', S, a_t)[..., None] * b_t[:, :, None, :]
             + v_t[..., None] * k_t[:, :, None, :])
        return S, jnp.einsum('bhvk,bhk->bhv', S, r_t)
    xs = tuple(jnp.moveaxis(t, 1, 0) for t in (r, w, k, v, a, b))
    S, y = lax.scan(step, S0, xs)
    return jnp.moveaxis(y, 0, 1), S


def _rwkv7(p_rw, prev_row, S0, l, W):
    B, L, _ = p_rw.shape
    p = _f32(p_rw)
    prev = jnp.concatenate([_f32(prev_row)[:, None], p[:, :-1]], axis=1)
    xs = p + (prev - p) * _f32(W['rw_mu'][l])
    r, k, v, wd, ad, gd = jnp.split(xs, RW_SPLITS, axis=-1)
    w_log = -jax.nn.softplus(-(_f32(W['rw_w0'][l]) + jnp.tanh(wd) @ _f32(W['rw_w_up'][l]))) - 0.5
    decay = jnp.exp(-jnp.exp(w_log))
    a = jax.nn.sigmoid(_f32(W['rw_a0'][l]) + ad @ _f32(W['rw_a_up'][l]))
    g = jax.nn.sigmoid(gd) @ _f32(W['rw_g_up'][l])
    hd = lambda t: t.reshape(B, L, N_HEADS, HEAD_DIM)
    kk = hd(k * _f32(W['rw_k_k'][l]))
    kk = kk / jnp.maximum(jnp.sqrt(jnp.sum(kk * kk, -1, keepdims=True)), 1e-12)
    k = hd(k * (1.0 + (a - 1.0) * _f32(W['rw_k_a'][l])))
    r, v, a_h = hd(r), hd(v), hd(a)
    y, S = _rwkv_scan(_f32(S0), r, hd(decay), k, v, -kk, kk * a_h)
    y = _head_ln(y, RW_LNX_EPS).reshape(B, L, GROUP_W) * _f32(W['rw_lnx_w'][l]) + _f32(W['rw_lnx_b'][l])
    bonus = (jnp.sum(r * k * _f32(W['rw_r_k'][l]), -1, keepdims=True) * v).reshape(B, L, GROUP_W)
    return (y + bonus) * g, S, p_rw[:, -1]


def _diff_attn_sample(q, k, v, k_past, v_past, lam):
    L = q.shape[1]
    P = k_past.shape[1]
    scale = DA_DQK ** -0.5
    s_past = jnp.einsum('bqhmd,bkhmd->bhmqk', q, k_past) * scale
    s_new = jnp.einsum('bqhmd,bkhmd->bhmqk', q, k) * scale
    s_new = jnp.where(jnp.tril(jnp.ones((L, L), bool)), s_new, MASK_NEG)
    p = jax.nn.softmax(jnp.concatenate([_f32(s_past), s_new], -1), axis=-1)
    pd = p[:, :, 0] - lam * p[:, :, 1]
    return (jnp.einsum('bhqk,bkhe->bqhe', pd[..., :P], v_past)
            + jnp.einsum('bhqk,bkhe->bqhe', pd[..., P:], v))


def _layer(l, x, pos, ret_s0, hg_s0, rw_s0, shift0, kv_past, W):
    B, L, _ = x.shape
    dt = x.dtype
    M = B * L
    proj2d, d_k, d_v = _in_proj(x.reshape(M, D_MODEL), W['w_in_bf16'][l], 256)
    proj = proj2d.reshape(B, L, N_COLS)
    rw_wts = _rwkv_weights(W, l)
    shift3 = _f32(shift0)[:, None, :]
    lb_rows = _hg_lb_rows(W['hg_lb'], l)
    hg_norm = jnp.tile(_f32(W['hg_norm_w'][l]), N_HEADS)
    ret_bd0, hg_bd0, rw_bd0 = _state_to_bd(_f32(ret_s0)), _hg_state_to_bd(_f32(hg_s0)), _state_to_bd(_f32(rw_s0))
    p_rw = proj[:, :, W_RET + W_HG:W_RET + W_HG + W_RW]

    lam_init = 0.8 - 0.6 * math.exp(-0.3 * l)
    lam = (jnp.exp(jnp.sum(_f32(W['da_lq1'][l]) * _f32(W['da_lk1'][l])))
           - jnp.exp(jnp.sum(_f32(W['da_lq2'][l]) * _f32(W['da_lk2'][l]))) + lam_init)
    c_da = W_RET + W_HG + W_RW
    if kv_past is None:
        assert L % HG_BLOCK == 0
        o_a, ret_bd = _ret_mix(proj2d, 0, ret_bd0, _ret_tables(pos, RET_CHUNK), B, L)
        o_b, hg_bd = _hgrn_mix(proj2d, W_RET, hg_bd0, lb_rows, hg_norm, l > 0, HG_BLOCK, B, L)
        o_c, rw_bd = _rwkv_mix(proj2d, RW_COL_BLOCK, shift3, rw_bd0, RW_CHUNK, B, L, rw_wts)
        ret_s, hg_s, rw_s = _bd_to_state(ret_bd), _hg_bd_to_state(hg_bd), _bd_to_state(rw_bd)
        norm_scale = jnp.tile(_f32(W['da_norm_w'][l]) * (1.0 - lam_init), N_HEADS)
        o_d = _attn_prompt(proj2d, lam, norm_scale, B, L)
    else:
        heads = lambda t, d: t.reshape(B, L, N_HEADS, d)
        r_q, r_k, r_v, r_g = jnp.split(proj[:, :, :W_RET], 4, axis=-1)
        log_gamma = jnp.log1p(-jnp.exp2(-5.0 - jnp.arange(N_HEADS, dtype=jnp.float32)))
        q = _rotary(heads(r_q, HEAD_DIM), pos)
        k = _rotary(heads(r_k, HEAD_DIM), pos) * HEAD_DIM ** -0.5
        o, ret_s = _chunk_scan(functools.partial(_retention_step, log_gamma=log_gamma),
                               (q, k, heads(r_v, HEAD_DIM)), _f32(ret_s0))
        o_a = (_rms(o).reshape(B, L, GROUP_W) * jax.nn.silu(r_g)).reshape(M, GROUP_W)

        h_q, h_f, h_i, h_g = jnp.split(proj[:, :, W_RET:W_RET + W_HG],
                                       [HG_QW, 2 * HG_QW, 2 * HG_QW + GROUP_W], axis=-1)
        if l == 0:
            log_f = jax.nn.log_sigmoid(h_f)
        else:
            log_f = jnp.logaddexp(lb_rows[0], lb_rows[1] + jax.nn.log_sigmoid(h_f))
        k_in = (lb_rows[2] if l > 0 else 1.0) * jax.nn.sigmoid(-h_f)
        q_h = jax.nn.silu(h_q) * HG_DK ** -0.5
        o, hg_s = _chunk_scan(_hgrn2_step, (heads(q_h, HG_DK), heads(k_in, HG_DK), heads(log_f, HG_DK),
                                            heads(h_i, HEAD_DIM)), _f32(hg_s0))
        o_b = ((_rms(o) * _f32(W['hg_norm_w'][l])).reshape(B, L, GROUP_W) * jax.nn.silu(h_g)).reshape(M, GROUP_W)

        o_c, rw_s, _ = _rwkv7(p_rw, shift0, rw_s0, l, W)
        o_c = o_c.reshape(M, GROUP_W)

        k_past, v_past = kv_past
        q5 = proj[:, :, c_da:c_da + GROUP_W].reshape(B, L, N_HEADS, 2, DA_DQK)
        o = _diff_attn_sample(q5, d_k.reshape(B, L, N_HEADS, 2, DA_DQK), d_v.reshape(B, L, N_HEADS, HEAD_DIM),
                              k_past.reshape(B, -1, N_HEADS, 2, DA_DQK), v_past, lam)
        o_d = (_rms(o) * _f32(W['da_norm_w'][l]) * (1.0 - lam_init)).reshape(M, GROUP_W)
    shift_new = p_rw[:, -1]

    x1 = _out_proj_ln((o_a, o_b, o_c, o_d), W['w_o_bf16'][l], x.reshape(M, D_MODEL),
                      W['ln1_g'][l], W['ln1_b'][l], 256)
    x = _moe_ln(x1, l, W).reshape(B, L, D_MODEL)
    new = (ret_s.astype(dt), hg_s.astype(dt), rw_s.astype(dt), shift_new,
           d_k.reshape(B, L, N_HEADS, HEAD_DIM), d_v.reshape(B, L, N_HEADS, HEAD_DIM))
    return x, new


def _prepare_weights(W):
    W = dict(W)
    depth = W['w_in'].shape[0]
    W['w_in_bf16'] = W['w_in'].astype(BF16)
    W['w_o_bf16'] = W['w_o'].astype(BF16)
    router_pad = jnp.zeros((depth, D_MODEL, 128 - N_EXPERTS), W['router_w'].dtype)
    W['gu_router_bf16'] = jnp.concatenate([W['sh_gate'], W['sh_up'], W['router_w'], router_pad], -1).astype(BF16)
    for name in ('e_gate', 'e_up', 'e_down', 'sh_down'):
        W[name + '_bf16'] = W[name].astype(BF16)
    return W


def kernel(x_prompt, x_sample, state_ret, state_hgrn, state_rwkv, state_rwkv_shift, cache_k, cache_v,
           page_table, w_in, w_o, hg_lb, hg_norm_w, rw_mu, rw_w0, rw_w_up, rw_a0, rw_a_up, rw_g_up,
           rw_k_k, rw_k_a, rw_r_k, rw_lnx_w, rw_lnx_b, da_lq1, da_lk1, da_lq2, da_lk2, da_norm_w,
           ln1_g, ln1_b, router_w, router_bias, e_gate, e_up, e_down, sh_gate, sh_up, sh_down,
           ln2_g, ln2_b):
    W = {'w_in': w_in, 'w_o': w_o, 'hg_lb': hg_lb, 'hg_norm_w': hg_norm_w, 'rw_mu': rw_mu,
         'rw_w0': rw_w0, 'rw_w_up': rw_w_up, 'rw_a0': rw_a0, 'rw_a_up': rw_a_up, 'rw_g_up': rw_g_up,
         'rw_k_k': rw_k_k, 'rw_k_a': rw_k_a, 'rw_r_k': rw_r_k, 'rw_lnx_w': rw_lnx_w, 'rw_lnx_b': rw_lnx_b,
         'da_lq1': da_lq1, 'da_lk1': da_lk1, 'da_lq2': da_lq2, 'da_lk2': da_lk2, 'da_norm_w': da_norm_w,
         'ln1_g': ln1_g, 'ln1_b': ln1_b, 'router_w': router_w, 'router_bias': router_bias,
         'e_gate': e_gate, 'e_up': e_up, 'e_down': e_down, 'sh_gate': sh_gate, 'sh_up': sh_up,
         'sh_down': sh_down, 'ln2_g': ln2_g, 'ln2_b': ln2_b}
    W = _prepare_weights(W)
    B, S, _ = x_prompt.shape
    DB, L, _ = x_sample.shape
    past_len = page_table.shape[1] * PAGE_SIZE
    pos_p = jnp.arange(S)
    pos_s = past_len + jnp.arange(L)
    zero_ret = jnp.zeros((B, N_HEADS, HEAD_DIM, HEAD_DIM), jnp.float32)
    zero_hg = jnp.zeros((B, N_HEADS, HG_DK, HEAD_DIM), jnp.float32)
    zero_shift = jnp.zeros((B, W_RW), x_prompt.dtype)
    yp, ys = x_prompt, x_sample
    new_p, new_s = [], []
    for l in range(DEPTH):
        yp, st = _layer(l, yp, pos_p, zero_ret, zero_hg, zero_ret, zero_shift, None, W)
        new_p.append(st)
        k_past = cache_k[l][page_table].reshape(DB, past_len, N_HEADS, HEAD_DIM)
        v_past = cache_v[l][page_table].reshape(DB, past_len, N_HEADS, HEAD_DIM)
        ys, st = _layer(l, ys, pos_s, state_ret[l], state_hgrn[l], state_rwkv[l], state_rwkv_shift[l],
                        (k_past, v_past), W)
        new_s.append(st)

    def stk(sts, i):
        return jnp.stack([s[i] for s in sts])

    return (yp, ys, stk(new_p, 0), stk(new_s, 0), stk(new_p, 1), stk(new_s, 1), stk(new_p, 2), stk(new_s, 2),
            stk(new_p, 3), stk(new_s, 3), stk(new_p, 4), stk(new_p, 5), stk(new_s, 4), stk(new_s, 5))
```

```python
import math, functools
import jax, jax.numpy as jnp
from jax import lax
from jax.experimental import pallas as pl
from jax.experimental.pallas import tpu as pltpu

D_MODEL = 1024
DEPTH = 2
PAGE_SIZE = 128
N_MIXERS = 4
GROUP_W = D_MODEL // N_MIXERS
HEAD_DIM = 64
N_HEADS = GROUP_W // HEAD_DIM
ROPE_BASE = 10000.0
HG_DK = 128
RW_DECAY_LORA = 64
RW_A_LORA = 64
RW_GATE_LORA = 128
RW_LNX_EPS = 6.4e-4
DA_DQK = HEAD_DIM // 2
W_RET = 3 * N_HEADS * HEAD_DIM + GROUP_W
W_HG = 2 * N_HEADS * HG_DK + N_HEADS * HEAD_DIM + GROUP_W
W_RW = 3 * GROUP_W + RW_DECAY_LORA + RW_A_LORA + RW_GATE_LORA
W_DA = 2 * (N_HEADS * 2 * DA_DQK) + N_HEADS * HEAD_DIM
N_COLS = W_RET + W_HG + W_RW + W_DA
RW_SPLITS = [GROUP_W, 2 * GROUP_W, 3 * GROUP_W, 3 * GROUP_W + RW_DECAY_LORA, 3 * GROUP_W + RW_DECAY_LORA + RW_A_LORA]
CHUNK = 64
Q_BLOCK = 128
MASK_NEG = -1e30
N_EXPERTS = 64
TOP_K = 8
D_EXPERT = 256
ROUTED_SCALE = 2.5
MOE_BLOCK = 128
ALPHA = (2.0 * DEPTH) ** 0.25
LN_EPS = 1e-5
NORM_EPS = 1e-5

V7X_VMEM_LIMIT_BYTES = 56 * 1024 * 1024


def _f32(t):
    return t.astype(jnp.float32)


def _matmul_body(x_ref, w_ref, o_ref):
    o_ref[...] = jnp.dot(x_ref[...].astype(jnp.bfloat16), w_ref[...],
                         preferred_element_type=jnp.float32)


def _matmul(x, w_bf16, tm):
    M, K = x.shape
    N = w_bf16.shape[1]
    tm = min(tm, M)
    assert M % tm == 0
    return pl.pallas_call(
        _matmul_body,
        grid=(M // tm,),
        in_specs=[pl.BlockSpec((tm, K), lambda i: (i, 0)),
                  pl.BlockSpec((K, N), lambda i: (0, 0))],
        out_specs=pl.BlockSpec((tm, N), lambda i: (i, 0)),
        out_shape=jax.ShapeDtypeStruct((M, N), jnp.float32),
        compiler_params=pltpu.CompilerParams(
            dimension_semantics=("arbitrary",), vmem_limit_bytes=V7X_VMEM_LIMIT_BYTES),
        name="matmul",
    )(x, w_bf16)


def _in_proj_body(x_ref, w_ref, o_ref, k_ref, v_ref):
    acc = jnp.dot(x_ref[...].astype(jnp.bfloat16), w_ref[...], preferred_element_type=jnp.float32)
    o_ref[...] = acc
    k_ref[...] = acc[:, N_COLS - 2 * GROUP_W:N_COLS - GROUP_W]
    v_ref[...] = acc[:, N_COLS - GROUP_W:]


def _in_proj(x, w_bf16, tm):
    M, K = x.shape
    tm = min(tm, M)
    assert M % tm == 0
    row = lambda i: (i, 0)
    return pl.pallas_call(
        _in_proj_body,
        grid=(M // tm,),
        in_specs=[pl.BlockSpec((tm, K), row), pl.BlockSpec((K, N_COLS), lambda i: (0, 0))],
        out_specs=[pl.BlockSpec((tm, N_COLS), row), pl.BlockSpec((tm, GROUP_W), row),
                   pl.BlockSpec((tm, GROUP_W), row)],
        out_shape=[jax.ShapeDtypeStruct((M, N_COLS), jnp.float32),
                   jax.ShapeDtypeStruct((M, GROUP_W), jnp.float32),
                   jax.ShapeDtypeStruct((M, GROUP_W), jnp.float32)],
        compiler_params=pltpu.CompilerParams(
            dimension_semantics=("arbitrary",), vmem_limit_bytes=V7X_VMEM_LIMIT_BYTES),
        name="in_proj",
    )(x, w_bf16)


def _ln_rows(z, g, b):
    mu = jnp.mean(z, -1, keepdims=True)
    zc = z - mu
    var = jnp.mean(zc * zc, -1, keepdims=True)
    return zc * lax.rsqrt(var + LN_EPS) * g + b


def _out_proj_ln_body(a_ref, b_ref, c_ref, d_ref, w_ref, res_ref, g_ref, beta_ref, o_ref):
    mix = None
    for j, part in enumerate((a_ref, b_ref, c_ref, d_ref)):
        term = jnp.dot(part[...].astype(jnp.bfloat16), w_ref[j * GROUP_W:(j + 1) * GROUP_W, :],
                       preferred_element_type=jnp.float32)
        mix = term if mix is None else mix + term
    o_ref[...] = _ln_rows(ALPHA * res_ref[...] + mix, g_ref[...], beta_ref[...])


def _out_proj_ln(parts, w_bf16, res, g, b, tm):
    M, N = res.shape
    tm = min(tm, M)
    assert M % tm == 0
    row = lambda i: (i, 0)
    fixed = lambda i: (0, 0)
    part = pl.BlockSpec((tm, GROUP_W), row)
    return pl.pallas_call(
        _out_proj_ln_body,
        grid=(M // tm,),
        in_specs=[part, part, part, part, pl.BlockSpec((N, N), fixed), pl.BlockSpec((tm, N), row),
                  pl.BlockSpec((1, N), fixed), pl.BlockSpec((1, N), fixed)],
        out_specs=pl.BlockSpec((tm, N), row),
        out_shape=jax.ShapeDtypeStruct((M, N), jnp.float32),
        compiler_params=pltpu.CompilerParams(
            dimension_semantics=("arbitrary",), vmem_limit_bytes=V7X_VMEM_LIMIT_BYTES),
        name="out_proj_ln",
    )(*parts, w_bf16, res, g.reshape(1, N), b.reshape(1, N))


EXPERT_ROWS = 256


def _start_row_gather(idx_ref, n_rows, src_hbm, dst_at, sem):
    def body(pair, carry):
        for priority in range(2):
            r = pair * 2 + priority
            t = idx_ref[0, 0, r]
            pltpu.make_async_copy(src_hbm.at[pl.ds(t, 1)], dst_at(r), sem).start(priority=priority)
        return carry
    lax.fori_loop(0, n_rows // 2, body, 0, unroll=4)


def _moe_experts_body(n_blocks, blk_ref, exp_ref, lo_ref, hi_ref, first_ref, tok_ref, tok_next_ref, x_hbm,
                      wg_ref, wu_ref, wd_ref, y_ref, xbuf, sem):
    w = pl.program_id(0)
    blk = blk_ref[w]
    slot = blk % 2
    lo = lo_ref[w]
    hi = hi_ref[w]

    def gather(idx_ref, s):
        _start_row_gather(idx_ref, EXPERT_ROWS, x_hbm, lambda r: xbuf.at[s, pl.ds(r, 1)], sem.at[s])

    @pl.when(w == 0)
    def _():
        gather(tok_ref, 0)

    @pl.when(first_ref[w] == 1)
    def _():
        @pl.when(blk + 1 < n_blocks)
        def _():
            gather(tok_next_ref, 1 - slot)
        pltpu.make_async_copy(x_hbm.at[pl.ds(0, EXPERT_ROWS)], xbuf.at[slot], sem.at[slot]).wait()
        y_ref[...] = jnp.zeros_like(y_ref)

    @pl.when(hi > lo)
    def _():
        x = xbuf[slot].astype(jnp.bfloat16)
        g = jnp.dot(x, wg_ref[0].astype(jnp.bfloat16), preferred_element_type=jnp.float32)
        u = jnp.dot(x, wu_ref[0].astype(jnp.bfloat16), preferred_element_type=jnp.float32)
        h = (g * jax.nn.sigmoid(g) * u).astype(jnp.bfloat16)
        y = jnp.dot(h, wd_ref[0].astype(jnp.bfloat16), preferred_element_type=jnp.float32)
        r = lax.broadcasted_iota(jnp.int32, (EXPERT_ROWS, 1), 0)
        y_ref[...] += jnp.where((r >= lo) & (r < hi), y, 0.0)


def _moe_experts(x, tok_sorted, items, wg, wu, wd):
    M, D = x.shape
    A = tok_sorted.shape[0]
    assert A % EXPERT_ROWS == 0
    n_blocks = A // EXPERT_ROWS
    n_items = items[0].shape[0]
    tok3 = tok_sorted.reshape(n_blocks, 1, EXPERT_ROWS)
    smem_blk = lambda f: pl.BlockSpec((1, 1, EXPERT_ROWS), f, memory_space=pltpu.SMEM)
    w_spec = lambda shape: pl.BlockSpec(shape, lambda w, blk, e, *_: (e[w], 0, 0))
    grid_spec = pltpu.PrefetchScalarGridSpec(
        num_scalar_prefetch=5,
        grid=(n_items,),
        in_specs=[smem_blk(lambda w, blk, *_: (blk[w], 0, 0)),
                  smem_blk(lambda w, blk, *_: (jnp.minimum(blk[w] + 1, n_blocks - 1), 0, 0)),
                  pl.BlockSpec(memory_space=pl.ANY),
                  w_spec((1, D, D_EXPERT)), w_spec((1, D, D_EXPERT)), w_spec((1, D_EXPERT, D))],
        out_specs=pl.BlockSpec((EXPERT_ROWS, D), lambda w, blk, *_: (blk[w], 0)),
        scratch_shapes=[pltpu.VMEM((2, EXPERT_ROWS, D), jnp.float32), pltpu.SemaphoreType.DMA((2,))],
    )
    return pl.pallas_call(
        functools.partial(_moe_experts_body, n_blocks),
        grid_spec=grid_spec,
        out_shape=jax.ShapeDtypeStruct((A, D), jnp.float32),
        compiler_params=pltpu.CompilerParams(
            dimension_semantics=("arbitrary",), vmem_limit_bytes=V7X_VMEM_LIMIT_BYTES),
        name="moe_experts",
    )(*items, tok3, tok3, x, wg, wu, wd)


def _moe_work_items(e_sorted, A):
    n_blocks = A // EXPERT_ROWS
    experts = jnp.arange(N_EXPERTS, dtype=jnp.int32)
    group_end = jnp.sum((e_sorted[None, :] <= experts[:, None]).astype(jnp.int32), axis=1)
    starts = jnp.sort(jnp.concatenate([jnp.arange(n_blocks, dtype=jnp.int32) * EXPERT_ROWS, group_end]))
    ends = jnp.concatenate([starts[1:], jnp.full((1,), A, jnp.int32)])
    blk = jnp.minimum(starts // EXPERT_ROWS, n_blocks - 1)
    expert = jnp.minimum(jnp.sum((group_end[None, :] <= starts[:, None]).astype(jnp.int32), axis=1), N_EXPERTS - 1)
    new_start = jnp.concatenate([jnp.ones((1,), bool), starts[1:] != starts[:-1]])
    first = ((starts % EXPERT_ROWS == 0) & (starts < A) & new_start).astype(jnp.int32)
    base = blk * EXPERT_ROWS
    return blk, expert, starts - base, ends - base, first


COMBINE_ROWS = 128


def _moe_combine_body(pos_ref, pos_next_ref, wts_ref, x_ref, gu_ref, shd_ref, g_ref, b_ref, y_hbm,
                      o_ref, buf, sem):
    i = pl.program_id(0)
    n = pl.num_programs(0)
    slot = i % 2
    n_rows = COMBINE_ROWS * TOP_K

    def gather(idx_ref, s):
        def dst(r):
            k = jnp.bitwise_and(r, TOP_K - 1)
            token = lax.shift_right_logical(r, TOP_K.bit_length() - 1)
            return buf.at[s, pl.ds(k * COMBINE_ROWS + token, 1)]
        _start_row_gather(idx_ref, n_rows, y_hbm, dst, sem.at[s])

    @pl.when(i == 0)
    def _():
        gather(pos_ref, 0)

    @pl.when(i + 1 < n)
    def _():
        gather(pos_next_ref, 1 - slot)

    pltpu.make_async_copy(y_hbm.at[pl.ds(0, n_rows)], buf.at[slot], sem.at[slot]).wait()
    wts = wts_ref[...]
    routed = wts[:, 0:1] * buf[slot, pl.ds(0, COMBINE_ROWS)]
    for k in range(1, TOP_K):
        routed = routed + wts[:, k:k + 1] * buf[slot, pl.ds(k * COMBINE_ROWS, COMBINE_ROWS)]
    gu = gu_ref[...]
    g = gu[:, :D_EXPERT]
    u = gu[:, D_EXPERT:]
    h = (g * jax.nn.sigmoid(g) * u).astype(jnp.bfloat16)
    shared = jnp.dot(h, shd_ref[...], preferred_element_type=jnp.float32)
    o_ref[...] = _ln_rows(ALPHA * x_ref[...] + (routed + shared), g_ref[...], b_ref[...])


def _moe_combine(y_sorted, inv_pos, wts, x, gu, shd_bf16, g, b):
    M, D = x.shape
    tt = min(COMBINE_ROWS, M)
    assert tt == COMBINE_ROWS and M % tt == 0
    n_tiles = M // tt
    pos3 = inv_pos.reshape(n_tiles, 1, tt * TOP_K)
    row = lambda i: (i, 0)
    fixed = lambda i: (0, 0)
    smem_blk = lambda f: pl.BlockSpec((1, 1, tt * TOP_K), f, memory_space=pltpu.SMEM)
    return pl.pallas_call(
        _moe_combine_body,
        grid=(n_tiles,),
        in_specs=[smem_blk(lambda i: (i, 0, 0)),
                  smem_blk(lambda i: (jnp.minimum(i + 1, n_tiles - 1), 0, 0)),
                  pl.BlockSpec((tt, TOP_K), row),
                  pl.BlockSpec((tt, D), row),
                  pl.BlockSpec((tt, 2 * D_EXPERT), row),
                  pl.BlockSpec((D_EXPERT, D), fixed),
                  pl.BlockSpec((1, D), fixed), pl.BlockSpec((1, D), fixed),
                  pl.BlockSpec(memory_space=pl.ANY)],
        out_specs=pl.BlockSpec((tt, D), row),
        out_shape=jax.ShapeDtypeStruct((M, D), jnp.float32),
        scratch_shapes=[pltpu.VMEM((2, tt * TOP_K, D), jnp.float32), pltpu.SemaphoreType.DMA((2,))],
        compiler_params=pltpu.CompilerParams(
            dimension_semantics=("arbitrary",), vmem_limit_bytes=V7X_VMEM_LIMIT_BYTES),
        name="moe_combine_ln",
    )(pos3, pos3, wts, x, gu, shd_bf16, g.reshape(1, D), b.reshape(1, D), y_sorted)


def _moe_ln(x, l, W):
    M, D = x.shape
    rgu = _matmul(x, W['gu_router_bf16'][l], 256)
    scores = jax.nn.sigmoid(rgu[:, 2 * D_EXPERT:2 * D_EXPERT + N_EXPERTS])
    _, idx = lax.top_k(scores + _f32(W['router_bias'][l]), TOP_K)
    wts = jnp.take_along_axis(scores, idx, axis=-1)
    wts = wts / jnp.sum(wts, -1, keepdims=True) * ROUTED_SCALE

    A = M * TOP_K
    flat_e = idx.reshape(-1).astype(jnp.int32)
    assign = jnp.arange(A, dtype=jnp.int32)
    e_sorted, order = lax.sort_key_val(flat_e, assign)
    _, inv_pos = lax.sort_key_val(order, assign)
    tok_sorted = lax.shift_right_logical(order, TOP_K.bit_length() - 1)
    items = _moe_work_items(e_sorted, A)

    y_sorted = _moe_experts(x, tok_sorted, items, W['e_gate'][l], W['e_up'][l], W['e_down'][l])
    return _moe_combine(y_sorted, inv_pos, wts, x, rgu, W['sh_down_bf16'][l], W['ln2_g'][l], W['ln2_b'][l])


DA_COL_BLOCK = (W_RET + W_HG + W_RW) // GROUP_W
ATTN_BLOCK = 256
LOG2E = 1.4426950408889634
BF16 = jnp.bfloat16
F32 = jnp.float32


ACC_W = 128


def _attn_prompt_body(lam_ref, q_ref, k_ref, v_ref, nw_ref, o_ref, kbf, vh, qs, m_scr, acc_scr):
    i = pl.program_id(1)
    T = ATTN_BLOCK
    n_hm = 2 * N_HEADS
    S = k_ref.shape[0]

    @pl.when(i == 0)
    def _():
        kbf[...] = k_ref[...].astype(BF16)
        v = v_ref[...]
        ones_col = (lax.broadcasted_iota(jnp.int32, (S, ACC_W - HEAD_DIM), 1) == 0).astype(F32)
        for h in range(N_HEADS):
            vh[h] = jnp.concatenate([v[:, h * HEAD_DIM:(h + 1) * HEAD_DIM], ones_col], -1).astype(BF16)

    q = q_ref[...]
    lane = lax.broadcasted_iota(jnp.int32, (T, GROUP_W), 1)
    for hm in range(n_hm):
        keep = (lane >= hm * DA_DQK) & (lane < (hm + 1) * DA_DQK)
        qs[pl.ds(hm * T, T), :] = jnp.where(keep, q, 0.0).astype(BF16)
    m_scr[...] = jnp.full_like(m_scr, MASK_NEG)
    acc_scr[...] = jnp.zeros_like(acc_scr)
    score_scale = DA_DQK ** -0.5 * LOG2E

    def kv_block(j, masked):
        kb = kbf[pl.ds(j * T, T), :]
        for h in range(N_HEADS):
            rows = pl.ds(h * 2 * T, 2 * T)
            s = lax.dot_general(qs[rows, :], kb, (((1,), (1,)), ((), ())), preferred_element_type=F32)
            s = s * score_scale
            if masked:
                qpos = lax.broadcasted_iota(jnp.int32, (2 * T, T), 0) % T
                kpos = lax.broadcasted_iota(jnp.int32, (2 * T, T), 1)
                s = jnp.where(kpos <= qpos, s, MASK_NEG)
            m_old = m_scr[rows, :]
            m_new = jnp.maximum(m_old, jnp.max(s, -1, keepdims=True))
            alpha = jnp.exp2(m_old - m_new)
            p = jnp.exp2(s - jnp.concatenate([m_new] * (T // ACC_W), -1))
            pv = jnp.dot(p.astype(BF16), vh[h, pl.ds(j * T, T), :], preferred_element_type=F32)
            acc_scr[rows, :] = alpha * acc_scr[rows, :] + pv
            m_scr[rows, :] = m_new

    def body(j, c):
        kv_block(j, False)
        return c
    lax.fori_loop(0, i, body, 0)
    kv_block(i, True)

    lam = lam_ref[0]
    outs = []
    for h in range(N_HEADS):
        a0 = acc_scr[pl.ds(h * 2 * T, T), :]
        a1 = acc_scr[pl.ds(h * 2 * T + T, T), :]
        o = (a0[:, :HEAD_DIM] / a0[:, HEAD_DIM:HEAD_DIM + 1]
             - lam * (a1[:, :HEAD_DIM] / a1[:, HEAD_DIM:HEAD_DIM + 1]))
        outs.append(o * lax.rsqrt(jnp.mean(o * o, -1, keepdims=True) + NORM_EPS))
    o_ref[...] = jnp.concatenate(outs, -1) * nw_ref[...]


def _attn_prompt(proj, lam, norm_scale, B, S):
    T = ATTN_BLOCK
    assert S % T == 0
    nq = S // T
    n_rows = 2 * N_HEADS * T
    return pl.pallas_call(
        _attn_prompt_body,
        grid=(B, nq),
        in_specs=[pl.BlockSpec(memory_space=pltpu.SMEM),
                  pl.BlockSpec((T, GROUP_W), lambda b, i: (b * nq + i, DA_COL_BLOCK)),
                  pl.BlockSpec((S, GROUP_W), lambda b, i: (b, DA_COL_BLOCK + 1)),
                  pl.BlockSpec((S, GROUP_W), lambda b, i: (b, DA_COL_BLOCK + 2)),
                  pl.BlockSpec((1, GROUP_W), lambda b, i: (0, 0))],
        out_specs=pl.BlockSpec((T, GROUP_W), lambda b, i: (b * nq + i, 0)),
        out_shape=jax.ShapeDtypeStruct((B * S, GROUP_W), F32),
        scratch_shapes=[pltpu.VMEM((S, GROUP_W), BF16),
                        pltpu.VMEM((N_HEADS, S, ACC_W), BF16),
                        pltpu.VMEM((n_rows, GROUP_W), BF16),
                        pltpu.VMEM((n_rows, ACC_W), F32),
                        pltpu.VMEM((n_rows, ACC_W), F32)],
        compiler_params=pltpu.CompilerParams(
            dimension_semantics=("arbitrary", "arbitrary"), vmem_limit_bytes=V7X_VMEM_LIMIT_BYTES),
        name="diff_attn_prompt",
    )(lam.reshape(1), proj, proj, proj, norm_scale.reshape(1, GROUP_W))


RW_COL_BLOCK = (W_RET + W_HG) // GROUP_W
RW_CHUNK = 64


def _dot(a, b):
    return jnp.dot(a, b, preferred_element_type=F32)


def _dot_nt(a, b):
    return lax.dot_general(a, b, (((1,), (1,)), ((), ())), preferred_element_type=F32)


def _dot_tn(a, b):
    return lax.dot_general(a, b, (((0,), (0,)), ((), ())), preferred_element_type=F32)


def _split3(x):
    hi = x.astype(BF16)
    r1 = x - hi.astype(F32)
    mid = r1.astype(BF16)
    lo = (r1 - mid.astype(F32)).astype(BF16)
    return hi, mid, lo


def _dot_exact_rhs(x, m_bf16):
    hi, mid, lo = _split3(x)
    return _dot(hi, m_bf16) + _dot(mid, m_bf16) + _dot(lo, m_bf16)


def _dot_exact_lhs(m_bf16, x):
    hi, mid, lo = _split3(x)
    return _dot(m_bf16, hi) + _dot(m_bf16, mid) + _dot(m_bf16, lo)


def _hi_lo(x):
    hi = x.astype(BF16)
    return hi, (x - hi.astype(F32)).astype(BF16)


def _mm3(a_hl, b_hl, dot):
    (ah, al), (bh, bl) = a_hl, b_hl
    return dot(ah, bh) + dot(ah, bl) + dot(al, bh)


def _rwkv_body(n_valid, pr_ref, pk_ref, pv_ref, pl_ref, shift0_ref, s0_ref, mu_ref, w0_ref, wup_ref, a0_ref,
               aup_ref, gup_ref, kk_ref, ka_ref, rk_ref, lnw_ref, lnb_ref, o_ref, s_out_ref, sbd, carry):
    c_idx = pl.program_id(1)
    T = RW_CHUNK
    HT = N_HEADS * T

    @pl.when(c_idx == 0)
    def _():
        sbd[...] = s0_ref[0]
        carry[...] = shift0_ref[0]

    row = lax.broadcasted_iota(jnp.int32, (T, GROUP_W), 0)
    lane_head = lax.broadcasted_iota(jnp.int32, (T, GROUP_W), 1) // HEAD_DIM

    def shifted(p_ref, blk):
        cols = slice(blk * GROUP_W, (blk + 1) * GROUP_W)
        p = p_ref[...]
        prev = jnp.where(row == 0, carry[:, cols], pltpu.roll(p, 1, axis=0))
        carry[:, cols] = p[T - 1:T, :]
        return p + (prev - p) * mu_ref[:, cols]

    xr = shifted(pr_ref, 0)
    xk = shifted(pk_ref, 1)
    xv = shifted(pv_ref, 2)
    xl = shifted(pl_ref, 3)

    gi = lax.broadcasted_iota(jnp.int32, (GROUP_W, GROUP_W), 0) // HEAD_DIM
    gj = lax.broadcasted_iota(jnp.int32, (GROUP_W, GROUP_W), 1) // HEAD_DIM
    seg = (gi == gj).astype(BF16)

    z = -(w0_ref[...] + _dot(jnp.tanh(xl).astype(BF16), wup_ref[...]))
    softplus = jnp.maximum(z, 0.0) + jnp.log(1.0 + jnp.exp(-jnp.abs(z)))
    lw = -jnp.exp(-softplus - 0.5)
    a = jax.nn.sigmoid(a0_ref[...] + _dot(xl.astype(BF16), aup_ref[...]))
    g = _dot(jax.nn.sigmoid(xl).astype(BF16), gup_ref[...])
    kk = xk * kk_ref[...]
    kkn = kk / jnp.maximum(jnp.sqrt(_dot_exact_rhs(kk * kk, seg)), 1e-12)
    k_mod = xk * (1.0 + (a - 1.0) * ka_ref[...])
    an = -kkn
    bb = kkn * a
    if n_valid < T:
        valid = row < n_valid
        lw = jnp.where(valid, lw, 0.0)
        an = jnp.where(valid, an, 0.0)
        k_mod = jnp.where(valid, k_mod, 0.0)

    ti = lax.broadcasted_iota(jnp.int32, (T, T), 0)
    tj = lax.broadcasted_iota(jnp.int32, (T, T), 1)
    c = _dot_exact_lhs((tj <= ti).astype(BF16), lw)
    c_last = c[T - 1:T, :]
    inv_dec = jnp.exp(-c)
    to_end = jnp.exp(c_last - c)
    a_t = an * jnp.exp(c - lw)
    b_t = bb * inv_dec
    k_t = k_mod * inv_dec
    r_t = xr * jnp.exp(c)

    def stack(x):
        return jnp.concatenate([jnp.where(lane_head == h, x, 0.0) for h in range(N_HEADS)], axis=0)

    def tile(x):
        return jnp.concatenate([x] * N_HEADS, axis=0)

    ri = lax.broadcasted_iota(jnp.int32, (HT, HT), 0)
    ci = lax.broadcasted_iota(jnp.int32, (HT, HT), 1)
    same_head = (ri // T) == (ci // T)
    m_strict = same_head & ((ci % T) < (ri % T))
    m_incl = same_head & ((ci % T) <= (ri % T))

    a_p = _hi_lo(stack(a_t))
    r_p = _hi_lo(stack(r_t))
    b_p = _hi_lo(tile(b_t))
    k_p = _hi_lo(tile(k_t))
    a_ab = jnp.where(m_strict, _mm3(a_p, b_p, _dot_nt), 0.0)
    a_ak = jnp.where(m_strict, _mm3(a_p, k_p, _dot_nt), 0.0)
    a_rb = jnp.where(m_incl, _mm3(r_p, b_p, _dot_nt), 0.0)
    a_rk = jnp.where(m_incl, _mm3(r_p, k_p, _dot_nt), 0.0)

    s_old = sbd[...]
    s_p = _hi_lo(s_old)
    v_p = _hi_lo(stack(xv))

    u = _mm3(a_p, s_p, _dot_nt) + _mm3(_hi_lo(a_ak), v_p, _dot)
    pw = a_ab
    n_steps = max(1, (T - 1).bit_length())
    for step in range(n_steps):
        pw_p = _hi_lo(pw)
        u = u + _mm3(pw_p, _hi_lo(u), _dot)
        if step + 1 < n_steps:
            pw = _mm3(pw_p, pw_p, _dot)

    u_p = _hi_lo(u)
    y_s = _mm3(r_p, s_p, _dot_nt) + _mm3(_hi_lo(a_rb), u_p, _dot) + _mm3(_hi_lo(a_rk), v_p, _dot)
    y = y_s[0:T]
    for h in range(1, N_HEADS):
        y = y + y_s[h * T:(h + 1) * T]

    s_new = (s_old * jnp.exp(c_last)
             + _mm3(u_p, _hi_lo(stack(bb * to_end)), _dot_tn) + _mm3(v_p, _hi_lo(stack(k_mod * to_end)), _dot_tn))
    sbd[...] = s_new

    @pl.when(c_idx == pl.num_programs(1) - 1)
    def _():
        s_out_ref[0] = s_new

    inv_n = 1.0 / HEAD_DIM
    yc = y - _dot_exact_rhs(y, seg) * inv_n
    var = _dot_exact_rhs(yc * yc, seg) * inv_n
    yn = yc * lax.rsqrt(var + RW_LNX_EPS) * lnw_ref[...] + lnb_ref[...]
    bonus = _dot_exact_rhs(xr * k_mod * rk_ref[...], seg) * xv
    o_ref[...] = (yn + bonus) * g


def _rwkv_mix(p, col_block0, shift0, s0_bd, n_valid, B, L, wts):
    T = RW_CHUNK
    assert L % T == 0
    nc = L // T
    blk = lambda j: pl.BlockSpec((T, GROUP_W), lambda b, c: (b * nc + c, col_block0 + j))
    fixed = lambda shape: pl.BlockSpec(shape, lambda b, c: (0,) * len(shape))
    vec = fixed((1, GROUP_W))
    mat = fixed((GROUP_W, GROUP_W))
    return pl.pallas_call(
        functools.partial(_rwkv_body, n_valid),
        grid=(B, nc),
        in_specs=[blk(0), blk(1), blk(2), blk(3),
                  pl.BlockSpec((1, 1, W_RW), lambda b, c: (b, 0, 0)),
                  pl.BlockSpec((1, GROUP_W, GROUP_W), lambda b, c: (b, 0, 0)),
                  fixed((1, W_RW)), vec, mat, vec, mat, mat, vec, vec, vec, vec, vec],
        out_specs=[pl.BlockSpec((T, GROUP_W), lambda b, c: (b * nc + c, 0)),
                   pl.BlockSpec((1, GROUP_W, GROUP_W), lambda b, c: (b, 0, 0))],
        out_shape=[jax.ShapeDtypeStruct((B * L, GROUP_W), F32),
                   jax.ShapeDtypeStruct((B, GROUP_W, GROUP_W), F32)],
        scratch_shapes=[pltpu.VMEM((GROUP_W, GROUP_W), F32), pltpu.VMEM((1, W_RW), F32)],
        compiler_params=pltpu.CompilerParams(
            dimension_semantics=("arbitrary", "arbitrary"), vmem_limit_bytes=V7X_VMEM_LIMIT_BYTES),
        name="rwkv7_mix",
    )(p, p, p, p, shift0, s0_bd, *wts)


def _rwkv_weights(W, l):
    z = lambda n: jnp.zeros((n, GROUP_W), F32)
    wup = jnp.concatenate([_f32(W['rw_w_up'][l]), z(GROUP_W - RW_DECAY_LORA)], 0).astype(BF16)
    aup = jnp.concatenate([z(RW_DECAY_LORA), _f32(W['rw_a_up'][l]), z(RW_GATE_LORA)], 0).astype(BF16)
    gup = jnp.concatenate([z(RW_DECAY_LORA + RW_A_LORA), _f32(W['rw_g_up'][l])], 0).astype(BF16)
    r1 = lambda t: _f32(t).reshape(1, -1)
    return (r1(W['rw_mu'][l]), r1(W['rw_w0'][l]), wup, r1(W['rw_a0'][l]), aup, gup, r1(W['rw_k_k'][l]),
            r1(W['rw_k_a'][l]), r1(W['rw_r_k'][l]), r1(W['rw_lnx_w'][l]), r1(W['rw_lnx_b'][l]))


def _state_to_bd(s):
    eye = jnp.eye(N_HEADS, dtype=s.dtype)
    return (s[:, :, :, None, :] * eye[None, :, None, :, None]).reshape(s.shape[0], GROUP_W, GROUP_W)


def _bd_to_state(sbd):
    s5 = sbd.reshape(sbd.shape[0], N_HEADS, HEAD_DIM, N_HEADS, HEAD_DIM)
    return jnp.stack([s5[:, h, :, h, :] for h in range(N_HEADS)], axis=1)


def _head_seg_ones():
    si = lax.broadcasted_iota(jnp.int32, (GROUP_W, GROUP_W), 0) // HEAD_DIM
    sj = lax.broadcasted_iota(jnp.int32, (GROUP_W, GROUP_W), 1) // HEAD_DIM
    return (si == sj).astype(BF16)


HG_QW = N_HEADS * HG_DK
HG_BLOCK = 128
HG_SUB = 16


def _hgrn_body(has_lb, n_valid, hq_ref, hf_ref, hi_ref, hg_ref, s0_ref, lb_ref, nw_ref, o_ref, s_out_ref,
               st, qt_s, kh_s, v_s, dec_s, oi_s):
    c_idx = pl.program_id(1)
    TB, C = HG_BLOCK, HG_SUB

    @pl.when(c_idx == 0)
    def _():
        st[...] = s0_ref[0]

    hf = hf_ref[...]
    log_sig = jnp.minimum(hf, 0.0) - jnp.log(1.0 + jnp.exp(-jnp.abs(hf)))
    if has_lb:
        log_lb = lb_ref[0:1, :]
        t2 = lb_ref[1:2, :] + log_sig
        lf = jnp.maximum(log_lb, t2) + jnp.log(1.0 + jnp.exp(-jnp.abs(log_lb - t2)))
        k_in = lb_ref[2:3, :] * jax.nn.sigmoid(-hf)
    else:
        lf = log_sig
        k_in = jax.nn.sigmoid(-hf)
    hq = hq_ref[...]
    q = hq * jax.nn.sigmoid(hq) * (HG_DK ** -0.5)
    v = hi_ref[...]
    row = lax.broadcasted_iota(jnp.int32, (TB, HG_QW), 0)
    if n_valid < TB:
        lf = jnp.where(row < n_valid, lf, 0.0)
        k_in = jnp.where(row < n_valid, k_in, 0.0)

    ti = lax.broadcasted_iota(jnp.int32, (TB, TB), 0)
    tj = lax.broadcasted_iota(jnp.int32, (TB, TB), 1)
    same = (ti // C) == (tj // C)
    b = _dot_exact_lhs((same & (tj <= ti)).astype(BF16), lf)
    b_last = _dot_exact_lhs(same.astype(BF16), lf)

    gi = lax.broadcasted_iota(jnp.int32, (HG_QW, GROUP_W), 0) // HG_DK
    gj = lax.broadcasted_iota(jnp.int32, (HG_QW, GROUP_W), 1) // HEAD_DIM
    head_sum = (gi == gj).astype(BF16)

    off = row % C
    off_v = lax.broadcasted_iota(jnp.int32, (TB, GROUP_W), 0) % C
    o_intra = _dot((q * k_in).astype(BF16), head_sum) * v
    for d in range(1, C):
        ok = off >= d
        e = jnp.exp(jnp.where(ok, b - pltpu.roll(b, d, axis=0), 0.0))
        p = jnp.where(ok, q * e * pltpu.roll(k_in, d, axis=0), 0.0)
        v_d = jnp.where(off_v >= d, pltpu.roll(v, d, axis=0), 0.0)
        o_intra = o_intra + _dot(p.astype(BF16), head_sum) * v_d

    qt_s[...] = (q * jnp.exp(b)).astype(BF16)
    kh_s[...] = (k_in * jnp.exp(b_last - b)).astype(BF16)
    v_s[...] = v.astype(BF16)
    dec_s[...] = jnp.exp(b_last)

    bi = lax.broadcasted_iota(jnp.int32, (GROUP_W, HG_QW), 0) // HEAD_DIM
    bj = lax.broadcasted_iota(jnp.int32, (GROUP_W, HG_QW), 1) // HG_DK
    block_diag = bi == bj

    def group(s, carry):
        start = pl.multiple_of(s * C, C)
        rows = pl.ds(start, C)
        s_cur = st[...]
        oi_s[rows, :] = _dot_nt(qt_s[rows, :], s_cur.astype(BF16))
        upd = _dot_tn(v_s[rows, :], kh_s[rows, :])
        st[...] = s_cur * dec_s[pl.ds(start, 1), :] + jnp.where(block_diag, upd, 0.0)
        return carry
    lax.fori_loop(0, TB // C, group, 0)

    @pl.when(c_idx == pl.num_programs(1) - 1)
    def _():
        s_out_ref[0] = st[...]

    o = o_intra + oi_s[...]
    ms = _dot_exact_rhs(o * o, _head_seg_ones()) * (1.0 / HEAD_DIM)
    hg = hg_ref[...]
    o_ref[...] = o * lax.rsqrt(ms + NORM_EPS) * nw_ref[...] * (hg * jax.nn.sigmoid(hg))


def _hgrn_mix(p, col0, s0_bd, lb_rows, norm_w4, has_lb, n_valid, B, L):
    TB = HG_BLOCK
    assert L % TB == 0 and col0 % HG_QW == 0
    nc = L // TB
    qb, gb = col0 // HG_QW, (col0 + 2 * HG_QW) // GROUP_W
    fixed = lambda shape: pl.BlockSpec(shape, lambda b, c: (0,) * len(shape))
    rows = lambda width, j: pl.BlockSpec((TB, width), lambda b, c: (b * nc + c, j))
    return pl.pallas_call(
        functools.partial(_hgrn_body, has_lb, n_valid),
        grid=(B, nc),
        in_specs=[rows(HG_QW, qb), rows(HG_QW, qb + 1), rows(GROUP_W, gb), rows(GROUP_W, gb + 1),
                  pl.BlockSpec((1, GROUP_W, HG_QW), lambda b, c: (b, 0, 0)),
                  fixed((8, HG_QW)), fixed((1, GROUP_W))],
        out_specs=[rows(GROUP_W, 0), pl.BlockSpec((1, GROUP_W, HG_QW), lambda b, c: (b, 0, 0))],
        out_shape=[jax.ShapeDtypeStruct((B * L, GROUP_W), F32),
                   jax.ShapeDtypeStruct((B, GROUP_W, HG_QW), F32)],
        scratch_shapes=[pltpu.VMEM((GROUP_W, HG_QW), F32), pltpu.VMEM((TB, HG_QW), BF16),
                        pltpu.VMEM((TB, HG_QW), BF16), pltpu.VMEM((TB, GROUP_W), BF16),
                        pltpu.VMEM((TB, HG_QW), F32), pltpu.VMEM((TB, GROUP_W), F32)],
        compiler_params=pltpu.CompilerParams(
            dimension_semantics=("arbitrary", "arbitrary"), vmem_limit_bytes=V7X_VMEM_LIMIT_BYTES),
        name="hgrn2_mix",
    )(p, p, p, p, s0_bd, lb_rows, norm_w4.reshape(1, GROUP_W))


def _hg_state_to_bd(s):
    eye = jnp.eye(N_HEADS, dtype=s.dtype)
    st = jnp.swapaxes(s, 2, 3)
    return (st[:, :, :, None, :] * eye[None, :, None, :, None]).reshape(s.shape[0], GROUP_W, HG_QW)


def _hg_bd_to_state(sbd):
    s5 = sbd.reshape(sbd.shape[0], N_HEADS, HEAD_DIM, N_HEADS, HG_DK)
    return jnp.swapaxes(jnp.stack([s5[:, h, :, h, :] for h in range(N_HEADS)], axis=1), 2, 3)


def _hg_lb_rows(hg_lb, l):
    lb_soft = jax.nn.softmax(_f32(hg_lb), axis=0)
    lb = (jnp.cumsum(lb_soft, axis=0) - lb_soft[0])[l]
    z = jnp.zeros_like(lb)
    if l == 0:
        return jnp.stack([z] * 8)
    return jnp.stack([jnp.log(lb), jnp.log1p(-lb), 1.0 - lb, z, z, z, z, z])


RET_CHUNK = 64


def _ret_body(q_ref, k_ref, v_ref, g_ref, cos_ref, sin_ref, dmask_ref, qdec_ref, kdec_ref, sdec_ref, s0_ref,
              o_ref, s_out_ref, st):
    c_idx = pl.program_id(1)
    T = RET_CHUNK

    @pl.when(c_idx == 0)
    def _():
        st[...] = s0_ref[0]

    lane = lax.broadcasted_iota(jnp.int32, (T, GROUP_W), 1)
    upper_half = (lane % HEAD_DIM) >= (HEAD_DIM // 2)
    lane_head = lane // HEAD_DIM
    cos = cos_ref[...]
    sin = sin_ref[...]

    def rotary(x):
        swapped = jnp.where(upper_half, pltpu.roll(x, HEAD_DIM // 2, axis=1),
                            pltpu.roll(x, GROUP_W - HEAD_DIM // 2, axis=1))
        return x * cos + swapped * sin

    def stack(x):
        return jnp.concatenate([jnp.where(lane_head == h, x, 0.0) for h in range(N_HEADS)], axis=0)

    q = rotary(q_ref[...])
    k = rotary(k_ref[...]) * (HEAD_DIM ** -0.5)
    q_s = stack(q).astype(BF16)
    k_tl = jnp.concatenate([k] * N_HEADS, axis=0).astype(BF16)
    v_m = stack(v_ref[...]).astype(BF16)
    scores = _dot_nt(q_s, k_tl) * dmask_ref[...]
    s_old = st[...]
    o_s = _dot(scores.astype(BF16), v_m) + _dot(q_s, s_old.astype(BF16)) * qdec_ref[...]
    o = o_s[0:T]
    for h in range(1, N_HEADS):
        o = o + o_s[h * T:(h + 1) * T]
    k_w = (stack(k) * kdec_ref[...]).astype(BF16)
    s_new = s_old * sdec_ref[...] + _dot_tn(k_w, v_m)
    st[...] = s_new

    @pl.when(c_idx == pl.num_programs(1) - 1)
    def _():
        s_out_ref[0] = s_new

    ms = _dot_exact_rhs(o * o, _head_seg_ones()) * (1.0 / HEAD_DIM)
    g = g_ref[...]
    o_ref[...] = o * lax.rsqrt(ms + NORM_EPS) * (g * jax.nn.sigmoid(g))


def _ret_tables(pos, n_valid):
    T = RET_CHUNK
    half = HEAD_DIM // 2
    freq = 1.0 / (ROPE_BASE ** jnp.linspace(0.0, 1.0, half, dtype=F32))
    ang = _f32(pos)[:, None] * freq[None, :]
    cos = jnp.tile(jnp.cos(ang), (1, 2 * N_HEADS))
    sin = jnp.tile(jnp.concatenate([-jnp.sin(ang), jnp.sin(ang)], -1), (1, N_HEADS))
    log_gamma = jnp.log1p(-jnp.exp2(-5.0 - jnp.arange(N_HEADS, dtype=F32)))
    t = jnp.arange(T, dtype=F32)
    gap = t[:, None] - t[None, :]
    dm = jnp.where(gap >= 0, jnp.exp(jnp.maximum(gap, 0.0)[None] * log_gamma[:, None, None]), 0.0)
    eye = jnp.eye(N_HEADS, dtype=F32)
    dmask = (dm[:, :, None, :] * eye[:, None, :, None]).reshape(N_HEADS * T, N_HEADS * T)
    lanes = lambda col: jnp.broadcast_to(col.reshape(-1, 1), (col.size, GROUP_W))
    qdec = jnp.exp((t + 1.0)[None, :] * log_gamma[:, None])
    kdec = jnp.where(t[None, :] < n_valid, jnp.exp((n_valid - 1.0 - t)[None, :] * log_gamma[:, None]), 0.0)
    sdec = jnp.repeat(jnp.exp(n_valid * log_gamma), HEAD_DIM)
    return cos, sin, dmask, lanes(qdec), lanes(kdec), lanes(sdec)


def _ret_mix(p, col0, s0_bd, tables, B, L):
    T = RET_CHUNK
    assert L % T == 0
    nc = L // T
    HT = N_HEADS * T
    blk = lambda j: pl.BlockSpec((T, GROUP_W), lambda b, c: (b * nc + c, col0 + j))
    tab = pl.BlockSpec((T, GROUP_W), lambda b, c: (c, 0))
    fixed = lambda shape: pl.BlockSpec(shape, lambda b, c: (0,) * len(shape))
    state = lambda: pl.BlockSpec((1, GROUP_W, GROUP_W), lambda b, c: (b, 0, 0))
    return pl.pallas_call(
        _ret_body,
        grid=(B, nc),
        in_specs=[blk(0), blk(1), blk(2), blk(3), tab, tab, fixed((HT, HT)), fixed((HT, GROUP_W)),
                  fixed((HT, GROUP_W)), fixed((GROUP_W, GROUP_W)), state()],
        out_specs=[pl.BlockSpec((T, GROUP_W), lambda b, c: (b * nc + c, 0)), state()],
        out_shape=[jax.ShapeDtypeStruct((B * L, GROUP_W), F32), jax.ShapeDtypeStruct((B, GROUP_W, GROUP_W), F32)],
        scratch_shapes=[pltpu.VMEM((GROUP_W, GROUP_W), F32)],
        compiler_params=pltpu.CompilerParams(
            dimension_semantics=("arbitrary", "arbitrary"), vmem_limit_bytes=V7X_VMEM_LIMIT_BYTES),
        name="retention_mix",
    )(p, p, p, p, *tables, s0_bd)


def _layernorm(x, g, b):
    xf = _f32(x)
    mu = jnp.mean(xf, -1, keepdims=True)
    xc = xf - mu
    var = jnp.mean(xc * xc, -1, keepdims=True)
    return (xc * lax.rsqrt(var + LN_EPS) * _f32(g) + _f32(b)).astype(x.dtype)


def _rms(x, eps=NORM_EPS):
    return x * lax.rsqrt(jnp.mean(x * x, -1, keepdims=True) + eps)


def _head_ln(x, eps):
    xc = x - jnp.mean(x, -1, keepdims=True)
    return xc * lax.rsqrt(jnp.mean(xc * xc, -1, keepdims=True) + eps)


def _rotary(x, pos):
    half = x.shape[-1] // 2
    freq = 1.0 / (ROPE_BASE ** jnp.linspace(0.0, 1.0, half, dtype=jnp.float32))
    ang = _f32(pos)[:, None] * freq[None, :]
    cos = jnp.cos(ang)[None, :, None, :]
    sin = jnp.sin(ang)[None, :, None, :]
    x1, x2 = x[..., :half], x[..., half:]
    return jnp.concatenate([x1 * cos - x2 * sin, x1 * sin + x2 * cos], -1)


def _chunk_scan(step, xs, s0):
    B, L = xs[0].shape[:2]
    c = CHUNK if L % CHUNK == 0 else L
    n = L // c
    chunked = tuple(jnp.moveaxis(a.reshape(B, n, c, *a.shape[2:]), 1, 0) for a in xs)
    s, out = lax.scan(step, s0, chunked)
    return jnp.moveaxis(out, 0, 1).reshape(B, L, *out.shape[3:]), s


def _retention_step(S, xs, log_gamma):
    q, k, v = xs
    C = q.shape[1]
    t = jnp.arange(C, dtype=jnp.float32)
    gap = t[:, None] - t[None, :]
    dmask = jnp.where(gap >= 0, jnp.exp(jnp.maximum(gap, 0.0)[None] * log_gamma[:, None, None]), 0.0)
    scores = jnp.einsum('bqhd,bkhd->bhqk', q, k) * dmask[None]
    o = jnp.einsum('bhqk,bkhe->bqhe', scores, v)
    o = o + jnp.einsum('bqhd,bhde->bqhe', q, S) * jnp.exp((t + 1.0)[:, None] * log_gamma[None, :])[None, :, :, None]
    k_w = k * jnp.exp((C - 1.0 - t)[:, None] * log_gamma[None, :])[None, :, :, None]
    S = jnp.exp(C * log_gamma)[None, :, None, None] * S + jnp.einsum('bkhd,bkhe->bhde', k_w, v)
    return S, o


def _hgrn2_step(S, xs):
    q, k, logf, v = xs
    C = q.shape[1]
    b = jnp.cumsum(logf, axis=1)
    causal = jnp.tril(jnp.ones((C, C), bool))[None, :, :, None, None]
    diff = jnp.where(causal, b[:, :, None] - b[:, None, :], 0.0)
    decay = jnp.where(causal, jnp.exp(diff), 0.0)
    attn = jnp.einsum('bqhd,bqkhd,bkhd->bhqk', q, decay, k)
    o = jnp.einsum('bhqk,bkhe->bqhe', attn, v) + jnp.einsum('bqhd,bhde->bqhe', q * jnp.exp(b), S)
    b_last = b[:, -1]
    S = jnp.exp(b_last)[..., None] * S + jnp.einsum('bkhd,bkhe->bhde', k * jnp.exp(b_last[:, None] - b), v)
    return S, o


def _rwkv_scan(S0, r, w, k, v, a, b):
    def step(S, xs):
        r_t, w_t, k_t, v_t, a_t, b_t = xs
        S = (S * w_t[:, :, None, :]
             + jnp.einsum('bhvk,bhk->bhv', S, a_t)[..., None] * b_t[:, :, None, :]
             + v_t[..., None] * k_t[:, :, None, :])
        return S, jnp.einsum('bhvk,bhk->bhv', S, r_t)
    xs = tuple(jnp.moveaxis(t, 1, 0) for t in (r, w, k, v, a, b))
    S, y = lax.scan(step, S0, xs)
    return jnp.moveaxis(y, 0, 1), S


def _rwkv7(p_rw, prev_row, S0, l, W):
    B, L, _ = p_rw.shape
    p = _f32(p_rw)
    prev = jnp.concatenate([_f32(prev_row)[:, None], p[:, :-1]], axis=1)
    xs = p + (prev - p) * _f32(W['rw_mu'][l])
    r, k, v, wd, ad, gd = jnp.split(xs, RW_SPLITS, axis=-1)
    w_log = -jax.nn.softplus(-(_f32(W['rw_w0'][l]) + jnp.tanh(wd) @ _f32(W['rw_w_up'][l]))) - 0.5
    decay = jnp.exp(-jnp.exp(w_log))
    a = jax.nn.sigmoid(_f32(W['rw_a0'][l]) + ad @ _f32(W['rw_a_up'][l]))
    g = jax.nn.sigmoid(gd) @ _f32(W['rw_g_up'][l])
    hd = lambda t: t.reshape(B, L, N_HEADS, HEAD_DIM)
    kk = hd(k * _f32(W['rw_k_k'][l]))
    kk = kk / jnp.maximum(jnp.sqrt(jnp.sum(kk * kk, -1, keepdims=True)), 1e-12)
    k = hd(k * (1.0 + (a - 1.0) * _f32(W['rw_k_a'][l])))
    r, v, a_h = hd(r), hd(v), hd(a)
    y, S = _rwkv_scan(_f32(S0), r, hd(decay), k, v, -kk, kk * a_h)
    y = _head_ln(y, RW_LNX_EPS).reshape(B, L, GROUP_W) * _f32(W['rw_lnx_w'][l]) + _f32(W['rw_lnx_b'][l])
    bonus = (jnp.sum(r * k * _f32(W['rw_r_k'][l]), -1, keepdims=True) * v).reshape(B, L, GROUP_W)
    return (y + bonus) * g, S, p_rw[:, -1]


def _diff_attn_sample(q, k, v, k_past, v_past, lam):
    L = q.shape[1]
    P = k_past.shape[1]
    scale = DA_DQK ** -0.5
    s_past = jnp.einsum('bqhmd,bkhmd->bhmqk', q, k_past) * scale
    s_new = jnp.einsum('bqhmd,bkhmd->bhmqk', q, k) * scale
    s_new = jnp.where(jnp.tril(jnp.ones((L, L), bool)), s_new, MASK_NEG)
    p = jax.nn.softmax(jnp.concatenate([_f32(s_past), s_new], -1), axis=-1)
    pd = p[:, :, 0] - lam * p[:, :, 1]
    return (jnp.einsum('bhqk,bkhe->bqhe', pd[..., :P], v_past)
            + jnp.einsum('bhqk,bkhe->bqhe', pd[..., P:], v))


def _layer(l, x, pos, ret_s0, hg_s0, rw_s0, shift0, kv_past, W):
    B, L, _ = x.shape
    dt = x.dtype
    M = B * L
    proj2d, d_k, d_v = _in_proj(x.reshape(M, D_MODEL), W['w_in_bf16'][l], 256)
    proj = proj2d.reshape(B, L, N_COLS)
    rw_wts = _rwkv_weights(W, l)
    shift3 = _f32(shift0)[:, None, :]
    lb_rows = _hg_lb_rows(W['hg_lb'], l)
    hg_norm = jnp.tile(_f32(W['hg_norm_w'][l]), N_HEADS)
    ret_bd0, hg_bd0, rw_bd0 = _state_to_bd(_f32(ret_s0)), _hg_state_to_bd(_f32(hg_s0)), _state_to_bd(_f32(rw_s0))
    p_rw = proj[:, :, W_RET + W_HG:W_RET + W_HG + W_RW]

    lam_init = 0.8 - 0.6 * math.exp(-0.3 * l)
    lam = (jnp.exp(jnp.sum(_f32(W['da_lq1'][l]) * _f32(W['da_lk1'][l])))
           - jnp.exp(jnp.sum(_f32(W['da_lq2'][l]) * _f32(W['da_lk2'][l]))) + lam_init)
    c_da = W_RET + W_HG + W_RW
    if kv_past is None:
        assert L % HG_BLOCK == 0
        o_a, ret_bd = _ret_mix(proj2d, 0, ret_bd0, _ret_tables(pos, RET_CHUNK), B, L)
        o_b, hg_bd = _hgrn_mix(proj2d, W_RET, hg_bd0, lb_rows, hg_norm, l > 0, HG_BLOCK, B, L)
        o_c, rw_bd = _rwkv_mix(proj2d, RW_COL_BLOCK, shift3, rw_bd0, RW_CHUNK, B, L, rw_wts)
        ret_s, hg_s, rw_s = _bd_to_state(ret_bd), _hg_bd_to_state(hg_bd), _bd_to_state(rw_bd)
        norm_scale = jnp.tile(_f32(W['da_norm_w'][l]) * (1.0 - lam_init), N_HEADS)
        o_d = _attn_prompt(proj2d, lam, norm_scale, B, L)
    else:
        heads = lambda t, d: t.reshape(B, L, N_HEADS, d)
        r_q, r_k, r_v, r_g = jnp.split(proj[:, :, :W_RET], 4, axis=-1)
        log_gamma = jnp.log1p(-jnp.exp2(-5.0 - jnp.arange(N_HEADS, dtype=jnp.float32)))
        q = _rotary(heads(r_q, HEAD_DIM), pos)
        k = _rotary(heads(r_k, HEAD_DIM), pos) * HEAD_DIM ** -0.5
        o, ret_s = _chunk_scan(functools.partial(_retention_step, log_gamma=log_gamma),
                               (q, k, heads(r_v, HEAD_DIM)), _f32(ret_s0))
        o_a = (_rms(o).reshape(B, L, GROUP_W) * jax.nn.silu(r_g)).reshape(M, GROUP_W)

        h_q, h_f, h_i, h_g = jnp.split(proj[:, :, W_RET:W_RET + W_HG],
                                       [HG_QW, 2 * HG_QW, 2 * HG_QW + GROUP_W], axis=-1)
        if l == 0:
            log_f = jax.nn.log_sigmoid(h_f)
        else:
            log_f = jnp.logaddexp(lb_rows[0], lb_rows[1] + jax.nn.log_sigmoid(h_f))
        k_in = (lb_rows[2] if l > 0 else 1.0) * jax.nn.sigmoid(-h_f)
        q_h = jax.nn.silu(h_q) * HG_DK ** -0.5
        o, hg_s = _chunk_scan(_hgrn2_step, (heads(q_h, HG_DK), heads(k_in, HG_DK), heads(log_f, HG_DK),
                                            heads(h_i, HEAD_DIM)), _f32(hg_s0))
        o_b = ((_rms(o) * _f32(W['hg_norm_w'][l])).reshape(B, L, GROUP_W) * jax.nn.silu(h_g)).reshape(M, GROUP_W)

        o_c, rw_s, _ = _rwkv7(p_rw, shift0, rw_s0, l, W)
        o_c = o_c.reshape(M, GROUP_W)

        k_past, v_past = kv_past
        q5 = proj[:, :, c_da:c_da + GROUP_W].reshape(B, L, N_HEADS, 2, DA_DQK)
        o = _diff_attn_sample(q5, d_k.reshape(B, L, N_HEADS, 2, DA_DQK), d_v.reshape(B, L, N_HEADS, HEAD_DIM),
                              k_past.reshape(B, -1, N_HEADS, 2, DA_DQK), v_past, lam)
        o_d = (_rms(o) * _f32(W['da_norm_w'][l]) * (1.0 - lam_init)).reshape(M, GROUP_W)
    shift_new = p_rw[:, -1]

    x1 = _out_proj_ln((o_a, o_b, o_c, o_d), W['w_o_bf16'][l], x.reshape(M, D_MODEL),
                      W['ln1_g'][l], W['ln1_b'][l], 256)
    x = _moe_ln(x1, l, W).reshape(B, L, D_MODEL)
    new = (ret_s.astype(dt), hg_s.astype(dt), rw_s.astype(dt), shift_new,
           d_k.reshape(B, L, N_HEADS, HEAD_DIM), d_v.reshape(B, L, N_HEADS, HEAD_DIM))
    return x, new


def _prepare_weights(W):
    W = dict(W)
    depth = W['w_in'].shape[0]
    W['w_in_bf16'] = W['w_in'].astype(BF16)
    W['w_o_bf16'] = W['w_o'].astype(BF16)
    router_pad = jnp.zeros((depth, D_MODEL, 128 - N_EXPERTS), W['router_w'].dtype)
    W['gu_router_bf16'] = jnp.concatenate([W['sh_gate'], W['sh_up'], W['router_w'], router_pad], -1).astype(BF16)
    W['sh_down_bf16'] = W['sh_down'].astype(BF16)
    return W


def kernel(x_prompt, x_sample, state_ret, state_hgrn, state_rwkv, state_rwkv_shift, cache_k, cache_v,
           page_table, w_in, w_o, hg_lb, hg_norm_w, rw_mu, rw_w0, rw_w_up, rw_a0, rw_a_up, rw_g_up,
           rw_k_k, rw_k_a, rw_r_k, rw_lnx_w, rw_lnx_b, da_lq1, da_lk1, da_lq2, da_lk2, da_norm_w,
           ln1_g, ln1_b, router_w, router_bias, e_gate, e_up, e_down, sh_gate, sh_up, sh_down,
           ln2_g, ln2_b):
    W = {'w_in': w_in, 'w_o': w_o, 'hg_lb': hg_lb, 'hg_norm_w': hg_norm_w, 'rw_mu': rw_mu,
         'rw_w0': rw_w0, 'rw_w_up': rw_w_up, 'rw_a0': rw_a0, 'rw_a_up': rw_a_up, 'rw_g_up': rw_g_up,
         'rw_k_k': rw_k_k, 'rw_k_a': rw_k_a, 'rw_r_k': rw_r_k, 'rw_lnx_w': rw_lnx_w, 'rw_lnx_b': rw_lnx_b,
         'da_lq1': da_lq1, 'da_lk1': da_lk1, 'da_lq2': da_lq2, 'da_lk2': da_lk2, 'da_norm_w': da_norm_w,
         'ln1_g': ln1_g, 'ln1_b': ln1_b, 'router_w': router_w, 'router_bias': router_bias,
         'e_gate': e_gate, 'e_up': e_up, 'e_down': e_down, 'sh_gate': sh_gate, 'sh_up': sh_up,
         'sh_down': sh_down, 'ln2_g': ln2_g, 'ln2_b': ln2_b}
    W = _prepare_weights(W)
    B, S, _ = x_prompt.shape
    DB, L, _ = x_sample.shape
    past_len = page_table.shape[1] * PAGE_SIZE
    pos_p = jnp.arange(S)
    pos_s = past_len + jnp.arange(L)
    zero_ret = jnp.zeros((B, N_HEADS, HEAD_DIM, HEAD_DIM), jnp.float32)
    zero_hg = jnp.zeros((B, N_HEADS, HG_DK, HEAD_DIM), jnp.float32)
    zero_shift = jnp.zeros((B, W_RW), x_prompt.dtype)
    yp, ys = x_prompt, x_sample
    new_p, new_s = [], []
    for l in range(DEPTH):
        yp, st = _layer(l, yp, pos_p, zero_ret, zero_hg, zero_ret, zero_shift, None, W)
        new_p.append(st)
        k_past = cache_k[l][page_table].reshape(DB, past_len, N_HEADS, HEAD_DIM)
        v_past = cache_v[l][page_table].reshape(DB, past_len, N_HEADS, HEAD_DIM)
        ys, st = _layer(l, ys, pos_s, state_ret[l], state_hgrn[l], state_rwkv[l], state_rwkv_shift[l],
                        (k_past, v_past), W)
        new_s.append(st)

    def stk(sts, i):
        return jnp.stack([s[i] for s in sts])

    return (yp, ys, stk(new_p, 0), stk(new_s, 0), stk(new_p, 1), stk(new_s, 1), stk(new_p, 2), stk(new_s, 2),
            stk(new_p, 3), stk(new_s, 3), stk(new_p, 4), stk(new_p, 5), stk(new_s, 4), stk(new_s, 5))
```

```python
import math, functools
import jax, jax.numpy as jnp
from jax import lax
from jax.experimental import pallas as pl
from jax.experimental.pallas import tpu as pltpu

D_MODEL = 1024
DEPTH = 2
PAGE_SIZE = 128
N_MIXERS = 4
GROUP_W = D_MODEL // N_MIXERS
HEAD_DIM = 64
N_HEADS = GROUP_W // HEAD_DIM
ROPE_BASE = 10000.0
HG_DK = 128
RW_DECAY_LORA = 64
RW_A_LORA = 64
RW_GATE_LORA = 128
RW_LNX_EPS = 6.4e-4
DA_DQK = HEAD_DIM // 2
W_RET = 3 * N_HEADS * HEAD_DIM + GROUP_W
W_HG = 2 * N_HEADS * HG_DK + N_HEADS * HEAD_DIM + GROUP_W
W_RW = 3 * GROUP_W + RW_DECAY_LORA + RW_A_LORA + RW_GATE_LORA
W_DA = 2 * (N_HEADS * 2 * DA_DQK) + N_HEADS * HEAD_DIM
N_COLS = W_RET + W_HG + W_RW + W_DA
RW_SPLITS = [GROUP_W, 2 * GROUP_W, 3 * GROUP_W, 3 * GROUP_W + RW_DECAY_LORA, 3 * GROUP_W + RW_DECAY_LORA + RW_A_LORA]
CHUNK = 64
Q_BLOCK = 128
MASK_NEG = -1e30
N_EXPERTS = 64
TOP_K = 8
D_EXPERT = 256
ROUTED_SCALE = 2.5
MOE_BLOCK = 128
ALPHA = (2.0 * DEPTH) ** 0.25
LN_EPS = 1e-5
NORM_EPS = 1e-5

V7X_VMEM_LIMIT_BYTES = 56 * 1024 * 1024


def _f32(t):
    return t.astype(jnp.float32)


def _matmul_body(x_ref, w_ref, o_ref):
    o_ref[...] = jnp.dot(x_ref[...].astype(jnp.bfloat16), w_ref[...],
                         preferred_element_type=jnp.float32)


def _matmul(x, w_bf16, tm):
    M, K = x.shape
    N = w_bf16.shape[1]
    tm = min(tm, M)
    assert M % tm == 0
    return pl.pallas_call(
        _matmul_body,
        grid=(M // tm,),
        in_specs=[pl.BlockSpec((tm, K), lambda i: (i, 0)),
                  pl.BlockSpec((K, N), lambda i: (0, 0))],
        out_specs=pl.BlockSpec((tm, N), lambda i: (i, 0)),
        out_shape=jax.ShapeDtypeStruct((M, N), jnp.float32),
        compiler_params=pltpu.CompilerParams(
            dimension_semantics=("arbitrary",), vmem_limit_bytes=V7X_VMEM_LIMIT_BYTES),
        name="matmul",
    )(x, w_bf16)


def _in_proj_body(x_ref, w_ref, o_ref, k_ref, v_ref):
    acc = jnp.dot(x_ref[...].astype(jnp.bfloat16), w_ref[...], preferred_element_type=jnp.float32)
    o_ref[...] = acc
    k_ref[...] = acc[:, N_COLS - 2 * GROUP_W:N_COLS - GROUP_W]
    v_ref[...] = acc[:, N_COLS - GROUP_W:]


def _in_proj(x, w_bf16, tm):
    M, K = x.shape
    tm = min(tm, M)
    assert M % tm == 0
    row = lambda i: (i, 0)
    return pl.pallas_call(
        _in_proj_body,
        grid=(M // tm,),
        in_specs=[pl.BlockSpec((tm, K), row), pl.BlockSpec((K, N_COLS), lambda i: (0, 0))],
        out_specs=[pl.BlockSpec((tm, N_COLS), row), pl.BlockSpec((tm, GROUP_W), row),
                   pl.BlockSpec((tm, GROUP_W), row)],
        out_shape=[jax.ShapeDtypeStruct((M, N_COLS), jnp.float32),
                   jax.ShapeDtypeStruct((M, GROUP_W), jnp.float32),
                   jax.ShapeDtypeStruct((M, GROUP_W), jnp.float32)],
        compiler_params=pltpu.CompilerParams(
            dimension_semantics=("arbitrary",), vmem_limit_bytes=V7X_VMEM_LIMIT_BYTES),
        name="in_proj",
    )(x, w_bf16)


def _ln_rows(z, g, b):
    mu = jnp.mean(z, -1, keepdims=True)
    zc = z - mu
    var = jnp.mean(zc * zc, -1, keepdims=True)
    return zc * lax.rsqrt(var + LN_EPS) * g + b


def _out_proj_ln_body(a_ref, b_ref, c_ref, d_ref, w_ref, res_ref, g_ref, beta_ref, o_ref):
    mix = None
    for j, part in enumerate((a_ref, b_ref, c_ref, d_ref)):
        term = jnp.dot(part[...].astype(jnp.bfloat16), w_ref[j * GROUP_W:(j + 1) * GROUP_W, :],
                       preferred_element_type=jnp.float32)
        mix = term if mix is None else mix + term
    o_ref[...] = _ln_rows(ALPHA * res_ref[...] + mix, g_ref[...], beta_ref[...])


def _out_proj_ln(parts, w_bf16, res, g, b, tm):
    M, N = res.shape
    tm = min(tm, M)
    assert M % tm == 0
    row = lambda i: (i, 0)
    fixed = lambda i: (0, 0)
    part = pl.BlockSpec((tm, GROUP_W), row)
    return pl.pallas_call(
        _out_proj_ln_body,
        grid=(M // tm,),
        in_specs=[part, part, part, part, pl.BlockSpec((N, N), fixed), pl.BlockSpec((tm, N), row),
                  pl.BlockSpec((1, N), fixed), pl.BlockSpec((1, N), fixed)],
        out_specs=pl.BlockSpec((tm, N), row),
        out_shape=jax.ShapeDtypeStruct((M, N), jnp.float32),
        compiler_params=pltpu.CompilerParams(
            dimension_semantics=("arbitrary",), vmem_limit_bytes=V7X_VMEM_LIMIT_BYTES),
        name="out_proj_ln",
    )(*parts, w_bf16, res, g.reshape(1, N), b.reshape(1, N))


EXPERT_ROWS = 256


def _start_row_gather(idx_ref, n_rows, src_hbm, dst_at, sem):
    def body(pair, carry):
        for priority in range(2):
            r = pair * 2 + priority
            t = idx_ref[0, 0, r]
            pltpu.make_async_copy(src_hbm.at[pl.ds(t, 1)], dst_at(r), sem).start(priority=priority)
        return carry
    lax.fori_loop(0, n_rows // 2, body, 0, unroll=4)


def _moe_experts_body(n_blocks, blk_ref, exp_ref, lo_ref, hi_ref, first_ref, tok_ref, tok_next_ref, x_hbm,
                      wg_ref, wu_ref, wd_ref, y_ref, xbuf, sem):
    w = pl.program_id(0)
    blk = blk_ref[w]
    slot = blk % 2
    lo = lo_ref[w]
    hi = hi_ref[w]

    def gather(idx_ref, s):
        _start_row_gather(idx_ref, EXPERT_ROWS, x_hbm, lambda r: xbuf.at[s, pl.ds(r, 1)], sem.at[s])

    @pl.when(w == 0)
    def _():
        gather(tok_ref, 0)

    @pl.when(first_ref[w] == 1)
    def _():
        @pl.when(blk + 1 < n_blocks)
        def _():
            gather(tok_next_ref, 1 - slot)
        pltpu.make_async_copy(x_hbm.at[pl.ds(0, EXPERT_ROWS)], xbuf.at[slot], sem.at[slot]).wait()
        y_ref[...] = jnp.zeros_like(y_ref)

    @pl.when(hi > lo)
    def _():
        x = xbuf[slot].astype(jnp.bfloat16)
        g = jnp.dot(x, wg_ref[0].astype(jnp.bfloat16), preferred_element_type=jnp.float32)
        u = jnp.dot(x, wu_ref[0].astype(jnp.bfloat16), preferred_element_type=jnp.float32)
        h = (g * jax.nn.sigmoid(g) * u).astype(jnp.bfloat16)
        y = jnp.dot(h, wd_ref[0].astype(jnp.bfloat16), preferred_element_type=jnp.float32)
        r = lax.broadcasted_iota(jnp.int32, (EXPERT_ROWS, 1), 0)
        y_ref[...] += jnp.where((r >= lo) & (r < hi), y, 0.0)


def _moe_experts(x, tok_sorted, items, wg, wu, wd):
    M, D = x.shape
    A = tok_sorted.shape[0]
    assert A % EXPERT_ROWS == 0
    n_blocks = A // EXPERT_ROWS
    n_items = items[0].shape[0]
    tok3 = tok_sorted.reshape(n_blocks, 1, EXPERT_ROWS)
    smem_blk = lambda f: pl.BlockSpec((1, 1, EXPERT_ROWS), f, memory_space=pltpu.SMEM)
    w_spec = lambda shape: pl.BlockSpec(shape, lambda w, blk, e, *_: (e[w], 0, 0))
    grid_spec = pltpu.PrefetchScalarGridSpec(
        num_scalar_prefetch=5,
        grid=(n_items,),
        in_specs=[smem_blk(lambda w, blk, *_: (blk[w], 0, 0)),
                  smem_blk(lambda w, blk, *_: (jnp.minimum(blk[w] + 1, n_blocks - 1), 0, 0)),
                  pl.BlockSpec(memory_space=pl.ANY),
                  w_spec((1, D, D_EXPERT)), w_spec((1, D, D_EXPERT)), w_spec((1, D_EXPERT, D))],
        out_specs=pl.BlockSpec((EXPERT_ROWS, D), lambda w, blk, *_: (blk[w], 0)),
        scratch_shapes=[pltpu.VMEM((2, EXPERT_ROWS, D), jnp.float32), pltpu.SemaphoreType.DMA((2,))],
    )
    return pl.pallas_call(
        functools.partial(_moe_experts_body, n_blocks),
        grid_spec=grid_spec,
        out_shape=jax.ShapeDtypeStruct((A, D), jnp.float32),
        compiler_params=pltpu.CompilerParams(
            dimension_semantics=("arbitrary",), vmem_limit_bytes=V7X_VMEM_LIMIT_BYTES),
        name="moe_experts",
    )(*items, tok3, tok3, x, wg, wu, wd)


def _moe_work_items(e_sorted, A):
    n_blocks = A // EXPERT_ROWS
    experts = jnp.arange(N_EXPERTS, dtype=jnp.int32)
    group_end = jnp.sum((e_sorted[None, :] <= experts[:, None]).astype(jnp.int32), axis=1)
    starts = jnp.sort(jnp.concatenate([jnp.arange(n_blocks, dtype=jnp.int32) * EXPERT_ROWS, group_end]))
    ends = jnp.concatenate([starts[1:], jnp.full((1,), A, jnp.int32)])
    blk = jnp.minimum(starts // EXPERT_ROWS, n_blocks - 1)
    expert = jnp.minimum(jnp.sum((group_end[None, :] <= starts[:, None]).astype(jnp.int32), axis=1), N_EXPERTS - 1)
    new_start = jnp.concatenate([jnp.ones((1,), bool), starts[1:] != starts[:-1]])
    first = ((starts % EXPERT_ROWS == 0) & (starts < A) & new_start).astype(jnp.int32)
    base = blk * EXPERT_ROWS
    return blk, expert, starts - base, ends - base, first


COMBINE_ROWS = 128


def _moe_combine_body(pos_ref, pos_next_ref, wts_ref, x_ref, gu_ref, shd_ref, g_ref, b_ref, y_hbm,
                      o_ref, buf, sem):
    i = pl.program_id(0)
    n = pl.num_programs(0)
    slot = i % 2
    n_rows = COMBINE_ROWS * TOP_K

    def gather(idx_ref, s):
        def dst(r):
            k = jnp.bitwise_and(r, TOP_K - 1)
            token = lax.shift_right_logical(r, TOP_K.bit_length() - 1)
            return buf.at[s, pl.ds(k * COMBINE_ROWS + token, 1)]
        _start_row_gather(idx_ref, n_rows, y_hbm, dst, sem.at[s])

    @pl.when(i == 0)
    def _():
        gather(pos_ref, 0)

    @pl.when(i + 1 < n)
    def _():
        gather(pos_next_ref, 1 - slot)

    pltpu.make_async_copy(y_hbm.at[pl.ds(0, n_rows)], buf.at[slot], sem.at[slot]).wait()
    wts = wts_ref[...]
    routed = wts[:, 0:1] * buf[slot, pl.ds(0, COMBINE_ROWS)]
    for k in range(1, TOP_K):
        routed = routed + wts[:, k:k + 1] * buf[slot, pl.ds(k * COMBINE_ROWS, COMBINE_ROWS)]
    gu = gu_ref[...]
    g = gu[:, :D_EXPERT]
    u = gu[:, D_EXPERT:]
    h = (g * jax.nn.sigmoid(g) * u).astype(jnp.bfloat16)
    shared = jnp.dot(h, shd_ref[...], preferred_element_type=jnp.float32)
    o_ref[...] = _ln_rows(ALPHA * x_ref[...] + (routed + shared), g_ref[...], b_ref[...])


def _moe_combine(y_sorted, inv_pos, wts, x, gu, shd_bf16, g, b):
    M, D = x.shape
    tt = min(COMBINE_ROWS, M)
    assert tt == COMBINE_ROWS and M % tt == 0
    n_tiles = M // tt
    pos3 = inv_pos.reshape(n_tiles, 1, tt * TOP_K)
    row = lambda i: (i, 0)
    fixed = lambda i: (0, 0)
    smem_blk = lambda f: pl.BlockSpec((1, 1, tt * TOP_K), f, memory_space=pltpu.SMEM)
    return pl.pallas_call(
        _moe_combine_body,
        grid=(n_tiles,),
        in_specs=[smem_blk(lambda i: (i, 0, 0)),
                  smem_blk(lambda i: (jnp.minimum(i + 1, n_tiles - 1), 0, 0)),
                  pl.BlockSpec((tt, TOP_K), row),
                  pl.BlockSpec((tt, D), row),
                  pl.BlockSpec((tt, 2 * D_EXPERT), row),
                  pl.BlockSpec((D_EXPERT, D), fixed),
                  pl.BlockSpec((1, D), fixed), pl.BlockSpec((1, D), fixed),
                  pl.BlockSpec(memory_space=pl.ANY)],
        out_specs=pl.BlockSpec((tt, D), row),
        out_shape=jax.ShapeDtypeStruct((M, D), jnp.float32),
        scratch_shapes=[pltpu.VMEM((2, tt * TOP_K, D), jnp.float32), pltpu.SemaphoreType.DMA((2,))],
        compiler_params=pltpu.CompilerParams(
            dimension_semantics=("arbitrary",), vmem_limit_bytes=V7X_VMEM_LIMIT_BYTES),
        name="moe_combine_ln",
    )(pos3, pos3, wts, x, gu, shd_bf16, g.reshape(1, D), b.reshape(1, D), y_sorted)


def _moe_ln(x, l, W):
    M, D = x.shape
    rgu = _matmul(x, W['gu_router_bf16'][l], 256)
    scores = jax.nn.sigmoid(rgu[:, 2 * D_EXPERT:2 * D_EXPERT + N_EXPERTS])
    _, idx = lax.top_k(scores + _f32(W['router_bias'][l]), TOP_K)
    wts = jnp.take_along_axis(scores, idx, axis=-1)
    wts = wts / jnp.sum(wts, -1, keepdims=True) * ROUTED_SCALE

    A = M * TOP_K
    flat_e = idx.reshape(-1).astype(jnp.int32)
    assign = jnp.arange(A, dtype=jnp.int32)
    e_sorted, order = lax.sort_key_val(flat_e, assign)
    _, inv_pos = lax.sort_key_val(order, assign)
    tok_sorted = lax.shift_right_logical(order, TOP_K.bit_length() - 1)
    items = _moe_work_items(e_sorted, A)

    y_sorted = _moe_experts(x, tok_sorted, items, W['e_gate'][l], W['e_up'][l], W['e_down'][l])
    return _moe_combine(y_sorted, inv_pos, wts, x, rgu, W['sh_down_bf16'][l], W['ln2_g'][l], W['ln2_b'][l])


DA_COL_BLOCK = (W_RET + W_HG + W_RW) // GROUP_W
ATTN_BLOCK = 256
LOG2E = 1.4426950408889634
BF16 = jnp.bfloat16
F32 = jnp.float32


ACC_W = 128


def _attn_prompt_body(lam_ref, q_ref, k_ref, v_ref, nw_ref, o_ref, kbf, vh, qs, m_scr, acc_scr):
    i = pl.program_id(1)
    T = ATTN_BLOCK
    n_hm = 2 * N_HEADS
    S = k_ref.shape[0]

    @pl.when(i == 0)
    def _():
        kbf[...] = k_ref[...].astype(BF16)
        v = v_ref[...]
        ones_col = (lax.broadcasted_iota(jnp.int32, (S, ACC_W - HEAD_DIM), 1) == 0).astype(F32)
        for h in range(N_HEADS):
            vh[h] = jnp.concatenate([v[:, h * HEAD_DIM:(h + 1) * HEAD_DIM], ones_col], -1).astype(BF16)

    q = q_ref[...]
    lane = lax.broadcasted_iota(jnp.int32, (T, GROUP_W), 1)
    for hm in range(n_hm):
        keep = (lane >= hm * DA_DQK) & (lane < (hm + 1) * DA_DQK)
        qs[pl.ds(hm * T, T), :] = jnp.where(keep, q, 0.0).astype(BF16)
    m_scr[...] = jnp.full_like(m_scr, MASK_NEG)
    acc_scr[...] = jnp.zeros_like(acc_scr)
    score_scale = DA_DQK ** -0.5 * LOG2E

    def kv_block(j, masked):
        kb = kbf[pl.ds(j * T, T), :]
        for h in range(N_HEADS):
            rows = pl.ds(h * 2 * T, 2 * T)
            s = lax.dot_general(qs[rows, :], kb, (((1,), (1,)), ((), ())), preferred_element_type=F32)
            s = s * score_scale
            if masked:
                qpos = lax.broadcasted_iota(jnp.int32, (2 * T, T), 0) % T
                kpos = lax.broadcasted_iota(jnp.int32, (2 * T, T), 1)
                s = jnp.where(kpos <= qpos, s, MASK_NEG)
            m_old = m_scr[rows, :]
            m_new = jnp.maximum(m_old, jnp.max(s, -1, keepdims=True))
            alpha = jnp.exp2(m_old - m_new)
            p = jnp.exp2(s - jnp.concatenate([m_new] * (T // ACC_W), -1))
            pv = jnp.dot(p.astype(BF16), vh[h, pl.ds(j * T, T), :], preferred_element_type=F32)
            acc_scr[rows, :] = alpha * acc_scr[rows, :] + pv
            m_scr[rows, :] = m_new

    def body(j, c):
        kv_block(j, False)
        return c
    lax.fori_loop(0, i, body, 0)
    kv_block(i, True)

    lam = lam_ref[0]
    outs = []
    for h in range(N_HEADS):
        a0 = acc_scr[pl.ds(h * 2 * T, T), :]
        a1 = acc_scr[pl.ds(h * 2 * T + T, T), :]
        o = (a0[:, :HEAD_DIM] / a0[:, HEAD_DIM:HEAD_DIM + 1]
             - lam * (a1[:, :HEAD_DIM] / a1[:, HEAD_DIM:HEAD_DIM + 1]))
        outs.append(o * lax.rsqrt(jnp.mean(o * o, -1, keepdims=True) + NORM_EPS))
    o_ref[...] = jnp.concatenate(outs, -1) * nw_ref[...]


def _attn_prompt(proj, lam, norm_scale, B, S):
    T = ATTN_BLOCK
    assert S % T == 0
    nq = S // T
    n_rows = 2 * N_HEADS * T
    return pl.pallas_call(
        _attn_prompt_body,
        grid=(B, nq),
        in_specs=[pl.BlockSpec(memory_space=pltpu.SMEM),
                  pl.BlockSpec((T, GROUP_W), lambda b, i: (b * nq + i, DA_COL_BLOCK)),
                  pl.BlockSpec((S, GROUP_W), lambda b, i: (b, DA_COL_BLOCK + 1)),
                  pl.BlockSpec((S, GROUP_W), lambda b, i: (b, DA_COL_BLOCK + 2)),
                  pl.BlockSpec((1, GROUP_W), lambda b, i: (0, 0))],
        out_specs=pl.BlockSpec((T, GROUP_W), lambda b, i: (b * nq + i, 0)),
        out_shape=jax.ShapeDtypeStruct((B * S, GROUP_W), F32),
        scratch_shapes=[pltpu.VMEM((S, GROUP_W), BF16),
                        pltpu.VMEM((N_HEADS, S, ACC_W), BF16),
                        pltpu.VMEM((n_rows, GROUP_W), BF16),
                        pltpu.VMEM((n_rows, ACC_W), F32),
                        pltpu.VMEM((n_rows, ACC_W), F32)],
        compiler_params=pltpu.CompilerParams(
            dimension_semantics=("arbitrary", "arbitrary"), vmem_limit_bytes=V7X_VMEM_LIMIT_BYTES),
        name="diff_attn_prompt",
    )(lam.reshape(1), proj, proj, proj, norm_scale.reshape(1, GROUP_W))


RW_COL_BLOCK = (W_RET + W_HG) // GROUP_W
RW_CHUNK = 64


def _dot(a, b):
    return jnp.dot(a, b, preferred_element_type=F32)


def _dot_nt(a, b):
    return lax.dot_general(a, b, (((1,), (1,)), ((), ())), preferred_element_type=F32)


def _dot_tn(a, b):
    return lax.dot_general(a, b, (((0,), (0,)), ((), ())), preferred_element_type=F32)


def _split3(x):
    hi = x.astype(BF16)
    r1 = x - hi.astype(F32)
    mid = r1.astype(BF16)
    lo = (r1 - mid.astype(F32)).astype(BF16)
    return hi, mid, lo


def _dot_exact_rhs(x, m_bf16):
    hi, mid, lo = _split3(x)
    return _dot(hi, m_bf16) + _dot(mid, m_bf16) + _dot(lo, m_bf16)


def _dot_exact_lhs(m_bf16, x):
    hi, mid, lo = _split3(x)
    return _dot(m_bf16, hi) + _dot(m_bf16, mid) + _dot(m_bf16, lo)


def _hi_lo(x):
    hi = x.astype(BF16)
    return hi, (x - hi.astype(F32)).astype(BF16)


def _mm3(a_hl, b_hl, dot):
    (ah, al), (bh, bl) = a_hl, b_hl
    return dot(ah, bh) + dot(ah, bl) + dot(al, bh)


def _rwkv_body(n_valid, pr_ref, pk_ref, pv_ref, pl_ref, shift0_ref, s0_ref, mu_ref, w0_ref, wup_ref, a0_ref,
               aup_ref, gup_ref, kk_ref, ka_ref, rk_ref, lnw_ref, lnb_ref, o_ref, s_out_ref, sbd, carry):
    c_idx = pl.program_id(1)
    T = RW_CHUNK
    HT = N_HEADS * T

    @pl.when(c_idx == 0)
    def _():
        sbd[...] = s0_ref[0]
        carry[...] = shift0_ref[0]

    row = lax.broadcasted_iota(jnp.int32, (T, GROUP_W), 0)
    lane_head = lax.broadcasted_iota(jnp.int32, (T, GROUP_W), 1) // HEAD_DIM

    def shifted(p_ref, blk):
        cols = slice(blk * GROUP_W, (blk + 1) * GROUP_W)
        p = p_ref[...]
        prev = jnp.where(row == 0, carry[:, cols], pltpu.roll(p, 1, axis=0))
        carry[:, cols] = p[T - 1:T, :]
        return p + (prev - p) * mu_ref[:, cols]

    xr = shifted(pr_ref, 0)
    xk = shifted(pk_ref, 1)
    xv = shifted(pv_ref, 2)
    xl = shifted(pl_ref, 3)

    gi = lax.broadcasted_iota(jnp.int32, (GROUP_W, GROUP_W), 0) // HEAD_DIM
    gj = lax.broadcasted_iota(jnp.int32, (GROUP_W, GROUP_W), 1) // HEAD_DIM
    seg = (gi == gj).astype(BF16)

    z = -(w0_ref[...] + _dot(jnp.tanh(xl).astype(BF16), wup_ref[...]))
    softplus = jnp.maximum(z, 0.0) + jnp.log(1.0 + jnp.exp(-jnp.abs(z)))
    lw = -jnp.exp(-softplus - 0.5)
    a = jax.nn.sigmoid(a0_ref[...] + _dot(xl.astype(BF16), aup_ref[...]))
    g = _dot(jax.nn.sigmoid(xl).astype(BF16), gup_ref[...])
    kk = xk * kk_ref[...]
    kkn = kk / jnp.maximum(jnp.sqrt(_dot_exact_rhs(kk * kk, seg)), 1e-12)
    k_mod = xk * (1.0 + (a - 1.0) * ka_ref[...])
    an = -kkn
    bb = kkn * a
    if n_valid < T:
        valid = row < n_valid
        lw = jnp.where(valid, lw, 0.0)
        an = jnp.where(valid, an, 0.0)
        k_mod = jnp.where(valid, k_mod, 0.0)

    ti = lax.broadcasted_iota(jnp.int32, (T, T), 0)
    tj = lax.broadcasted_iota(jnp.int32, (T, T), 1)
    c = _dot_exact_lhs((tj <= ti).astype(BF16), lw)
    c_last = c[T - 1:T, :]
    inv_dec = jnp.exp(-c)
    to_end = jnp.exp(c_last - c)
    a_t = an * jnp.exp(c - lw)
    b_t = bb * inv_dec
    k_t = k_mod * inv_dec
    r_t = xr * jnp.exp(c)

    def stack(x):
        return jnp.concatenate([jnp.where(lane_head == h, x, 0.0) for h in range(N_HEADS)], axis=0)

    def tile(x):
        return jnp.concatenate([x] * N_HEADS, axis=0)

    ri = lax.broadcasted_iota(jnp.int32, (HT, HT), 0)
    ci = lax.broadcasted_iota(jnp.int32, (HT, HT), 1)
    same_head = (ri // T) == (ci // T)
    m_strict = same_head & ((ci % T) < (ri % T))
    m_incl = same_head & ((ci % T) <= (ri % T))

    a_p = _hi_lo(stack(a_t))
    r_p = _hi_lo(stack(r_t))
    b_p = _hi_lo(tile(b_t))
    k_p = _hi_lo(tile(k_t))
    a_ab = jnp.where(m_strict, _mm3(a_p, b_p, _dot_nt), 0.0)
    a_ak = jnp.where(m_strict, _mm3(a_p, k_p, _dot_nt), 0.0)
    a_rb = jnp.where(m_incl, _mm3(r_p, b_p, _dot_nt), 0.0)
    a_rk = jnp.where(m_incl, _mm3(r_p, k_p, _dot_nt), 0.0)

    s_old = sbd[...]
    s_p = _hi_lo(s_old)
    v_p = _hi_lo(stack(xv))

    u = _mm3(a_p, s_p, _dot_nt) + _mm3(_hi_lo(a_ak), v_p, _dot)
    pw = a_ab
    n_steps = max(1, (T - 1).bit_length())
    for step in range(n_steps):
        pw_p = _hi_lo(pw)
        u = u + _mm3(pw_p, _hi_lo(u), _dot)
        if step + 1 < n_steps:
            pw = _mm3(pw_p, pw_p, _dot)

    u_p = _hi_lo(u)
    y_s = _mm3(r_p, s_p, _dot_nt) + _mm3(_hi_lo(a_rb), u_p, _dot) + _mm3(_hi_lo(a_rk), v_p, _dot)
    y = y_s[0:T]
    for h in range(1, N_HEADS):
        y = y + y_s[h * T:(h + 1) * T]

    s_new = (s_old * jnp.exp(c_last)
             + _mm3(u_p, _hi_lo(stack(bb * to_end)), _dot_tn) + _mm3(v_p, _hi_lo(stack(k_mod * to_end)), _dot_tn))
    sbd[...] = s_new

    @pl.when(c_idx == pl.num_programs(1) - 1)
    def _():
        s_out_ref[0] = s_new

    inv_n = 1.0 / HEAD_DIM
    yc = y - _dot_exact_rhs(y, seg) * inv_n
    var = _dot_exact_rhs(yc * yc, seg) * inv_n
    yn = yc * lax.rsqrt(var + RW_LNX_EPS) * lnw_ref[...] + lnb_ref[...]
    bonus = _dot_exact_rhs(xr * k_mod * rk_ref[...], seg) * xv
    o_ref[...] = (yn + bonus) * g


def _rwkv_mix(p, col_block0, shift0, s0_bd, n_valid, B, L, wts):
    T = RW_CHUNK
    assert L % T == 0
    nc = L // T
    blk = lambda j: pl.BlockSpec((T, GROUP_W), lambda b, c: (b * nc + c, col_block0 + j))
    fixed = lambda shape: pl.BlockSpec(shape, lambda b, c: (0,) * len(shape))
    vec = fixed((1, GROUP_W))
    mat = fixed((GROUP_W, GROUP_W))
    return pl.pallas_call(
        functools.partial(_rwkv_body, n_valid),
        grid=(B, nc),
        in_specs=[blk(0), blk(1), blk(2), blk(3),
                  pl.BlockSpec((1, 1, W_RW), lambda b, c: (b, 0, 0)),
                  pl.BlockSpec((1, GROUP_W, GROUP_W), lambda b, c: (b, 0, 0)),
                  fixed((1, W_RW)), vec, mat, vec, mat, mat, vec, vec, vec, vec, vec],
        out_specs=[pl.BlockSpec((T, GROUP_W), lambda b, c: (b * nc + c, 0)),
                   pl.BlockSpec((1, GROUP_W, GROUP_W), lambda b, c: (b, 0, 0))],
        out_shape=[jax.ShapeDtypeStruct((B * L, GROUP_W), F32),
                   jax.ShapeDtypeStruct((B, GROUP_W, GROUP_W), F32)],
        scratch_shapes=[pltpu.VMEM((GROUP_W, GROUP_W), F32), pltpu.VMEM((1, W_RW), F32)],
        compiler_params=pltpu.CompilerParams(
            dimension_semantics=("arbitrary", "arbitrary"), vmem_limit_bytes=V7X_VMEM_LIMIT_BYTES),
        name="rwkv7_mix",
    )(p, p, p, p, shift0, s0_bd, *wts)


def _rwkv_weights(W, l):
    z = lambda n: jnp.zeros((n, GROUP_W), F32)
    wup = jnp.concatenate([_f32(W['rw_w_up'][l]), z(GROUP_W - RW_DECAY_LORA)], 0).astype(BF16)
    aup = jnp.concatenate([z(RW_DECAY_LORA), _f32(W['rw_a_up'][l]), z(RW_GATE_LORA)], 0).astype(BF16)
    gup = jnp.concatenate([z(RW_DECAY_LORA + RW_A_LORA), _f32(W['rw_g_up'][l])], 0).astype(BF16)
    r1 = lambda t: _f32(t).reshape(1, -1)
    return (r1(W['rw_mu'][l]), r1(W['rw_w0'][l]), wup, r1(W['rw_a0'][l]), aup, gup, r1(W['rw_k_k'][l]),
            r1(W['rw_k_a'][l]), r1(W['rw_r_k'][l]), r1(W['rw_lnx_w'][l]), r1(W['rw_lnx_b'][l]))


def _state_to_bd(s):
    eye = jnp.eye(N_HEADS, dtype=s.dtype)
    return (s[:, :, :, None, :] * eye[None, :, None, :, None]).reshape(s.shape[0], GROUP_W, GROUP_W)


def _bd_to_state(sbd):
    s5 = sbd.reshape(sbd.shape[0], N_HEADS, HEAD_DIM, N_HEADS, HEAD_DIM)
    return jnp.stack([s5[:, h, :, h, :] for h in range(N_HEADS)], axis=1)


def _head_seg_ones():
    si = lax.broadcasted_iota(jnp.int32, (GROUP_W, GROUP_W), 0) // HEAD_DIM
    sj = lax.broadcasted_iota(jnp.int32, (GROUP_W, GROUP_W), 1) // HEAD_DIM
    return (si == sj).astype(BF16)


HG_QW = N_HEADS * HG_DK
HG_BLOCK = 128
HG_SUB = 16


def _hgrn_body(has_lb, n_valid, hq_ref, hf_ref, hi_ref, hg_ref, s0_ref, lb_ref, nw_ref, o_ref, s_out_ref,
               st, qt_s, kh_s, v_s, dec_s, oi_s):
    c_idx = pl.program_id(1)
    TB, C = HG_BLOCK, HG_SUB

    @pl.when(c_idx == 0)
    def _():
        st[...] = s0_ref[0]

    hf = hf_ref[...]
    log_sig = jnp.minimum(hf, 0.0) - jnp.log(1.0 + jnp.exp(-jnp.abs(hf)))
    if has_lb:
        log_lb = lb_ref[0:1, :]
        t2 = lb_ref[1:2, :] + log_sig
        lf = jnp.maximum(log_lb, t2) + jnp.log(1.0 + jnp.exp(-jnp.abs(log_lb - t2)))
        k_in = lb_ref[2:3, :] * jax.nn.sigmoid(-hf)
    else:
        lf = log_sig
        k_in = jax.nn.sigmoid(-hf)
    hq = hq_ref[...]
    q = hq * jax.nn.sigmoid(hq) * (HG_DK ** -0.5)
    v = hi_ref[...]
    row = lax.broadcasted_iota(jnp.int32, (TB, HG_QW), 0)
    if n_valid < TB:
        lf = jnp.where(row < n_valid, lf, 0.0)
        k_in = jnp.where(row < n_valid, k_in, 0.0)

    ti = lax.broadcasted_iota(jnp.int32, (TB, TB), 0)
    tj = lax.broadcasted_iota(jnp.int32, (TB, TB), 1)
    same = (ti // C) == (tj // C)
    b = _dot_exact_lhs((same & (tj <= ti)).astype(BF16), lf)
    b_last = _dot_exact_lhs(same.astype(BF16), lf)

    gi = lax.broadcasted_iota(jnp.int32, (HG_QW, GROUP_W), 0) // HG_DK
    gj = lax.broadcasted_iota(jnp.int32, (HG_QW, GROUP_W), 1) // HEAD_DIM
    head_sum = (gi == gj).astype(BF16)

    off = row % C
    off_v = lax.broadcasted_iota(jnp.int32, (TB, GROUP_W), 0) % C
    ps = [(q * k_in).astype(BF16)]
    vs = [v]
    for d in range(1, C):
        ok = off >= d
        e = jnp.exp(jnp.where(ok, b - pltpu.roll(b, d, axis=0), 0.0))
        ps.append(jnp.where(ok, q * e * pltpu.roll(k_in, d, axis=0), 0.0).astype(BF16))
        vs.append(jnp.where(off_v >= d, pltpu.roll(v, d, axis=0), 0.0))
    attn = _dot(jnp.concatenate(ps, axis=0), head_sum)
    o_intra = attn[0:TB] * vs[0]
    for d in range(1, C):
        o_intra = o_intra + attn[d * TB:(d + 1) * TB] * vs[d]

    qt_s[...] = (q * jnp.exp(b)).astype(BF16)
    kh_s[...] = (k_in * jnp.exp(b_last - b)).astype(BF16)
    v_s[...] = v.astype(BF16)
    dec_s[...] = jnp.exp(b_last)

    bi = lax.broadcasted_iota(jnp.int32, (GROUP_W, HG_QW), 0) // HEAD_DIM
    bj = lax.broadcasted_iota(jnp.int32, (GROUP_W, HG_QW), 1) // HG_DK
    block_diag = bi == bj

    def group(s, carry):
        start = pl.multiple_of(s * C, C)
        rows = pl.ds(start, C)
        s_cur = st[...]
        oi_s[rows, :] = _dot_nt(qt_s[rows, :], s_cur.astype(BF16))
        upd = _dot_tn(v_s[rows, :], kh_s[rows, :])
        st[...] = s_cur * dec_s[pl.ds(start, 1), :] + jnp.where(block_diag, upd, 0.0)
        return carry
    lax.fori_loop(0, TB // C, group, 0)

    @pl.when(c_idx == pl.num_programs(1) - 1)
    def _():
        s_out_ref[0] = st[...]

    o = o_intra + oi_s[...]
    ms = _dot_exact_rhs(o * o, _head_seg_ones()) * (1.0 / HEAD_DIM)
    hg = hg_ref[...]
    o_ref[...] = o * lax.rsqrt(ms + NORM_EPS) * nw_ref[...] * (hg * jax.nn.sigmoid(hg))


def _hgrn_mix(p, col0, s0_bd, lb_rows, norm_w4, has_lb, n_valid, B, L):
    TB = HG_BLOCK
    assert L % TB == 0 and col0 % HG_QW == 0
    nc = L // TB
    qb, gb = col0 // HG_QW, (col0 + 2 * HG_QW) // GROUP_W
    fixed = lambda shape: pl.BlockSpec(shape, lambda b, c: (0,) * len(shape))
    rows = lambda width, j: pl.BlockSpec((TB, width), lambda b, c: (b * nc + c, j))
    return pl.pallas_call(
        functools.partial(_hgrn_body, has_lb, n_valid),
        grid=(B, nc),
        in_specs=[rows(HG_QW, qb), rows(HG_QW, qb + 1), rows(GROUP_W, gb), rows(GROUP_W, gb + 1),
                  pl.BlockSpec((1, GROUP_W, HG_QW), lambda b, c: (b, 0, 0)),
                  fixed((8, HG_QW)), fixed((1, GROUP_W))],
        out_specs=[rows(GROUP_W, 0), pl.BlockSpec((1, GROUP_W, HG_QW), lambda b, c: (b, 0, 0))],
        out_shape=[jax.ShapeDtypeStruct((B * L, GROUP_W), F32),
                   jax.ShapeDtypeStruct((B, GROUP_W, HG_QW), F32)],
        scratch_shapes=[pltpu.VMEM((GROUP_W, HG_QW), F32), pltpu.VMEM((TB, HG_QW), BF16),
                        pltpu.VMEM((TB, HG_QW), BF16), pltpu.VMEM((TB, GROUP_W), BF16),
                        pltpu.VMEM((TB, HG_QW), F32), pltpu.VMEM((TB, GROUP_W), F32)],
        compiler_params=pltpu.CompilerParams(
            dimension_semantics=("arbitrary", "arbitrary"), vmem_limit_bytes=V7X_VMEM_LIMIT_BYTES),
        name="hgrn2_mix",
    )(p, p, p, p, s0_bd, lb_rows, norm_w4.reshape(1, GROUP_W))


def _hg_state_to_bd(s):
    eye = jnp.eye(N_HEADS, dtype=s.dtype)
    st = jnp.swapaxes(s, 2, 3)
    return (st[:, :, :, None, :] * eye[None, :, None, :, None]).reshape(s.shape[0], GROUP_W, HG_QW)


def _hg_bd_to_state(sbd):
    s5 = sbd.reshape(sbd.shape[0], N_HEADS, HEAD_DIM, N_HEADS, HG_DK)
    return jnp.swapaxes(jnp.stack([s5[:, h, :, h, :] for h in range(N_HEADS)], axis=1), 2, 3)


def _hg_lb_rows(hg_lb, l):
    lb_soft = jax.nn.softmax(_f32(hg_lb), axis=0)
    lb = (jnp.cumsum(lb_soft, axis=0) - lb_soft[0])[l]
    z = jnp.zeros_like(lb)
    if l == 0:
        return jnp.stack([z] * 8)
    return jnp.stack([jnp.log(lb), jnp.log1p(-lb), 1.0 - lb, z, z, z, z, z])


RET_CHUNK = 64


def _ret_body(q_ref, k_ref, v_ref, g_ref, cos_ref, sin_ref, dmask_ref, qdec_ref, kdec_ref, sdec_ref, s0_ref,
              o_ref, s_out_ref, st):
    c_idx = pl.program_id(1)
    T = RET_CHUNK

    @pl.when(c_idx == 0)
    def _():
        st[...] = s0_ref[0]

    lane = lax.broadcasted_iota(jnp.int32, (T, GROUP_W), 1)
    upper_half = (lane % HEAD_DIM) >= (HEAD_DIM // 2)
    lane_head = lane // HEAD_DIM
    cos = cos_ref[...]
    sin = sin_ref[...]

    def rotary(x):
        swapped = jnp.where(upper_half, pltpu.roll(x, HEAD_DIM // 2, axis=1),
                            pltpu.roll(x, GROUP_W - HEAD_DIM // 2, axis=1))
        return x * cos + swapped * sin

    def stack(x):
        return jnp.concatenate([jnp.where(lane_head == h, x, 0.0) for h in range(N_HEADS)], axis=0)

    q = rotary(q_ref[...])
    k = rotary(k_ref[...]) * (HEAD_DIM ** -0.5)
    q_s = stack(q).astype(BF16)
    k_tl = jnp.concatenate([k] * N_HEADS, axis=0).astype(BF16)
    v_m = stack(v_ref[...]).astype(BF16)
    scores = _dot_nt(q_s, k_tl) * dmask_ref[...]
    s_old = st[...]
    o_s = _dot(scores.astype(BF16), v_m) + _dot(q_s, s_old.astype(BF16)) * qdec_ref[...]
    o = o_s[0:T]
    for h in range(1, N_HEADS):
        o = o + o_s[h * T:(h + 1) * T]
    k_w = (stack(k) * kdec_ref[...]).astype(BF16)
    s_new = s_old * sdec_ref[...] + _dot_tn(k_w, v_m)
    st[...] = s_new

    @pl.when(c_idx == pl.num_programs(1) - 1)
    def _():
        s_out_ref[0] = s_new

    ms = _dot_exact_rhs(o * o, _head_seg_ones()) * (1.0 / HEAD_DIM)
    g = g_ref[...]
    o_ref[...] = o * lax.rsqrt(ms + NORM_EPS) * (g * jax.nn.sigmoid(g))


def _ret_tables(pos, n_valid):
    T = RET_CHUNK
    half = HEAD_DIM // 2
    freq = 1.0 / (ROPE_BASE ** jnp.linspace(0.0, 1.0, half, dtype=F32))
    ang = _f32(pos)[:, None] * freq[None, :]
    cos = jnp.tile(jnp.cos(ang), (1, 2 * N_HEADS))
    sin = jnp.tile(jnp.concatenate([-jnp.sin(ang), jnp.sin(ang)], -1), (1, N_HEADS))
    log_gamma = jnp.log1p(-jnp.exp2(-5.0 - jnp.arange(N_HEADS, dtype=F32)))
    t = jnp.arange(T, dtype=F32)
    gap = t[:, None] - t[None, :]
    dm = jnp.where(gap >= 0, jnp.exp(jnp.maximum(gap, 0.0)[None] * log_gamma[:, None, None]), 0.0)
    eye = jnp.eye(N_HEADS, dtype=F32)
    dmask = (dm[:, :, None, :] * eye[:, None, :, None]).reshape(N_HEADS * T, N_HEADS * T)
    lanes = lambda col: jnp.broadcast_to(col.reshape(-1, 1), (col.size, GROUP_W))
    qdec = jnp.exp((t + 1.0)[None, :] * log_gamma[:, None])
    kdec = jnp.where(t[None, :] < n_valid, jnp.exp((n_valid - 1.0 - t)[None, :] * log_gamma[:, None]), 0.0)
    sdec = jnp.repeat(jnp.exp(n_valid * log_gamma), HEAD_DIM)
    return cos, sin, dmask, lanes(qdec), lanes(kdec), lanes(sdec)


def _ret_mix(p, col0, s0_bd, tables, B, L):
    T = RET_CHUNK
    assert L % T == 0
    nc = L // T
    HT = N_HEADS * T
    blk = lambda j: pl.BlockSpec((T, GROUP_W), lambda b, c: (b * nc + c, col0 + j))
    tab = pl.BlockSpec((T, GROUP_W), lambda b, c: (c, 0))
    fixed = lambda shape: pl.BlockSpec(shape, lambda b, c: (0,) * len(shape))
    state = lambda: pl.BlockSpec((1, GROUP_W, GROUP_W), lambda b, c: (b, 0, 0))
    return pl.pallas_call(
        _ret_body,
        grid=(B, nc),
        in_specs=[blk(0), blk(1), blk(2), blk(3), tab, tab, fixed((HT, HT)), fixed((HT, GROUP_W)),
                  fixed((HT, GROUP_W)), fixed((GROUP_W, GROUP_W)), state()],
        out_specs=[pl.BlockSpec((T, GROUP_W), lambda b, c: (b * nc + c, 0)), state()],
        out_shape=[jax.ShapeDtypeStruct((B * L, GROUP_W), F32), jax.ShapeDtypeStruct((B, GROUP_W, GROUP_W), F32)],
        scratch_shapes=[pltpu.VMEM((GROUP_W, GROUP_W), F32)],
        compiler_params=pltpu.CompilerParams(
            dimension_semantics=("arbitrary", "arbitrary"), vmem_limit_bytes=V7X_VMEM_LIMIT_BYTES),
        name="retention_mix",
    )(p, p, p, p, *tables, s0_bd)


def _layernorm(x, g, b):
    xf = _f32(x)
    mu = jnp.mean(xf, -1, keepdims=True)
    xc = xf - mu
    var = jnp.mean(xc * xc, -1, keepdims=True)
    return (xc * lax.rsqrt(var + LN_EPS) * _f32(g) + _f32(b)).astype(x.dtype)


def _rms(x, eps=NORM_EPS):
    return x * lax.rsqrt(jnp.mean(x * x, -1, keepdims=True) + eps)


def _head_ln(x, eps):
    xc = x - jnp.mean(x, -1, keepdims=True)
    return xc * lax.rsqrt(jnp.mean(xc * xc, -1, keepdims=True) + eps)


def _rotary(x, pos):
    half = x.shape[-1] // 2
    freq = 1.0 / (ROPE_BASE ** jnp.linspace(0.0, 1.0, half, dtype=jnp.float32))
    ang = _f32(pos)[:, None] * freq[None, :]
    cos = jnp.cos(ang)[None, :, None, :]
    sin = jnp.sin(ang)[None, :, None, :]
    x1, x2 = x[..., :half], x[..., half:]
    return jnp.concatenate([x1 * cos - x2 * sin, x1 * sin + x2 * cos], -1)


def _chunk_scan(step, xs, s0):
    B, L = xs[0].shape[:2]
    c = CHUNK if L % CHUNK == 0 else L
    n = L // c
    chunked = tuple(jnp.moveaxis(a.reshape(B, n, c, *a.shape[2:]), 1, 0) for a in xs)
    s, out = lax.scan(step, s0, chunked)
    return jnp.moveaxis(out, 0, 1).reshape(B, L, *out.shape[3:]), s


def _retention_step(S, xs, log_gamma):
    q, k, v = xs
    C = q.shape[1]
    t = jnp.arange(C, dtype=jnp.float32)
    gap = t[:, None] - t[None, :]
    dmask = jnp.where(gap >= 0, jnp.exp(jnp.maximum(gap, 0.0)[None] * log_gamma[:, None, None]), 0.0)
    scores = jnp.einsum('bqhd,bkhd->bhqk', q, k) * dmask[None]
    o = jnp.einsum('bhqk,bkhe->bqhe', scores, v)
    o = o + jnp.einsum('bqhd,bhde->bqhe', q, S) * jnp.exp((t + 1.0)[:, None] * log_gamma[None, :])[None, :, :, None]
    k_w = k * jnp.exp((C - 1.0 - t)[:, None] * log_gamma[None, :])[None, :, :, None]
    S = jnp.exp(C * log_gamma)[None, :, None, None] * S + jnp.einsum('bkhd,bkhe->bhde', k_w, v)
    return S, o


def _hgrn2_step(S, xs):
    q, k, logf, v = xs
    C = q.shape[1]
    b = jnp.cumsum(logf, axis=1)
    causal = jnp.tril(jnp.ones((C, C), bool))[None, :, :, None, None]
    diff = jnp.where(causal, b[:, :, None] - b[:, None, :], 0.0)
    decay = jnp.where(causal, jnp.exp(diff), 0.0)
    attn = jnp.einsum('bqhd,bqkhd,bkhd->bhqk', q, decay, k)
    o = jnp.einsum('bhqk,bkhe->bqhe', attn, v) + jnp.einsum('bqhd,bhde->bqhe', q * jnp.exp(b), S)
    b_last = b[:, -1]
    S = jnp.exp(b_last)[..., None] * S + jnp.einsum('bkhd,bkhe->bhde', k * jnp.exp(b_last[:, None] - b), v)
    return S, o


def _rwkv_scan(S0, r, w, k, v, a, b):
    def step(S, xs):
        r_t, w_t, k_t, v_t, a_t, b_t = xs
        S = (S * w_t[:, :, None, :]
             + jnp.einsum('bhvk,bhk->bhv', S, a_t)[..., None] * b_t[:, :, None, :]
             + v_t[..., None] * k_t[:, :, None, :])
        return S, jnp.einsum('bhvk,bhk->bhv', S, r_t)
    xs = tuple(jnp.moveaxis(t, 1, 0) for t in (r, w, k, v, a, b))
    S, y = lax.scan(step, S0, xs)
    return jnp.moveaxis(y, 0, 1), S


def _rwkv7(p_rw, prev_row, S0, l, W):
    B, L, _ = p_rw.shape
    p = _f32(p_rw)
    prev = jnp.concatenate([_f32(prev_row)[:, None], p[:, :-1]], axis=1)
    xs = p + (prev - p) * _f32(W['rw_mu'][l])
    r, k, v, wd, ad, gd = jnp.split(xs, RW_SPLITS, axis=-1)
    w_log = -jax.nn.softplus(-(_f32(W['rw_w0'][l]) + jnp.tanh(wd) @ _f32(W['rw_w_up'][l]))) - 0.5
    decay = jnp.exp(-jnp.exp(w_log))
    a = jax.nn.sigmoid(_f32(W['rw_a0'][l]) + ad @ _f32(W['rw_a_up'][l]))
    g = jax.nn.sigmoid(gd) @ _f32(W['rw_g_up'][l])
    hd = lambda t: t.reshape(B, L, N_HEADS, HEAD_DIM)
    kk = hd(k * _f32(W['rw_k_k'][l]))
    kk = kk / jnp.maximum(jnp.sqrt(jnp.sum(kk * kk, -1, keepdims=True)), 1e-12)
    k = hd(k * (1.0 + (a - 1.0) * _f32(W['rw_k_a'][l])))
    r, v, a_h = hd(r), hd(v), hd(a)
    y, S = _rwkv_scan(_f32(S0), r, hd(decay), k, v, -kk, kk * a_h)
    y = _head_ln(y, RW_LNX_EPS).reshape(B, L, GROUP_W) * _f32(W['rw_lnx_w'][l]) + _f32(W['rw_lnx_b'][l])
    bonus = (jnp.sum(r * k * _f32(W['rw_r_k'][l]), -1, keepdims=True) * v).reshape(B, L, GROUP_W)
    return (y + bonus) * g, S, p_rw[:, -1]


def _diff_attn_sample(q, k, v, k_past, v_past, lam):
    L = q.shape[1]
    P = k_past.shape[1]
    scale = DA_DQK ** -0.5
    s_past = jnp.einsum('bqhmd,bkhmd->bhmqk', q, k_past) * scale
    s_new = jnp.einsum('bqhmd,bkhmd->bhmqk', q, k) * scale
    s_new = jnp.where(jnp.tril(jnp.ones((L, L), bool)), s_new, MASK_NEG)
    p = jax.nn.softmax(jnp.concatenate([_f32(s_past), s_new], -1), axis=-1)
    pd = p[:, :, 0] - lam * p[:, :, 1]
    return (jnp.einsum('bhqk,bkhe->bqhe', pd[..., :P], v_past)
            + jnp.einsum('bhqk,bkhe->bqhe', pd[..., P:], v))


def _layer(l, x, pos, ret_s0, hg_s0, rw_s0, shift0, kv_past, W):
    B, L, _ = x.shape
    dt = x.dtype
    M = B * L
    proj2d, d_k, d_v = _in_proj(x.reshape(M, D_MODEL), W['w_in_bf16'][l], 256)
    proj = proj2d.reshape(B, L, N_COLS)
    rw_wts = _rwkv_weights(W, l)
    shift3 = _f32(shift0)[:, None, :]
    lb_rows = _hg_lb_rows(W['hg_lb'], l)
    hg_norm = jnp.tile(_f32(W['hg_norm_w'][l]), N_HEADS)
    ret_bd0, hg_bd0, rw_bd0 = _state_to_bd(_f32(ret_s0)), _hg_state_to_bd(_f32(hg_s0)), _state_to_bd(_f32(rw_s0))
    p_rw = proj[:, :, W_RET + W_HG:W_RET + W_HG + W_RW]

    lam_init = 0.8 - 0.6 * math.exp(-0.3 * l)
    lam = (jnp.exp(jnp.sum(_f32(W['da_lq1'][l]) * _f32(W['da_lk1'][l])))
           - jnp.exp(jnp.sum(_f32(W['da_lq2'][l]) * _f32(W['da_lk2'][l]))) + lam_init)
    c_da = W_RET + W_HG + W_RW
    if kv_past is None:
        assert L % HG_BLOCK == 0
        o_a, ret_bd = _ret_mix(proj2d, 0, ret_bd0, _ret_tables(pos, RET_CHUNK), B, L)
        o_b, hg_bd = _hgrn_mix(proj2d, W_RET, hg_bd0, lb_rows, hg_norm, l > 0, HG_BLOCK, B, L)
        o_c, rw_bd = _rwkv_mix(proj2d, RW_COL_BLOCK, shift3, rw_bd0, RW_CHUNK, B, L, rw_wts)
        ret_s, hg_s, rw_s = _bd_to_state(ret_bd), _hg_bd_to_state(hg_bd), _bd_to_state(rw_bd)
        norm_scale = jnp.tile(_f32(W['da_norm_w'][l]) * (1.0 - lam_init), N_HEADS)
        o_d = _attn_prompt(proj2d, lam, norm_scale, B, L)
    else:
        heads = lambda t, d: t.reshape(B, L, N_HEADS, d)
        r_q, r_k, r_v, r_g = jnp.split(proj[:, :, :W_RET], 4, axis=-1)
        log_gamma = jnp.log1p(-jnp.exp2(-5.0 - jnp.arange(N_HEADS, dtype=jnp.float32)))
        q = _rotary(heads(r_q, HEAD_DIM), pos)
        k = _rotary(heads(r_k, HEAD_DIM), pos) * HEAD_DIM ** -0.5
        o, ret_s = _chunk_scan(functools.partial(_retention_step, log_gamma=log_gamma),
                               (q, k, heads(r_v, HEAD_DIM)), _f32(ret_s0))
        o_a = (_rms(o).reshape(B, L, GROUP_W) * jax.nn.silu(r_g)).reshape(M, GROUP_W)

        h_q, h_f, h_i, h_g = jnp.split(proj[:, :, W_RET:W_RET + W_HG],
                                       [HG_QW, 2 * HG_QW, 2 * HG_QW + GROUP_W], axis=-1)
        if l == 0:
            log_f = jax.nn.log_sigmoid(h_f)
        else:
            log_f = jnp.logaddexp(lb_rows[0], lb_rows[1] + jax.nn.log_sigmoid(h_f))
        k_in = (lb_rows[2] if l > 0 else 1.0) * jax.nn.sigmoid(-h_f)
        q_h = jax.nn.silu(h_q) * HG_DK ** -0.5
        o, hg_s = _chunk_scan(_hgrn2_step, (heads(q_h, HG_DK), heads(k_in, HG_DK), heads(log_f, HG_DK),
                                            heads(h_i, HEAD_DIM)), _f32(hg_s0))
        o_b = ((_rms(o) * _f32(W['hg_norm_w'][l])).reshape(B, L, GROUP_W) * jax.nn.silu(h_g)).reshape(M, GROUP_W)

        o_c, rw_s, _ = _rwkv7(p_rw, shift0, rw_s0, l, W)
        o_c = o_c.reshape(M, GROUP_W)

        k_past, v_past = kv_past
        q5 = proj[:, :, c_da:c_da + GROUP_W].reshape(B, L, N_HEADS, 2, DA_DQK)
        o = _diff_attn_sample(q5, d_k.reshape(B, L, N_HEADS, 2, DA_DQK), d_v.reshape(B, L, N_HEADS, HEAD_DIM),
                              k_past.reshape(B, -1, N_HEADS, 2, DA_DQK), v_past, lam)
        o_d = (_rms(o) * _f32(W['da_norm_w'][l]) * (1.0 - lam_init)).reshape(M, GROUP_W)
    shift_new = p_rw[:, -1]

    x1 = _out_proj_ln((o_a, o_b, o_c, o_d), W['w_o_bf16'][l], x.reshape(M, D_MODEL),
                      W['ln1_g'][l], W['ln1_b'][l], 256)
    x = _moe_ln(x1, l, W).reshape(B, L, D_MODEL)
    new = (ret_s.astype(dt), hg_s.astype(dt), rw_s.astype(dt), shift_new,
           d_k.reshape(B, L, N_HEADS, HEAD_DIM), d_v.reshape(B, L, N_HEADS, HEAD_DIM))
    return x, new


def _prepare_weights(W):
    W = dict(W)
    depth = W['w_in'].shape[0]
    W['w_in_bf16'] = W['w_in'].astype(BF16)
    W['w_o_bf16'] = W['w_o'].astype(BF16)
    router_pad = jnp.zeros((depth, D_MODEL, 128 - N_EXPERTS), W['router_w'].dtype)
    W['gu_router_bf16'] = jnp.concatenate([W['sh_gate'], W['sh_up'], W['router_w'], router_pad], -1).astype(BF16)
    W['sh_down_bf16'] = W['sh_down'].astype(BF16)
    return W


def kernel(x_prompt, x_sample, state_ret, state_hgrn, state_rwkv, state_rwkv_shift, cache_k, cache_v,
           page_table, w_in, w_o, hg_lb, hg_norm_w, rw_mu, rw_w0, rw_w_up, rw_a0, rw_a_up, rw_g_up,
           rw_k_k, rw_k_a, rw_r_k, rw_lnx_w, rw_lnx_b, da_lq1, da_lk1, da_lq2, da_lk2, da_norm_w,
           ln1_g, ln1_b, router_w, router_bias, e_gate, e_up, e_down, sh_gate, sh_up, sh_down,
           ln2_g, ln2_b):
    W = {'w_in': w_in, 'w_o': w_o, 'hg_lb': hg_lb, 'hg_norm_w': hg_norm_w, 'rw_mu': rw_mu,
         'rw_w0': rw_w0, 'rw_w_up': rw_w_up, 'rw_a0': rw_a0, 'rw_a_up': rw_a_up, 'rw_g_up': rw_g_up,
         'rw_k_k': rw_k_k, 'rw_k_a': rw_k_a, 'rw_r_k': rw_r_k, 'rw_lnx_w': rw_lnx_w, 'rw_lnx_b': rw_lnx_b,
         'da_lq1': da_lq1, 'da_lk1': da_lk1, 'da_lq2': da_lq2, 'da_lk2': da_lk2, 'da_norm_w': da_norm_w,
         'ln1_g': ln1_g, 'ln1_b': ln1_b, 'router_w': router_w, 'router_bias': router_bias,
         'e_gate': e_gate, 'e_up': e_up, 'e_down': e_down, 'sh_gate': sh_gate, 'sh_up': sh_up,
         'sh_down': sh_down, 'ln2_g': ln2_g, 'ln2_b': ln2_b}
    W = _prepare_weights(W)
    B, S, _ = x_prompt.shape
    DB, L, _ = x_sample.shape
    past_len = page_table.shape[1] * PAGE_SIZE
    pos_p = jnp.arange(S)
    pos_s = past_len + jnp.arange(L)
    zero_ret = jnp.zeros((B, N_HEADS, HEAD_DIM, HEAD_DIM), jnp.float32)
    zero_hg = jnp.zeros((B, N_HEADS, HG_DK, HEAD_DIM), jnp.float32)
    zero_shift = jnp.zeros((B, W_RW), x_prompt.dtype)
    yp, ys = x_prompt, x_sample
    new_p, new_s = [], []
    for l in range(DEPTH):
        yp, st = _layer(l, yp, pos_p, zero_ret, zero_hg, zero_ret, zero_shift, None, W)
        new_p.append(st)
        k_past = cache_k[l][page_table].reshape(DB, past_len, N_HEADS, HEAD_DIM)
        v_past = cache_v[l][page_table].reshape(DB, past_len, N_HEADS, HEAD_DIM)
        ys, st = _layer(l, ys, pos_s, state_ret[l], state_hgrn[l], state_rwkv[l], state_rwkv_shift[l],
                        (k_past, v_past), W)
        new_s.append(st)

    def stk(sts, i):
        return jnp.stack([s[i] for s in sts])

    return (yp, ys, stk(new_p, 0), stk(new_s, 0), stk(new_p, 1), stk(new_s, 1), stk(new_p, 2), stk(new_s, 2),
            stk(new_p, 3), stk(new_s, 3), stk(new_p, 4), stk(new_p, 5), stk(new_s, 4), stk(new_s, 5))
```
